```python
import jax, jax.numpy as jnp
from jax import lax
import numpy as np

D_MODEL = 2048
BATCH = 1
SEQ = 16384
DEPTH = 2

NH_MLSTM = 4
DQK_MLSTM = 128
DV_MLSTM = 256
MLSTM_CHUNK = 128
GATE_SOFTCAP = 15.0
NH_MOBA = 8
DH_MOBA = 128
MOBA_BLOCK = 256
MOBA_TOPK = 3
MOBA_Q_CHUNK = 64
D_FF = 4 * D_MODEL
EPS = 1e-6

W_MLSTM = NH_MLSTM * DV_MLSTM
W_MOBA = NH_MOBA * DH_MOBA
IN_SIZES = (NH_MLSTM * DQK_MLSTM, NH_MLSTM * DQK_MLSTM, W_MLSTM, W_MLSTM, 2 * NH_MLSTM,
            W_MOBA, W_MOBA, W_MOBA, 2 * D_MODEL)
IN_COLS = int(sum(IN_SIZES))
IN_SPLITS = tuple(int(s) for s in np.cumsum(IN_SIZES)[:-1])

kernel_name = "hybrid_mlstm_moba_gated_block"


def rms_norm(x, g):
    xf = x.astype(jnp.float32)
    y = xf * lax.rsqrt(jnp.mean(jnp.square(xf), axis=-1, keepdims=True) + EPS)
    return y * g.astype(jnp.float32)


def soft_cap(a):
    return GATE_SOFTCAP * jnp.tanh(a / GATE_SOFTCAP)


def alibi_slopes(n_heads):
    return 2.0 ** (-8.0 * jnp.arange(1, n_heads + 1, dtype=jnp.float32) / n_heads)


def mlstm_chunkwise(q, k, v, ig, lf):
    B, H, S, dk = q.shape
    dv = v.shape[-1]
    L = MLSTM_CHUNK
    nc = S // L
    q = q * (dk ** -0.5)

    def to_chunks(a):
        return jnp.moveaxis(a.reshape((B, H, nc, L) + a.shape[3:]), 2, 0)

    causal = jnp.tril(jnp.ones((L, L), dtype=bool))

    def body(carry, xs):
        C, n, m = carry
        qc, kc, vc, igc, lfc = xs
        b = jnp.cumsum(lfc, axis=-1)
        log_d = b[..., :, None] - b[..., None, :] + igc[..., None, :]
        log_d = jnp.where(causal, log_d, -jnp.inf)
        m_inter = b + m[..., None]
        m_row = jnp.maximum(m_inter, jnp.max(log_d, axis=-1))
        s = jnp.einsum('bhld,bhsd->bhls', qc, kc) * jnp.exp(log_d - m_row[..., None])
        inter = jnp.exp(m_inter - m_row)
        num = jnp.einsum('bhls,bhsv->bhlv', s, vc) + inter[..., None] * jnp.einsum('bhld,bhdv->bhlv', qc, C)
        den = jnp.sum(s, axis=-1) + inter * jnp.einsum('bhld,bhd->bhl', qc, n)
        h = num / jnp.maximum(jnp.abs(den), jnp.exp(-m_row))[..., None]
        b_last = b[..., -1]
        log_w = b_last[..., None] - b + igc
        m_new = jnp.maximum(b_last + m, jnp.max(log_w, axis=-1))
        w = jnp.exp(log_w - m_new[..., None])
        decay = jnp.exp(b_last + m - m_new)
        C_new = decay[..., None, None] * C + jnp.einsum('bhs,bhsd,bhsv->bhdv', w, kc, vc)
        n_new = decay[..., None] * n + jnp.einsum('bhs,bhsd->bhd', w, kc)
        return (C_new, n_new, m_new), h

    carry0 = (jnp.zeros((B, H, dk, dv), jnp.float32), jnp.zeros((B, H, dk), jnp.float32),
              jnp.zeros((B, H), jnp.float32))
    _, h = lax.scan(body, carry0, (to_chunks(q), to_chunks(k), to_chunks(v), to_chunks(ig), to_chunks(lf)))
    return jnp.moveaxis(h, 0, 2).reshape(B, H, S, dv)


def moba_attention(q, k, v):
    B, H, S, dh = q.shape
    n_blk = -(-S // MOBA_BLOCK)
    s_pad = n_blk * MOBA_BLOCK
    pad = ((0, 0), (0, 0), (0, s_pad - S), (0, 0))
    q, k, v = jnp.pad(q, pad), jnp.pad(k, pad), jnp.pad(v, pad)
    kb = k.reshape(B, H, n_blk, MOBA_BLOCK, dh)
    vb = v.reshape(B, H, n_blk, MOBA_BLOCK, dh)
    k_mean = jnp.mean(kb, axis=3)
    q_blk = jnp.arange(s_pad) // MOBA_BLOCK
    fully_past = jnp.arange(n_blk)[None, :] < q_blk[:, None]
    gate = jnp.where(fully_past, jnp.einsum('bhtd,bhnd->bhtn', q, k_mean), -jnp.inf)
    top_k = min(MOBA_TOPK, n_blk)
    _, sel = lax.top_k(gate, top_k)
    slopes = alibi_slopes(H)[None, :, None, None]
    scale = dh ** -0.5
    n_qc = s_pad // MOBA_Q_CHUNK
    q_c = jnp.moveaxis(q.reshape(B, H, n_qc, MOBA_Q_CHUNK, dh), 2, 0)
    sel_c = jnp.moveaxis(sel.reshape(B, H, n_qc, MOBA_Q_CHUNK, top_k), 2, 0)
    bi = jnp.arange(B)[:, None, None, None]
    hi = jnp.arange(H)[None, :, None, None]
    blk_pos = jnp.arange(MOBA_BLOCK)

    def one_chunk(args):
        c, q_i, s_i = args
        t = c * MOBA_Q_CHUNK + jnp.arange(MOBA_Q_CHUNK)
        own = (c * MOBA_Q_CHUNK) // MOBA_BLOCK
        k_sel = kb[bi, hi, s_i]
        v_sel = vb[bi, hi, s_i]
        pos_sel = s_i[..., None] * MOBA_BLOCK + blk_pos
        dist_sel = (t[:, None, None] - pos_sel).astype(jnp.float32)
        s_sel = jnp.einsum('bhtd,bhtjsd->bhtjs', q_i, k_sel) * scale - slopes[..., None] * dist_sel
        s_sel = jnp.where((s_i < own)[..., None], s_sel, -jnp.inf)
        k_own = lax.dynamic_index_in_dim(kb, own, axis=2, keepdims=False)
        v_own = lax.dynamic_index_in_dim(vb, own, axis=2, keepdims=False)
        dist_own = (t[:, None] - (own * MOBA_BLOCK + blk_pos)[None, :]).astype(jnp.float32)
        s_own = jnp.einsum('bhtd,bhsd->bhts', q_i, k_own) * scale - slopes * dist_own
        s_own = jnp.where(dist_own >= 0, s_own, -jnp.inf)
        logits = jnp.concatenate([s_sel.reshape(B, H, MOBA_Q_CHUNK, top_k * MOBA_BLOCK), s_own], axis=-1)
        p = jax.nn.softmax(logits, axis=-1)
        p_sel = p[..., :top_k * MOBA_BLOCK].reshape(B, H, MOBA_Q_CHUNK, top_k, MOBA_BLOCK)
        p_own = p[..., top_k * MOBA_BLOCK:]
        return (jnp.einsum('bhtjs,bhtjsd->bhtd', p_sel, v_sel)
                + jnp.einsum('bhts,bhsd->bhtd', p_own, v_own))

    out = lax.map(one_chunk, (jnp.arange(n_qc), q_c, sel_c))
    return jnp.moveaxis(out, 0, 2).reshape(B, H, s_pad, dh)[:, :, :S]


def setup_inputs(seed: int = 0) -> dict:
    key = jax.random.key(seed)
    ks = jax.random.split(key, 16)

    def nrm(k, shape, scale):
        return jax.random.normal(k, shape, jnp.float32) * scale

    gate_offset = jnp.concatenate([jnp.zeros((NH_MLSTM,), jnp.float32), 3.0 * jnp.ones((NH_MLSTM,), jnp.float32)])
    return {
        "x": nrm(ks[0], (BATCH, SEQ, D_MODEL), 1.0),
        "norm_mix": 1.0 + nrm(ks[1], (DEPTH, D_MODEL), 0.02),
        "w_in": nrm(ks[2], (DEPTH, D_MODEL, IN_COLS), D_MODEL ** -0.5),
        "b_if": gate_offset + nrm(ks[3], (DEPTH, 2 * NH_MLSTM), 0.1),
        "norm_h_mlstm": 1.0 + nrm(ks[4], (DEPTH, W_MLSTM), 0.02),
        "norm_q_moba": 1.0 + nrm(ks[5], (DEPTH, DH_MOBA), 0.02),
        "norm_k_moba": 1.0 + nrm(ks[6], (DEPTH, DH_MOBA), 0.02),
        "w_branch_a": nrm(ks[7], (DEPTH, W_MLSTM, D_MODEL), W_MLSTM ** -0.5),
        "w_branch_b": nrm(ks[8], (DEPTH, W_MOBA, D_MODEL), W_MOBA ** -0.5),
        "w_out": nrm(ks[9], (DEPTH, D_MODEL, D_MODEL), D_MODEL ** -0.5),
        "norm_mlp": 1.0 + nrm(ks[10], (DEPTH, D_MODEL), 0.02),
        "w_up": nrm(ks[11], (DEPTH, D_MODEL, D_FF), D_MODEL ** -0.5),
        "w_down": nrm(ks[12], (DEPTH, D_FF, D_MODEL), D_FF ** -0.5),
    }


def reference(x, norm_mix, w_in, b_if, norm_h_mlstm, norm_q_moba, norm_k_moba,
              w_branch_a, w_branch_b, w_out, norm_mlp, w_up, w_down):
    dt = x.dtype
    B, S, _ = x.shape
    f32 = jnp.float32

    def heads(a, nh):
        return a.astype(f32).reshape(B, S, nh, -1).transpose(0, 2, 1, 3)

    for l in range(DEPTH):
        xn = rms_norm(x, norm_mix[l]).astype(dt)
        proj = xn @ w_in[l]
        qa, ka, va, oa, if_pre, qb, kb, vb, gates = jnp.split(proj, IN_SPLITS, axis=-1)

        if_pre = soft_cap(if_pre.astype(f32) + b_if[l].astype(f32))
        ig = if_pre[..., :NH_MLSTM].transpose(0, 2, 1)
        lf = jax.nn.log_sigmoid(if_pre[..., NH_MLSTM:]).transpose(0, 2, 1)
        h_a = mlstm_chunkwise(heads(qa, NH_MLSTM), heads(ka, NH_MLSTM), heads(va, NH_MLSTM), ig, lf)
        h_a = rms_norm(h_a.transpose(0, 2, 1, 3), norm_h_mlstm[l].reshape(NH_MLSTM, DV_MLSTM))
        h_a = (h_a.reshape(B, S, W_MLSTM) * jax.nn.sigmoid(oa.astype(f32))).astype(dt)

        qh = rms_norm(heads(qb, NH_MOBA), norm_q_moba[l])
        kh = rms_norm(heads(kb, NH_MOBA), norm_k_moba[l])
        h_b = moba_attention(qh, kh, heads(vb, NH_MOBA))
        h_b = h_b.transpose(0, 2, 1, 3).reshape(B, S, W_MOBA).astype(dt)

        g = jax.nn.sigmoid(gates.astype(f32))
        merged = (g[..., :D_MODEL] * (h_a @ w_branch_a[l]).astype(f32)
                  + g[..., D_MODEL:] * (h_b @ w_branch_b[l]).astype(f32)).astype(dt)
        x = x + merged @ w_out[l]

        hn = rms_norm(x, norm_mlp[l]).astype(dt)
        x = x + jnp.square(jax.nn.relu(hn @ w_up[l])) @ w_down[l]
    return x
```

```python
import functools

import jax
import jax.numpy as jnp
from jax import lax
from jax.experimental import pallas as pl
from jax.experimental.pallas import tpu as pltpu

F32 = jnp.float32
BF16 = jnp.bfloat16

NH_A, DQK_A, DV_A, CHUNK_A = 4, 128, 256, 128
GATE_SOFTCAP = 15.0
NH_B, DH_B, BLOCK_B, TOPK_B = 8, 128, 256, 3
EPS = 1e-6

W_A = NH_A * DV_A
W_B = NH_B * DH_B
QK_A = NH_A * DQK_A
N_IF = 2 * NH_A
LANE = 128
IF_PAD = LANE
NEG_BIG = -1e30
VMEM_LIMIT = 56 * 1024 * 1024

C_QA, C_KA, C_VA, C_OA = 0, QK_A, 2 * QK_A, 2 * QK_A + W_A
C_QB = C_OA + W_A
C_KB = C_QB + W_B
C_VB = C_KB + W_B
C_G = C_VB + W_B


def _cparams(sem):
    return pltpu.CompilerParams(dimension_semantics=sem, vmem_limit_bytes=VMEM_LIMIT)


def _rms(x, gain):
    ms = jnp.mean(x * x, axis=-1, keepdims=True)
    return x * lax.rsqrt(ms + EPS) * gain


def _in_proj_kernel(x_ref, g_ref, w_ref, wif_ref, bif_ref, o_ref, oif_ref, xn_ref):
    @pl.when(pl.program_id(1) == 0)
    def _():
        xn = _rms(x_ref[...], g_ref[...]).astype(BF16)
        xn_ref[...] = xn
        oif_ref[...] = jnp.dot(xn, wif_ref[...], preferred_element_type=F32) + bif_ref[...]

    o_ref[...] = jnp.dot(xn_ref[...], w_ref[...], preferred_element_type=F32).astype(o_ref.dtype)


def _in_proj(x, gain, w_main, w_if, b_if, tm, tn):
    S, D = x.shape
    N = w_main.shape[1]
    return pl.pallas_call(
        _in_proj_kernel,
        grid=(S // tm, N // tn),
        in_specs=[
            pl.BlockSpec((tm, D), lambda m, n: (m, 0)),
            pl.BlockSpec((1, D), lambda m, n: (0, 0)),
            pl.BlockSpec((D, tn), lambda m, n: (0, n)),
            pl.BlockSpec((D, IF_PAD), lambda m, n: (0, 0)),
            pl.BlockSpec((1, IF_PAD), lambda m, n: (0, 0)),
        ],
        out_specs=[
            pl.BlockSpec((tm, tn), lambda m, n: (m, n)),
            pl.BlockSpec((tm, IF_PAD), lambda m, n: (m, 0)),
        ],
        out_shape=[
            jax.ShapeDtypeStruct((S, N), BF16),
            jax.ShapeDtypeStruct((S, IF_PAD), F32),
        ],
        scratch_shapes=[pltpu.VMEM((tm, D), BF16)],
        compiler_params=_cparams(("parallel", "arbitrary")),
        name="in_proj",
    )(x, gain, w_main, w_if, b_if)


def _mlstm_kernel(q_ref, k_ref, v_ref, oa_ref, if_ref, gain_ref, o_ref, c_ref, m_ref):
    L = CHUNK_A
    DVX = DV_A + LANE

    @pl.when(pl.program_id(0) == 0)
    def _():
        c_ref[...] = jnp.zeros_like(c_ref)
        m_ref[...] = jnp.zeros_like(m_ref)

    g = if_ref[...]
    gc = GATE_SOFTCAP * jnp.tanh(g / GATE_SOFTCAP)
    col = lax.broadcasted_iota(jnp.int32, (L, LANE), 1)
    row = lax.broadcasted_iota(jnp.int32, (L, LANE), 0)
    log_f = jnp.minimum(gc, 0.0) - jnp.log1p(jnp.exp(-jnp.abs(gc)))
    G = jnp.where(col < NH_A, gc, log_f)
    tril = (row >= col).astype(F32)
    Bc = jnp.dot(tril, G, preferred_element_type=F32, precision=lax.Precision.HIGHEST)
    RT = (G - pltpu.roll(Bc, LANE - NH_A, 1)).T
    causal = row >= col
    scale = DQK_A ** -0.5
    ones_col = jnp.where(col == 0, 1.0, 0.0).astype(BF16)

    for h in range(NH_A):
        qh = q_ref[:, h * DQK_A:(h + 1) * DQK_A]
        kh = k_ref[:, h * DQK_A:(h + 1) * DQK_A]
        vh = v_ref[:, h * DV_A:(h + 1) * DV_A]
        vext = jnp.concatenate([vh, ones_col], axis=1)
        b_col = Bc[:, NH_A + h:NH_A + h + 1]
        b_last = Bc[L - 1:L, NH_A + h:NH_A + h + 1]
        ig_col = G[:, h:h + 1]
        m_prev = m_ref[h:h + 1, 0:1]
        c_prev = c_ref[h]

        log_d = jnp.where(causal, b_col + RT[h:h + 1, :], -jnp.inf)
        m_inter = b_col + m_prev
        m_row = jnp.maximum(m_inter, jnp.max(log_d, axis=1, keepdims=True))
        dmat = jnp.exp(log_d - m_row)
        s = lax.dot_general(qh, kh, (((1,), (1,)), ((), ())), preferred_element_type=F32)
        sd = (s * (dmat * scale)).astype(BF16)
        inter = jnp.exp(m_inter - m_row) * scale
        num_ext = (jnp.dot(sd, vext, preferred_element_type=F32)
                   + inter * jnp.dot(qh, c_prev.astype(BF16), preferred_element_type=F32))
        num = num_ext[:, :DV_A]
        den = num_ext[:, DV_A:DV_A + 1]
        hval = num / jnp.maximum(jnp.abs(den), jnp.exp(-m_row))

        hn = _rms(hval, gain_ref[:, h * DV_A:(h + 1) * DV_A])
        og = jax.nn.sigmoid(oa_ref[:, h * DV_A:(h + 1) * DV_A].astype(F32))
        o_ref[:, h * DV_A:(h + 1) * DV_A] = (hn * og).astype(o_ref.dtype)

        log_w = b_last - b_col + ig_col
        m_new = jnp.maximum(b_last + m_prev, jnp.max(log_w, axis=0, keepdims=True))
        w = jnp.exp(log_w - m_new)
        decay = jnp.exp(b_last + m_prev - m_new)
        wv = (w * vext.astype(F32)).astype(BF16)
        c_ref[h] = decay * c_prev + lax.dot_general(
            kh, wv, (((0,), (0,)), ((), ())), preferred_element_type=F32)
        m_ref[h:h + 1, :] = jnp.broadcast_to(m_new, (1, LANE))


def _mlstm(proj, ifp, gain_h):
    S = proj.shape[0]
    L = CHUNK_A
    return pl.pallas_call(
        _mlstm_kernel,
        grid=(S // L,),
        in_specs=[
            pl.BlockSpec((L, QK_A), lambda c: (c, C_QA // QK_A)),
            pl.BlockSpec((L, QK_A), lambda c: (c, C_KA // QK_A)),
            pl.BlockSpec((L, W_A), lambda c: (c, C_VA // W_A)),
            pl.BlockSpec((L, W_A), lambda c: (c, C_OA // W_A)),
            pl.BlockSpec((L, IF_PAD), lambda c: (c, 0)),
            pl.BlockSpec((1, W_A), lambda c: (0, 0)),
        ],
        out_specs=pl.BlockSpec((L, W_A), lambda c: (c, 0)),
        out_shape=jax.ShapeDtypeStruct((S, W_A), BF16),
        scratch_shapes=[
            pltpu.VMEM((NH_A, DQK_A, DV_A + LANE), F32),
            pltpu.VMEM((8, LANE), F32),
        ],
        compiler_params=_cparams(("arbitrary",)),
        name="mlstm",
    )(proj, proj, proj, proj, ifp, gain_h)


def _alibi_slope(h):
    return 2.0 ** (-8.0 * (h + 1) / NH_B)


def _moba_prep_kernel(q_ref, k_ref, gq_ref, gk_ref, qs_ref, kaug_ref, kmean_ref, *, n_blk):
    i = pl.program_id(0)
    T = BLOCK_B
    scale = DH_B ** -0.5
    lane = lax.broadcasted_iota(jnp.int32, (T, LANE), 1)
    pos = lax.broadcasted_iota(jnp.int32, (T, LANE), 0).astype(F32)
    blk_f = i.astype(F32)
    row_is_blk = lax.broadcasted_iota(jnp.int32, (LANE, DH_B), 0) == i

    @pl.when(i == 0)
    def _():
        kmean_ref[...] = jnp.zeros_like(kmean_ref)

    for h in range(NH_B):
        sl = slice(h * DH_B, (h + 1) * DH_B)
        qn = _rms(q_ref[:, sl].astype(F32), gq_ref[...])
        qs_ref[:, sl] = (qn * scale).astype(qs_ref.dtype)
        kn = _rms(k_ref[:, sl].astype(F32), gk_ref[...])
        kmean_ref[:, sl] = jnp.where(row_is_blk, jnp.mean(kn, axis=0, keepdims=True), kmean_ref[:, sl])
        slope = _alibi_slope(h)
        extra = jnp.where(lane == i, 1.0, 0.0)
        extra = jnp.where(lane == n_blk, slope * T * blk_f, extra)
        extra = jnp.where(lane == n_blk + 1, slope * pos, extra)
        extra = jnp.where(lane == n_blk + 2, 1.0, extra)
        kaug_ref[h, :, :DH_B] = kn.astype(kaug_ref.dtype)
        kaug_ref[h, :, DH_B:] = extra.astype(kaug_ref.dtype)


def _moba_prep(proj, gq, gk):
    S = proj.shape[0]
    T = BLOCK_B
    n_blk = S // T
    assert n_blk + 3 <= LANE
    kern = functools.partial(_moba_prep_kernel, n_blk=n_blk)
    return pl.pallas_call(
        kern,
        grid=(n_blk,),
        in_specs=[
            pl.BlockSpec((T, W_B), lambda i: (i, C_QB // W_B)),
            pl.BlockSpec((T, W_B), lambda i: (i, C_KB // W_B)),
            pl.BlockSpec((1, DH_B), lambda i: (0, 0)),
            pl.BlockSpec((1, DH_B), lambda i: (0, 0)),
        ],
        out_specs=[
            pl.BlockSpec((T, W_B), lambda i: (i, 0)),
            pl.BlockSpec((NH_B, T, DH_B + LANE), lambda i: (0, i, 0)),
            pl.BlockSpec((LANE, W_B), lambda i: (0, 0)),
        ],
        out_shape=[
            jax.ShapeDtypeStruct((S, W_B), BF16),
            jax.ShapeDtypeStruct((NH_B, S, DH_B + LANE), BF16),
            jax.ShapeDtypeStruct((LANE, W_B), F32),
        ],
        compiler_params=_cparams(("arbitrary",)),
        name="moba_prep",
    )(proj, proj, gq, gk)


def _moba_attn_kernel(qs_ref, kaug_ref, v_ref, kmean_ref, o_ref, m_ref, l_ref, acc_ref, *, n_blk):
    h = pl.program_id(0)
    i = pl.program_id(1)
    T = BLOCK_B
    qs = qs_ref[...]

    gate = lax.dot_general(qs, kmean_ref[...].astype(BF16), (((1,), (1,)), ((), ())),
                           preferred_element_type=F32)
    blk = lax.broadcasted_iota(jnp.int32, (T, LANE), 1)
    valid = blk < i
    g = jnp.where(valid, gate, -jnp.inf)
    sel = jnp.zeros((T, LANE), F32)
    for _ in range(TOPK_B):
        mx = jnp.max(g, axis=1, keepdims=True)
        idx = jnp.min(jnp.where(g == mx, blk, LANE), axis=1, keepdims=True)
        pick = blk == idx
        sel = jnp.where(pick, 1.0, sel)
        g = jnp.where(pick, -jnp.inf, g)
    keep = jnp.where(valid, sel, 0.0) + jnp.where(blk == i, 1.0, 0.0)
    aug = jnp.where(keep > 0.0, 0.0, NEG_BIG)

    slope = jnp.exp2(jnp.full((T, LANE), -8.0 / NH_B, F32) * (h + 1).astype(F32))
    aug = jnp.where(blk >= n_blk, 0.0, aug)
    aug = jnp.where((blk == n_blk) | (blk == n_blk + 1), 1.0, aug)
    aug = jnp.where(blk == n_blk + 2, -slope * (T * i).astype(F32), aug)
    q_aug = jnp.concatenate([qs, aug.astype(BF16)], axis=1)

    def scores(j):
        start = pl.multiple_of(j * T, T)
        kj = kaug_ref[pl.ds(start, T), :]
        return lax.dot_general(q_aug, kj, (((1,), (1,)), ((), ())), preferred_element_type=F32)

    r = lax.broadcasted_iota(jnp.int32, (T, T), 0)
    c = lax.broadcasted_iota(jnp.int32, (T, T), 1)
    s = jnp.where(r >= c, scores(i), -jnp.inf)
    m0 = jnp.max(s, axis=1, keepdims=True)
    p = jnp.exp(s - m0)
    m_ref[...] = m0
    l_ref[...] = jnp.sum(p, axis=1, keepdims=True)
    v_own = v_ref[pl.ds(pl.multiple_of(i * T, T), T), :]
    acc_ref[...] = jnp.dot(p.astype(BF16), v_own, preferred_element_type=F32)

    def body(j, carry):
        s = scores(j)
        m_prev = m_ref[...]
        m_new = jnp.maximum(m_prev, jnp.max(s, axis=1, keepdims=True))
        p = jnp.exp(s - m_new)
        alpha = jnp.exp(m_prev - m_new)
        l_ref[...] = alpha * l_ref[...] + jnp.sum(p, axis=1, keepdims=True)
        vj = v_ref[pl.ds(pl.multiple_of(j * T, T), T), :]
        acc_ref[...] = alpha * acc_ref[...] + jnp.dot(p.astype(BF16), vj, preferred_element_type=F32)
        m_ref[...] = m_new
        return carry

    lax.fori_loop(0, i, body, 0)
    o_ref[...] = (acc_ref[...] / l_ref[...]).astype(o_ref.dtype)


def _moba_attn(qs, kaug, proj, kmean):
    S = qs.shape[0]
    T = BLOCK_B
    n_blk = S // T
    kern = functools.partial(_moba_attn_kernel, n_blk=n_blk)
    return pl.pallas_call(
        kern,
        grid=(NH_B, n_blk),
        in_specs=[
            pl.BlockSpec((T, DH_B), lambda h, i: (i, h)),
            pl.BlockSpec((None, S, DH_B + LANE), lambda h, i: (h, 0, 0)),
            pl.BlockSpec((S, DH_B), lambda h, i: (0, C_VB // DH_B + h)),
            pl.BlockSpec((LANE, DH_B), lambda h, i: (0, h)),
        ],
        out_specs=pl.BlockSpec((T, DH_B), lambda h, i: (i, h)),
        out_shape=jax.ShapeDtypeStruct((S, W_B), BF16),
        scratch_shapes=[
            pltpu.VMEM((T, 1), F32),
            pltpu.VMEM((T, 1), F32),
            pltpu.VMEM((T, DH_B), F32),
        ],
        compiler_params=_cparams(("parallel", "arbitrary")),
        name="moba_attn",
    )(qs, kaug, proj, kmean)


def _merge_out_kernel(x_ref, ha_ref, hb_ref, ga_ref, gb_ref, wa_ref, wb_ref, wo_ref, o_ref):
    ta = jnp.dot(ha_ref[...], wa_ref[...], preferred_element_type=F32)
    tb = jnp.dot(hb_ref[...], wb_ref[...], preferred_element_type=F32)
    merged = (jax.nn.sigmoid(ga_ref[...].astype(F32)) * ta
              + jax.nn.sigmoid(gb_ref[...].astype(F32)) * tb).astype(BF16)
    o_ref[...] = x_ref[...] + jnp.dot(merged, wo_ref[...], preferred_element_type=F32)


def _merge_out(x, h_a, h_b, proj, w_a, w_b, w_o, tm):
    S, D = x.shape
    const = dict(pipeline_mode=pl.Buffered(1))
    return pl.pallas_call(
        _merge_out_kernel,
        grid=(S // tm,),
        in_specs=[
            pl.BlockSpec((tm, D), lambda m: (m, 0)),
            pl.BlockSpec((tm, W_A), lambda m: (m, 0)),
            pl.BlockSpec((tm, W_B), lambda m: (m, 0)),
            pl.BlockSpec((tm, D), lambda m: (m, C_G // D)),
            pl.BlockSpec((tm, D), lambda m: (m, C_G // D + 1)),
            pl.BlockSpec((W_A, D), lambda m: (0, 0), **const),
            pl.BlockSpec((W_B, D), lambda m: (0, 0), **const),
            pl.BlockSpec((D, D), lambda m: (0, 0), **const),
        ],
        out_specs=pl.BlockSpec((tm, D), lambda m: (m, 0)),
        out_shape=jax.ShapeDtypeStruct((S, D), F32),
        compiler_params=_cparams(("parallel",)),
        name="merge_out",
    )(x, h_a, h_b, proj, proj, w_a, w_b, w_o)


def _mlp_kernel(x_ref, g_ref, wu_ref, wd_ref, o_ref, hn_ref):
    @pl.when(pl.program_id(1) == 0)
    def _():
        x = x_ref[...]
        hn_ref[...] = _rms(x, g_ref[...]).astype(BF16)
        o_ref[...] = x

    u = jnp.dot(hn_ref[...], wu_ref[...], preferred_element_type=F32)
    a = jnp.square(jnp.maximum(u, 0.0)).astype(BF16)
    o_ref[...] += jnp.dot(a, wd_ref[...], preferred_element_type=F32)


def _mlp(x, gain, w_up, w_down, tm, tf):
    S, D = x.shape
    FF = w_up.shape[1]
    return pl.pallas_call(
        _mlp_kernel,
        grid=(S // tm, FF // tf),
        in_specs=[
            pl.BlockSpec((tm, D), lambda m, f: (m, 0)),
            pl.BlockSpec((1, D), lambda m, f: (0, 0)),
            pl.BlockSpec((D, tf), lambda m, f: (0, f)),
            pl.BlockSpec((tf, D), lambda m, f: (f, 0)),
        ],
        out_specs=pl.BlockSpec((tm, D), lambda m, f: (m, 0)),
        out_shape=jax.ShapeDtypeStruct((S, D), F32),
        scratch_shapes=[pltpu.VMEM((tm, D), BF16)],
        compiler_params=_cparams(("parallel", "arbitrary")),
        name="mlp",
    )(x, gain, w_up, w_down)


def _tile(n, pref):
    t = min(n, pref)
    assert n % t == 0
    return t


def kernel(x, norm_mix, w_in, b_if, norm_h_mlstm, norm_q_moba, norm_k_moba,
           w_branch_a, w_branch_b, w_out, norm_mlp, w_up, w_down):
    B, S, D = x.shape
    assert B == 1 and S % BLOCK_B == 0 and D % LANE == 0
    depth = w_in.shape[0]
    c_if = C_QB
    n_main = w_in.shape[2] - N_IF
    assert n_main == C_G + 2 * D

    xs = x.reshape(S, D)
    for l in range(depth):
        w_main = jnp.concatenate([w_in[l, :, :c_if], w_in[l, :, c_if + N_IF:]], axis=1).astype(BF16)
        w_if = jnp.pad(w_in[l, :, c_if:c_if + N_IF], ((0, 0), (0, IF_PAD - N_IF))).astype(BF16)
        bias_if = jnp.pad(b_if[l].astype(F32), (0, IF_PAD - N_IF)).reshape(1, IF_PAD)

        proj, ifp = _in_proj(xs, norm_mix[l].reshape(1, D), w_main, w_if, bias_if,
                             _tile(S, 512), _tile(n_main, 1024))
        h_a = _mlstm(proj, ifp, norm_h_mlstm[l].reshape(1, W_A))
        qs, kaug, kmean = _moba_prep(proj, norm_q_moba[l].reshape(1, DH_B),
                                     norm_k_moba[l].reshape(1, DH_B))
        h_b = _moba_attn(qs, kaug, proj, kmean)
        xs = _merge_out(xs, h_a, h_b, proj, w_branch_a[l].astype(BF16),
                        w_branch_b[l].astype(BF16), w_out[l].astype(BF16), _tile(S, 256))
        xs = _mlp(xs, norm_mlp[l].reshape(1, D), w_up[l].astype(BF16), w_down[l].astype(BF16),
                  _tile(S, 512), _tile(w_up.shape[2], 512))
    return xs.reshape(B, S, D)
```

```python
import functools

import jax
import jax.numpy as jnp
from jax import lax
from jax.experimental import pallas as pl
from jax.experimental.pallas import tpu as pltpu

F32 = jnp.float32
BF16 = jnp.bfloat16

NH_A, DQK_A, DV_A, CHUNK_A = 4, 128, 256, 128
GATE_SOFTCAP = 15.0
NH_B, DH_B, BLOCK_B, TOPK_B = 8, 128, 256, 3
EPS = 1e-6

W_A = NH_A * DV_A
W_B = NH_B * DH_B
QK_A = NH_A * DQK_A
N_IF = 2 * NH_A
LANE = 128
IF_PAD = LANE
NEG_BIG = -1e30
VMEM_LIMIT = 56 * 1024 * 1024

C_QA, C_KA, C_VA, C_OA = 0, QK_A, 2 * QK_A, 2 * QK_A + W_A
C_QB = C_OA + W_A
C_KB = C_QB + W_B
C_VB = C_KB + W_B
C_G = C_VB + W_B


def _cparams(sem):
    return pltpu.CompilerParams(dimension_semantics=sem, vmem_limit_bytes=VMEM_LIMIT)


def _rms(x, gain):
    ms = jnp.mean(x * x, axis=-1, keepdims=True)
    return x * lax.rsqrt(ms + EPS) * gain


def _in_proj_kernel(x_ref, g_ref, w_ref, wif_ref, bif_ref, o_ref, oif_ref, xn_ref):
    @pl.when(pl.program_id(1) == 0)
    def _():
        xn = _rms(x_ref[...], g_ref[...]).astype(BF16)
        xn_ref[...] = xn
        oif_ref[...] = jnp.dot(xn, wif_ref[...], preferred_element_type=F32) + bif_ref[...]

    o_ref[...] = jnp.dot(xn_ref[...], w_ref[...], preferred_element_type=F32).astype(o_ref.dtype)


def _in_proj(x, gain, w_main, w_if, b_if, tm, tn):
    S, D = x.shape
    N = w_main.shape[1]
    return pl.pallas_call(
        _in_proj_kernel,
        grid=(S // tm, N // tn),
        in_specs=[
            pl.BlockSpec((tm, D), lambda m, n: (m, 0)),
            pl.BlockSpec((1, D), lambda m, n: (0, 0)),
            pl.BlockSpec((D, tn), lambda m, n: (0, n)),
            pl.BlockSpec((D, IF_PAD), lambda m, n: (0, 0)),
            pl.BlockSpec((1, IF_PAD), lambda m, n: (0, 0)),
        ],
        out_specs=[
            pl.BlockSpec((tm, tn), lambda m, n: (m, n)),
            pl.BlockSpec((tm, IF_PAD), lambda m, n: (m, 0)),
        ],
        out_shape=[
            jax.ShapeDtypeStruct((S, N), BF16),
            jax.ShapeDtypeStruct((S, IF_PAD), F32),
        ],
        scratch_shapes=[pltpu.VMEM((tm, D), BF16)],
        compiler_params=_cparams(("parallel", "arbitrary")),
        name="in_proj",
    )(x, gain, w_main, w_if, b_if)


def _mlstm_kernel(q_ref, k_ref, v_ref, oa_ref, if_ref, gain_ref, o_ref, c_ref, m_ref):
    L = CHUNK_A
    DVX = DV_A + LANE

    @pl.when(pl.program_id(0) == 0)
    def _():
        c_ref[...] = jnp.zeros_like(c_ref)
        m_ref[...] = jnp.zeros_like(m_ref)

    g = if_ref[...]
    gc = GATE_SOFTCAP * jnp.tanh(g / GATE_SOFTCAP)
    col = lax.broadcasted_iota(jnp.int32, (L, LANE), 1)
    row = lax.broadcasted_iota(jnp.int32, (L, LANE), 0)
    log_f = jnp.minimum(gc, 0.0) - jnp.log1p(jnp.exp(-jnp.abs(gc)))
    G = jnp.where(col < NH_A, gc, log_f)
    tril = (row >= col).astype(F32)
    Bc = jnp.dot(tril, G, preferred_element_type=F32, precision=lax.Precision.HIGHEST)
    RT = (G - pltpu.roll(Bc, LANE - NH_A, 1)).T
    causal = row >= col
    scale = DQK_A ** -0.5
    ones_col = jnp.where(col == 0, 1.0, 0.0).astype(BF16)

    for h in range(NH_A):
        qh = q_ref[:, h * DQK_A:(h + 1) * DQK_A]
        kh = k_ref[:, h * DQK_A:(h + 1) * DQK_A]
        vh = v_ref[:, h * DV_A:(h + 1) * DV_A]
        vext = jnp.concatenate([vh, ones_col], axis=1)
        b_col = Bc[:, NH_A + h:NH_A + h + 1]
        b_last = Bc[L - 1:L, NH_A + h:NH_A + h + 1]
        ig_col = G[:, h:h + 1]
        m_prev = m_ref[h:h + 1, 0:1]
        c_prev = c_ref[h]

        log_d = jnp.where(causal, b_col + RT[h:h + 1, :], -jnp.inf)
        m_inter = b_col + m_prev
        m_row = jnp.maximum(m_inter, jnp.max(log_d, axis=1, keepdims=True))
        dmat = jnp.exp(log_d - m_row)
        s = lax.dot_general(qh, kh, (((1,), (1,)), ((), ())), preferred_element_type=F32)
        sd = (s * (dmat * scale)).astype(BF16)
        inter = jnp.exp(m_inter - m_row) * scale
        num_ext = (jnp.dot(sd, vext, preferred_element_type=F32)
                   + inter * jnp.dot(qh, c_prev.astype(BF16), preferred_element_type=F32))
        num = num_ext[:, :DV_A]
        den = num_ext[:, DV_A:DV_A + 1]
        hval = num / jnp.maximum(jnp.abs(den), jnp.exp(-m_row))

        hn = _rms(hval, gain_ref[:, h * DV_A:(h + 1) * DV_A])
        og = jax.nn.sigmoid(oa_ref[:, h * DV_A:(h + 1) * DV_A].astype(F32))
        o_ref[:, h * DV_A:(h + 1) * DV_A] = (hn * og).astype(o_ref.dtype)

        log_w = b_last - b_col + ig_col
        m_new = jnp.maximum(b_last + m_prev, jnp.max(log_w, axis=0, keepdims=True))
        w = jnp.exp(log_w - m_new)
        decay = jnp.exp(b_last + m_prev - m_new)
        wv = (w * vext.astype(F32)).astype(BF16)
        c_ref[h] = decay * c_prev + lax.dot_general(
            kh, wv, (((0,), (0,)), ((), ())), preferred_element_type=F32)
        m_ref[h:h + 1, :] = jnp.broadcast_to(m_new, (1, LANE))


def _mlstm(proj, ifp, gain_h):
    S = proj.shape[0]
    L = CHUNK_A
    return pl.pallas_call(
        _mlstm_kernel,
        grid=(S // L,),
        in_specs=[
            pl.BlockSpec((L, QK_A), lambda c: (c, C_QA // QK_A)),
            pl.BlockSpec((L, QK_A), lambda c: (c, C_KA // QK_A)),
            pl.BlockSpec((L, W_A), lambda c: (c, C_VA // W_A)),
            pl.BlockSpec((L, W_A), lambda c: (c, C_OA // W_A)),
            pl.BlockSpec((L, IF_PAD), lambda c: (c, 0)),
            pl.BlockSpec((1, W_A), lambda c: (0, 0)),
        ],
        out_specs=pl.BlockSpec((L, W_A), lambda c: (c, 0)),
        out_shape=jax.ShapeDtypeStruct((S, W_A), BF16),
        scratch_shapes=[
            pltpu.VMEM((NH_A, DQK_A, DV_A + LANE), F32),
            pltpu.VMEM((8, LANE), F32),
        ],
        compiler_params=_cparams(("arbitrary",)),
        name="mlstm",
    )(proj, proj, proj, proj, ifp, gain_h)


def _alibi_slope(h):
    return 2.0 ** (-8.0 * (h + 1) / NH_B)


def _moba_prep_kernel(q_ref, k_ref, v_ref, gq_ref, gk_ref, qs_ref, kaug_ref, vt_ref, kmean_ref, *, n_blk):
    i = pl.program_id(0)
    T = BLOCK_B
    scale = DH_B ** -0.5
    lane = lax.broadcasted_iota(jnp.int32, (T, LANE), 1)
    pos = lax.broadcasted_iota(jnp.int32, (T, LANE), 0).astype(F32)
    blk_f = i.astype(F32)
    row_is_blk = lax.broadcasted_iota(jnp.int32, (LANE, DH_B), 0) == i

    @pl.when(i == 0)
    def _():
        kmean_ref[...] = jnp.zeros_like(kmean_ref)

    for h in range(NH_B):
        sl = slice(h * DH_B, (h + 1) * DH_B)
        qn = _rms(q_ref[:, sl].astype(F32), gq_ref[...])
        qs_ref[:, sl] = (qn * scale).astype(qs_ref.dtype)
        kn = _rms(k_ref[:, sl].astype(F32), gk_ref[...])
        kmean_ref[:, sl] = jnp.where(row_is_blk, jnp.mean(kn, axis=0, keepdims=True), kmean_ref[:, sl])
        slope = _alibi_slope(h)
        extra = jnp.where(lane == i, 1.0, 0.0)
        extra = jnp.where(lane == n_blk, slope * T * blk_f, extra)
        extra = jnp.where(lane == n_blk + 1, slope * pos, extra)
        extra = jnp.where(lane == n_blk + 2, 1.0, extra)
        kaug_ref[h, :, :DH_B] = kn.astype(kaug_ref.dtype)
        kaug_ref[h, :, DH_B:] = extra.astype(kaug_ref.dtype)
        vt_ref[h] = v_ref[:, sl].astype(F32).T.astype(vt_ref.dtype)


def _moba_prep(proj, gq, gk):
    S = proj.shape[0]
    T = BLOCK_B
    n_blk = S // T
    assert n_blk + 3 <= LANE
    kern = functools.partial(_moba_prep_kernel, n_blk=n_blk)
    return pl.pallas_call(
        kern,
        grid=(n_blk,),
        in_specs=[
            pl.BlockSpec((T, W_B), lambda i: (i, C_QB // W_B)),
            pl.BlockSpec((T, W_B), lambda i: (i, C_KB // W_B)),
            pl.BlockSpec((T, W_B), lambda i: (i, C_VB // W_B)),
            pl.BlockSpec((1, DH_B), lambda i: (0, 0)),
            pl.BlockSpec((1, DH_B), lambda i: (0, 0)),
        ],
        out_specs=[
            pl.BlockSpec((T, W_B), lambda i: (i, 0)),
            pl.BlockSpec((NH_B, T, DH_B + LANE), lambda i: (0, i, 0)),
            pl.BlockSpec((NH_B, None, DH_B, T), lambda i: (0, i, 0, 0)),
            pl.BlockSpec((LANE, W_B), lambda i: (0, 0)),
        ],
        out_shape=[
            jax.ShapeDtypeStruct((S, W_B), BF16),
            jax.ShapeDtypeStruct((NH_B, S, DH_B + LANE), BF16),
            jax.ShapeDtypeStruct((NH_B, n_blk, DH_B, T), BF16),
            jax.ShapeDtypeStruct((LANE, W_B), F32),
        ],
        compiler_params=_cparams(("arbitrary",)),
        name="moba_prep",
    )(proj, proj, proj, gq, gk)


CHUNK_B = 4


def _moba_attn_kernel(qs_ref, kaug_ref, vt_ref, kmean_ref, o_ref,
                      qaug_ref, s_ref, cmax_ref, m_ref, l_ref, acc_ref, *, n_blk):
    h = pl.program_id(0)
    i = pl.program_id(1)
    T = BLOCK_B
    W = CHUNK_B * T
    qs = qs_ref[...]

    gate = lax.dot_general(qs, kmean_ref[...].astype(BF16), (((1,), (1,)), ((), ())),
                           preferred_element_type=F32)
    blk = lax.broadcasted_iota(jnp.int32, (T, LANE), 1)
    valid = blk < i
    g = jnp.where(valid, gate, -jnp.inf)
    sel = jnp.zeros((T, LANE), F32)
    for _ in range(TOPK_B):
        mx = jnp.max(g, axis=1, keepdims=True)
        idx = jnp.min(jnp.where(g == mx, blk, LANE), axis=1, keepdims=True)
        pick = blk == idx
        sel = jnp.where(pick, 1.0, sel)
        g = jnp.where(pick, -jnp.inf, g)
    keep = jnp.where(valid, sel, 0.0) + jnp.where(blk == i, 1.0, 0.0)
    aug = jnp.where(keep > 0.0, 0.0, NEG_BIG)

    slope = jnp.exp2(jnp.full((T, LANE), -8.0 / NH_B, F32) * (h + 1).astype(F32))
    aug = jnp.where(blk >= n_blk, 0.0, aug)
    aug = jnp.where((blk == n_blk) | (blk == n_blk + 1), 1.0, aug)
    aug = jnp.where(blk == n_blk + 2, -slope * (T * i).astype(F32), aug)
    qaug_ref[...] = jnp.concatenate([qs, aug.astype(BF16)], axis=1)

    def chunk_scores(cid):
        kc = kaug_ref[pl.ds(pl.multiple_of(cid * W, W), W), :]
        return lax.dot_general(kc, qaug_ref[...], (((1,), (1,)), ((), ())),
                               preferred_element_type=F32)

    last = i // CHUNK_B
    key = lax.broadcasted_iota(jnp.int32, (W, T), 0)
    tok = lax.broadcasted_iota(jnp.int32, (W, T), 1)
    s0 = jnp.where(key - (i % CHUNK_B) * T > tok, NEG_BIG, chunk_scores(last))
    s_ref[...] = s0
    cmax_ref[...] = jnp.max(s0, axis=0, keepdims=True)
    m_ref[...] = jnp.full(m_ref.shape, NEG_BIG, F32)
    l_ref[...] = jnp.zeros_like(l_ref)
    acc_ref[...] = jnp.zeros_like(acc_ref)

    def step(cid, next_cid):
        m_prev = m_ref[...]
        m_new = jnp.maximum(m_prev, cmax_ref[...])
        alpha = jnp.exp(m_prev - m_new)
        s_next = None if next_cid is None else chunk_scores(next_cid)
        l_sum = jnp.zeros((1, T), F32)
        pv = jnp.zeros((DH_B, T), F32)
        cmax = None
        for b in range(CHUNK_B):
            rows = slice(b * T, (b + 1) * T)
            p = jnp.exp(s_ref[rows, :] - m_new)
            l_sum = l_sum + jnp.sum(p, axis=0, keepdims=True)
            pv = pv + jnp.dot(vt_ref[cid * CHUNK_B + b], p.astype(BF16), preferred_element_type=F32)
            if s_next is not None:
                nb = s_next[rows, :]
                s_ref[rows, :] = nb
                mb = jnp.max(nb, axis=0, keepdims=True)
                cmax = mb if cmax is None else jnp.maximum(cmax, mb)
        l_ref[...] = alpha * l_ref[...] + l_sum
        acc_ref[...] = alpha * acc_ref[...] + pv
        m_ref[...] = m_new
        if cmax is not None:
            cmax_ref[...] = cmax

    def body(n, carry):
        step(jnp.where(n == 0, last, n - 1), n)
        return carry

    lax.fori_loop(0, last, body, 0)
    step(jnp.where(last == 0, last, last - 1), None)
    o_ref[...] = (acc_ref[...] / l_ref[...]).T.astype(o_ref.dtype)


def _moba_attn(qs, kaug, vt, kmean):
    S = qs.shape[0]
    T = BLOCK_B
    n_blk = S // T
    assert n_blk % CHUNK_B == 0
    kern = functools.partial(_moba_attn_kernel, n_blk=n_blk)
    return pl.pallas_call(
        kern,
        grid=(NH_B, n_blk),
        in_specs=[
            pl.BlockSpec((T, DH_B), lambda h, i: (i, h)),
            pl.BlockSpec((None, S, DH_B + LANE), lambda h, i: (h, 0, 0)),
            pl.BlockSpec((None, n_blk, DH_B, T), lambda h, i: (h, 0, 0, 0)),
            pl.BlockSpec((LANE, DH_B), lambda h, i: (0, h)),
        ],
        out_specs=pl.BlockSpec((T, DH_B), lambda h, i: (i, h)),
        out_shape=jax.ShapeDtypeStruct((S, W_B), BF16),
        scratch_shapes=[
            pltpu.VMEM((T, DH_B + LANE), BF16),
            pltpu.VMEM((CHUNK_B * T, T), F32),
            pltpu.VMEM((1, T), F32),
            pltpu.VMEM((1, T), F32),
            pltpu.VMEM((1, T), F32),
            pltpu.VMEM((DH_B, T), F32),
        ],
        compiler_params=_cparams(("parallel", "arbitrary")),
        name="moba_attn",
    )(qs, kaug, vt, kmean)


def _merge_out_kernel(x_ref, ha_ref, hb_ref, ga_ref, gb_ref, wa_ref, wb_ref, wo_ref, o_ref):
    ta = jnp.dot(ha_ref[...], wa_ref[...], preferred_element_type=F32)
    tb = jnp.dot(hb_ref[...], wb_ref[...], preferred_element_type=F32)
    merged = (jax.nn.sigmoid(ga_ref[...].astype(F32)) * ta
              + jax.nn.sigmoid(gb_ref[...].astype(F32)) * tb).astype(BF16)
    o_ref[...] = x_ref[...] + jnp.dot(merged, wo_ref[...], preferred_element_type=F32)


def _merge_out(x, h_a, h_b, proj, w_a, w_b, w_o, tm):
    S, D = x.shape
    const = dict(pipeline_mode=pl.Buffered(1))
    return pl.pallas_call(
        _merge_out_kernel,
        grid=(S // tm,),
        in_specs=[
            pl.BlockSpec((tm, D), lambda m: (m, 0)),
            pl.BlockSpec((tm, W_A), lambda m: (m, 0)),
            pl.BlockSpec((tm, W_B), lambda m: (m, 0)),
            pl.BlockSpec((tm, D), lambda m: (m, C_G // D)),
            pl.BlockSpec((tm, D), lambda m: (m, C_G // D + 1)),
            pl.BlockSpec((W_A, D), lambda m: (0, 0), **const),
            pl.BlockSpec((W_B, D), lambda m: (0, 0), **const),
            pl.BlockSpec((D, D), lambda m: (0, 0), **const),
        ],
        out_specs=pl.BlockSpec((tm, D), lambda m: (m, 0)),
        out_shape=jax.ShapeDtypeStruct((S, D), F32),
        compiler_params=_cparams(("parallel",)),
        name="merge_out",
    )(x, h_a, h_b, proj, proj, w_a, w_b, w_o)


def _mlp_kernel(x_ref, g_ref, wu_ref, wd_ref, o_ref, hn_ref):
    @pl.when(pl.program_id(1) == 0)
    def _():
        x = x_ref[...]
        hn_ref[...] = _rms(x, g_ref[...]).astype(BF16)
        o_ref[...] = x

    u = jnp.dot(hn_ref[...], wu_ref[...], preferred_element_type=F32)
    a = jnp.square(jnp.maximum(u, 0.0)).astype(BF16)
    o_ref[...] += jnp.dot(a, wd_ref[...], preferred_element_type=F32)


def _mlp(x, gain, w_up, w_down, tm, tf):
    S, D = x.shape
    FF = w_up.shape[1]
    return pl.pallas_call(
        _mlp_kernel,
        grid=(S // tm, FF // tf),
        in_specs=[
            pl.BlockSpec((tm, D), lambda m, f: (m, 0)),
            pl.BlockSpec((1, D), lambda m, f: (0, 0)),
            pl.BlockSpec((D, tf), lambda m, f: (0, f)),
            pl.BlockSpec((tf, D), lambda m, f: (f, 0)),
        ],
        out_specs=pl.BlockSpec((tm, D), lambda m, f: (m, 0)),
        out_shape=jax.ShapeDtypeStruct((S, D), F32),
        scratch_shapes=[pltpu.VMEM((tm, D), BF16)],
        compiler_params=_cparams(("parallel", "arbitrary")),
        name="mlp",
    )(x, gain, w_up, w_down)


def _tile(n, pref):
    t = min(n, pref)
    assert n % t == 0
    return t


def kernel(x, norm_mix, w_in, b_if, norm_h_mlstm, norm_q_moba, norm_k_moba,
           w_branch_a, w_branch_b, w_out, norm_mlp, w_up, w_down):
    B, S, D = x.shape
    assert B == 1 and S % BLOCK_B == 0 and D % LANE == 0
    depth = w_in.shape[0]
    c_if = C_QB
    n_main = w_in.shape[2] - N_IF
    assert n_main == C_G + 2 * D

    xs = x.reshape(S, D)
    for l in range(depth):
        w_main = jnp.concatenate([w_in[l, :, :c_if], w_in[l, :, c_if + N_IF:]], axis=1).astype(BF16)
        w_if = jnp.pad(w_in[l, :, c_if:c_if + N_IF], ((0, 0), (0, IF_PAD - N_IF))).astype(BF16)
        bias_if = jnp.pad(b_if[l].astype(F32), (0, IF_PAD - N_IF)).reshape(1, IF_PAD)

        proj, ifp = _in_proj(xs, norm_mix[l].reshape(1, D), w_main, w_if, bias_if,
                             _tile(S, 512), _tile(n_main, 1024))
        h_a = _mlstm(proj, ifp, norm_h_mlstm[l].reshape(1, W_A))
        qs, kaug, vt, kmean = _moba_prep(proj, norm_q_moba[l].reshape(1, DH_B),
                                         norm_k_moba[l].reshape(1, DH_B))
        h_b = _moba_attn(qs, kaug, vt, kmean)
        xs = _merge_out(xs, h_a, h_b, proj, w_branch_a[l].astype(BF16),
                        w_branch_b[l].astype(BF16), w_out[l].astype(BF16), _tile(S, 256))
        xs = _mlp(xs, norm_mlp[l].reshape(1, D), w_up[l].astype(BF16), w_down[l].astype(BF16),
                  _tile(S, 512), _tile(w_up.shape[2], 512))
    return xs.reshape(B, S, D)
```

```python
import functools

import jax
import jax.numpy as jnp
from jax import lax
from jax.experimental import pallas as pl
from jax.experimental.pallas import tpu as pltpu

F32 = jnp.float32
BF16 = jnp.bfloat16

NH_A, DQK_A, DV_A, CHUNK_A = 4, 128, 256, 128
GATE_SOFTCAP = 15.0
NH_B, DH_B, BLOCK_B, TOPK_B = 8, 128, 256, 3
EPS = 1e-6

W_A = NH_A * DV_A
W_B = NH_B * DH_B
QK_A = NH_A * DQK_A
N_IF = 2 * NH_A
LANE = 128
IF_PAD = LANE
NEG_BIG = -1e30
VMEM_LIMIT = 56 * 1024 * 1024

C_QA, C_KA, C_VA, C_OA = 0, QK_A, 2 * QK_A, 2 * QK_A + W_A
C_QB = C_OA + W_A
C_KB = C_QB + W_B
C_VB = C_KB + W_B
C_G = C_VB + W_B


def _cparams(sem):
    return pltpu.CompilerParams(dimension_semantics=sem, vmem_limit_bytes=VMEM_LIMIT)


def _rms(x, gain):
    ms = jnp.mean(x * x, axis=-1, keepdims=True)
    return x * lax.rsqrt(ms + EPS) * gain


def _in_proj_kernel(x_ref, g_ref, w_ref, wif_ref, bif_ref, o_ref, oif_ref, xn_ref):
    @pl.when(pl.program_id(1) == 0)
    def _():
        xn = _rms(x_ref[...], g_ref[...]).astype(BF16)
        xn_ref[...] = xn
        oif_ref[...] = jnp.dot(xn, wif_ref[...], preferred_element_type=F32) + bif_ref[...]

    o_ref[...] = jnp.dot(xn_ref[...], w_ref[...], preferred_element_type=F32).astype(o_ref.dtype)


def _in_proj(x, gain, w_main, w_if, b_if, tm, tn):
    S, D = x.shape
    N = w_main.shape[1]
    return pl.pallas_call(
        _in_proj_kernel,
        grid=(S // tm, N // tn),
        in_specs=[
            pl.BlockSpec((tm, D), lambda m, n: (m, 0)),
            pl.BlockSpec((1, D), lambda m, n: (0, 0)),
            pl.BlockSpec((D, tn), lambda m, n: (0, n)),
            pl.BlockSpec((D, IF_PAD), lambda m, n: (0, 0)),
            pl.BlockSpec((1, IF_PAD), lambda m, n: (0, 0)),
        ],
        out_specs=[
            pl.BlockSpec((tm, tn), lambda m, n: (m, n)),
            pl.BlockSpec((tm, IF_PAD), lambda m, n: (m, 0)),
        ],
        out_shape=[
            jax.ShapeDtypeStruct((S, N), BF16),
            jax.ShapeDtypeStruct((S, IF_PAD), F32),
        ],
        scratch_shapes=[pltpu.VMEM((tm, D), BF16)],
        compiler_params=_cparams(("parallel", "arbitrary")),
        name="in_proj",
    )(x, gain, w_main, w_if, b_if)


def _mlstm_kernel(q_ref, k_ref, v_ref, oa_ref, if_ref, gain_ref, o_ref, c_ref, m_ref):
    L = CHUNK_A
    DVX = DV_A + LANE

    @pl.when(pl.program_id(0) == 0)
    def _():
        c_ref[...] = jnp.zeros_like(c_ref)
        m_ref[...] = jnp.zeros_like(m_ref)

    g = if_ref[...]
    gc = GATE_SOFTCAP * jnp.tanh(g / GATE_SOFTCAP)
    col = lax.broadcasted_iota(jnp.int32, (L, LANE), 1)
    row = lax.broadcasted_iota(jnp.int32, (L, LANE), 0)
    log_f = jnp.minimum(gc, 0.0) - jnp.log1p(jnp.exp(-jnp.abs(gc)))
    G = jnp.where(col < NH_A, gc, log_f)
    tril = (row >= col).astype(F32)
    Bc = jnp.dot(tril, G, preferred_element_type=F32, precision=lax.Precision.HIGHEST)
    RT = (G - pltpu.roll(Bc, LANE - NH_A, 1)).T
    causal = row >= col
    scale = DQK_A ** -0.5
    ones_col = jnp.where(col == 0, 1.0, 0.0).astype(BF16)

    for h in range(NH_A):
        qh = q_ref[:, h * DQK_A:(h + 1) * DQK_A]
        kh = k_ref[:, h * DQK_A:(h + 1) * DQK_A]
        vh = v_ref[:, h * DV_A:(h + 1) * DV_A]
        vext = jnp.concatenate([vh, ones_col], axis=1)
        b_col = Bc[:, NH_A + h:NH_A + h + 1]
        b_last = Bc[L - 1:L, NH_A + h:NH_A + h + 1]
        ig_col = G[:, h:h + 1]
        m_prev = m_ref[h:h + 1, 0:1]
        c_prev = c_ref[h]

        log_d = jnp.where(causal, b_col + RT[h:h + 1, :], -jnp.inf)
        m_inter = b_col + m_prev
        m_row = jnp.maximum(m_inter, jnp.max(log_d, axis=1, keepdims=True))
        dmat = jnp.exp(log_d - m_row)
        s = lax.dot_general(qh, kh, (((1,), (1,)), ((), ())), preferred_element_type=F32)
        sd = (s * (dmat * scale)).astype(BF16)
        inter = jnp.exp(m_inter - m_row) * scale
        num_ext = (jnp.dot(sd, vext, preferred_element_type=F32)
                   + inter * jnp.dot(qh, c_prev.astype(BF16), preferred_element_type=F32))
        num = num_ext[:, :DV_A]
        den = num_ext[:, DV_A:DV_A + 1]
        hval = num / jnp.maximum(jnp.abs(den), jnp.exp(-m_row))

        hn = _rms(hval, gain_ref[:, h * DV_A:(h + 1) * DV_A])
        og = jax.nn.sigmoid(oa_ref[:, h * DV_A:(h + 1) * DV_A].astype(F32))
        o_ref[:, h * DV_A:(h + 1) * DV_A] = (hn * og).astype(o_ref.dtype)

        log_w = b_last - b_col + ig_col
        m_new = jnp.maximum(b_last + m_prev, jnp.max(log_w, axis=0, keepdims=True))
        w = jnp.exp(log_w - m_new)
        decay = jnp.exp(b_last + m_prev - m_new)
        wv = (w * vext.astype(F32)).astype(BF16)
        c_ref[h] = decay * c_prev + lax.dot_general(
            kh, wv, (((0,), (0,)), ((), ())), preferred_element_type=F32)
        m_ref[h:h + 1, :] = jnp.broadcast_to(m_new, (1, LANE))


def _mlstm(proj, ifp, gain_h):
    S = proj.shape[0]
    L = CHUNK_A
    return pl.pallas_call(
        _mlstm_kernel,
        grid=(S // L,),
        in_specs=[
            pl.BlockSpec((L, QK_A), lambda c: (c, C_QA // QK_A)),
            pl.BlockSpec((L, QK_A), lambda c: (c, C_KA // QK_A)),
            pl.BlockSpec((L, W_A), lambda c: (c, C_VA // W_A)),
            pl.BlockSpec((L, W_A), lambda c: (c, C_OA // W_A)),
            pl.BlockSpec((L, IF_PAD), lambda c: (c, 0)),
            pl.BlockSpec((1, W_A), lambda c: (0, 0)),
        ],
        out_specs=pl.BlockSpec((L, W_A), lambda c: (c, 0)),
        out_shape=jax.ShapeDtypeStruct((S, W_A), BF16),
        scratch_shapes=[
            pltpu.VMEM((NH_A, DQK_A, DV_A + LANE), F32),
            pltpu.VMEM((8, LANE), F32),
        ],
        compiler_params=_cparams(("arbitrary",)),
        name="mlstm",
    )(proj, proj, proj, proj, ifp, gain_h)


def _alibi_slope(h):
    return 2.0 ** (-8.0 * (h + 1) / NH_B)


def _moba_prep_kernel(q_ref, k_ref, v_ref, gq_ref, gk_ref, qst_ref, kaug_ref, vt_ref, kmean_ref, *, n_blk):
    i = pl.program_id(0)
    T = BLOCK_B
    scale = DH_B ** -0.5
    lane = lax.broadcasted_iota(jnp.int32, (T, LANE), 1)
    pos = lax.broadcasted_iota(jnp.int32, (T, LANE), 0).astype(F32)
    blk_f = i.astype(F32)
    row_is_blk = lax.broadcasted_iota(jnp.int32, (LANE, DH_B), 0) == i

    @pl.when(i == 0)
    def _():
        kmean_ref[...] = jnp.zeros_like(kmean_ref)

    for h in range(NH_B):
        sl = slice(h * DH_B, (h + 1) * DH_B)
        qn = _rms(q_ref[:, sl].astype(F32), gq_ref[...])
        qst_ref[h] = (qn * scale).T.astype(qst_ref.dtype)
        kn = _rms(k_ref[:, sl].astype(F32), gk_ref[...])
        kmean_ref[:, sl] = jnp.where(row_is_blk, jnp.mean(kn, axis=0, keepdims=True), kmean_ref[:, sl])
        slope = _alibi_slope(h)
        extra = jnp.where(lane == i, 1.0, 0.0)
        extra = jnp.where(lane == n_blk, slope * T * blk_f, extra)
        extra = jnp.where(lane == n_blk + 1, slope * pos, extra)
        extra = jnp.where(lane == n_blk + 2, 1.0, extra)
        kaug_ref[h, :, :DH_B] = kn.astype(kaug_ref.dtype)
        kaug_ref[h, :, DH_B:] = extra.astype(kaug_ref.dtype)
        vt_ref[h] = v_ref[:, sl].astype(F32).T.astype(vt_ref.dtype)


def _moba_prep(proj, gq, gk):
    S = proj.shape[0]
    T = BLOCK_B
    n_blk = S // T
    assert n_blk + 3 <= LANE
    kern = functools.partial(_moba_prep_kernel, n_blk=n_blk)
    return pl.pallas_call(
        kern,
        grid=(n_blk,),
        in_specs=[
            pl.BlockSpec((T, W_B), lambda i: (i, C_QB // W_B)),
            pl.BlockSpec((T, W_B), lambda i: (i, C_KB // W_B)),
            pl.BlockSpec((T, W_B), lambda i: (i, C_VB // W_B)),
            pl.BlockSpec((1, DH_B), lambda i: (0, 0)),
            pl.BlockSpec((1, DH_B), lambda i: (0, 0)),
        ],
        out_specs=[
            pl.BlockSpec((NH_B, DH_B, T), lambda i: (0, 0, i)),
            pl.BlockSpec((NH_B, T, DH_B + LANE), lambda i: (0, i, 0)),
            pl.BlockSpec((NH_B, None, DH_B, T), lambda i: (0, i, 0, 0)),
            pl.BlockSpec((LANE, W_B), lambda i: (0, 0)),
        ],
        out_shape=[
            jax.ShapeDtypeStruct((NH_B, DH_B, S), BF16),
            jax.ShapeDtypeStruct((NH_B, S, DH_B + LANE), BF16),
            jax.ShapeDtypeStruct((NH_B, n_blk, DH_B, T), BF16),
            jax.ShapeDtypeStruct((LANE, W_B), F32),
        ],
        compiler_params=_cparams(("arbitrary",)),
        name="moba_prep",
    )(proj, proj, proj, gq, gk)


CHUNK_B = 4


def _moba_attn_kernel(qst_ref, kaug_ref, vt_ref, kmean_ref, o_ref,
                      qaug_ref, s_ref, cmax_ref, m_ref, l_ref, acc_ref, *, n_blk):
    h = pl.program_id(0)
    sb = pl.program_id(1)
    T = BLOCK_B
    W = CHUNK_B * T
    qst = qst_ref[...]

    gate = jnp.dot(kmean_ref[...].astype(BF16), qst, preferred_element_type=F32)
    blk = lax.broadcasted_iota(jnp.int32, (LANE, W), 0)
    assert T & (T - 1) == 0
    own = sb * CHUNK_B + jnp.right_shift(lax.broadcasted_iota(jnp.int32, (LANE, W), 1), T.bit_length() - 1)
    valid = blk < own
    g = jnp.where(valid, gate, -jnp.inf)
    sel = jnp.zeros((LANE, W), F32)
    for _ in range(TOPK_B):
        mx = jnp.max(g, axis=0, keepdims=True)
        idx = jnp.min(jnp.where(g == mx, blk, LANE), axis=0, keepdims=True)
        pick = blk == idx
        sel = jnp.where(pick, 1.0, sel)
        g = jnp.where(pick, -jnp.inf, g)
    keep = jnp.where(valid, sel, 0.0) + jnp.where(blk == own, 1.0, 0.0)
    aug = jnp.where(keep > 0.0, 0.0, NEG_BIG)

    slope = jnp.exp2(jnp.full((LANE, W), -8.0 / NH_B, F32) * (h + 1).astype(F32))
    aug = jnp.where(blk >= n_blk, 0.0, aug)
    aug = jnp.where((blk == n_blk) | (blk == n_blk + 1), 1.0, aug)
    aug = jnp.where(blk == n_blk + 2, -slope * (T * own).astype(F32), aug)
    qaug_ref[:DH_B, :] = qst
    qaug_ref[DH_B:, :] = aug.astype(BF16)

    def block_scores(cid, b):
        start = pl.multiple_of(cid * W + b * T, T)
        return jnp.dot(kaug_ref[pl.ds(start, T), :], qaug_ref[...],
                       preferred_element_type=F32)

    key = lax.broadcasted_iota(jnp.int32, (T, W), 0)
    tok = lax.broadcasted_iota(jnp.int32, (T, W), 1)
    cmax = None
    for b in range(CHUNK_B):
        sblk = jnp.where(key + b * T > tok, NEG_BIG, block_scores(sb, b))
        s_ref[b * T:(b + 1) * T, :] = sblk
        mb = jnp.max(sblk, axis=0, keepdims=True)
        cmax = mb if cmax is None else jnp.maximum(cmax, mb)
    cmax_ref[...] = cmax
    m_ref[...] = jnp.full(m_ref.shape, NEG_BIG, F32)
    l_ref[...] = jnp.zeros_like(l_ref)
    acc_ref[...] = jnp.zeros_like(acc_ref)

    def step(cid, next_cid):
        m_prev = m_ref[...]
        m_new = jnp.maximum(m_prev, cmax_ref[...])
        alpha = jnp.exp(m_prev - m_new)
        l_sum = jnp.zeros((1, W), F32)
        pv = jnp.zeros((DH_B, W), F32)
        cmax = None
        for b in range(CHUNK_B):
            rows = slice(b * T, (b + 1) * T)
            p = jnp.exp(s_ref[rows, :] - m_new)
            l_sum = l_sum + jnp.sum(p, axis=0, keepdims=True)
            pv = pv + jnp.dot(vt_ref[cid * CHUNK_B + b], p.astype(BF16), preferred_element_type=F32)
            if next_cid is not None:
                nb = block_scores(next_cid, b)
                s_ref[rows, :] = nb
                mb = jnp.max(nb, axis=0, keepdims=True)
                cmax = mb if cmax is None else jnp.maximum(cmax, mb)
        l_ref[...] = alpha * l_ref[...] + l_sum
        acc_ref[...] = alpha * acc_ref[...] + pv
        m_ref[...] = m_new
        if cmax is not None:
            cmax_ref[...] = cmax

    def body(n, carry):
        step(jnp.where(n == 0, sb, n - 1), n)
        return carry

    lax.fori_loop(0, sb, body, 0)
    step(jnp.where(sb == 0, sb, sb - 1), None)
    o_ref[...] = (acc_ref[...] / l_ref[...]).T.astype(o_ref.dtype)


def _moba_attn(qst, kaug, vt, kmean):
    S = qst.shape[2]
    T = BLOCK_B
    W = CHUNK_B * T
    n_blk = S // T
    assert n_blk % CHUNK_B == 0
    kern = functools.partial(_moba_attn_kernel, n_blk=n_blk)
    return pl.pallas_call(
        kern,
        grid=(NH_B, n_blk // CHUNK_B),
        in_specs=[
            pl.BlockSpec((None, DH_B, W), lambda h, s: (h, 0, s)),
            pl.BlockSpec((None, S, DH_B + LANE), lambda h, s: (h, 0, 0)),
            pl.BlockSpec((None, n_blk, DH_B, T), lambda h, s: (h, 0, 0, 0)),
            pl.BlockSpec((LANE, DH_B), lambda h, s: (0, h)),
        ],
        out_specs=pl.BlockSpec((W, DH_B), lambda h, s: (s, h)),
        out_shape=jax.ShapeDtypeStruct((S, W_B), BF16),
        scratch_shapes=[
            pltpu.VMEM((DH_B + LANE, W), BF16),
            pltpu.VMEM((W, W), F32),
            pltpu.VMEM((1, W), F32),
            pltpu.VMEM((1, W), F32),
            pltpu.VMEM((1, W), F32),
            pltpu.VMEM((DH_B, W), F32),
        ],
        compiler_params=_cparams(("parallel", "arbitrary")),
        name="moba_attn",
    )(qst, kaug, vt, kmean)


def _merge_out_kernel(x_ref, ha_ref, hb_ref, ga_ref, gb_ref, wa_ref, wb_ref, wo_ref, o_ref):
    ta = jnp.dot(ha_ref[...], wa_ref[...], preferred_element_type=F32)
    tb = jnp.dot(hb_ref[...], wb_ref[...], preferred_element_type=F32)
    merged = (jax.nn.sigmoid(ga_ref[...].astype(F32)) * ta
              + jax.nn.sigmoid(gb_ref[...].astype(F32)) * tb).astype(BF16)
    o_ref[...] = x_ref[...] + jnp.dot(merged, wo_ref[...], preferred_element_type=F32)


def _merge_out(x, h_a, h_b, proj, w_a, w_b, w_o, tm):
    S, D = x.shape
    const = dict(pipeline_mode=pl.Buffered(1))
    return pl.pallas_call(
        _merge_out_kernel,
        grid=(S // tm,),
        in_specs=[
            pl.BlockSpec((tm, D), lambda m: (m, 0)),
            pl.BlockSpec((tm, W_A), lambda m: (m, 0)),
            pl.BlockSpec((tm, W_B), lambda m: (m, 0)),
            pl.BlockSpec((tm, D), lambda m: (m, C_G // D)),
            pl.BlockSpec((tm, D), lambda m: (m, C_G // D + 1)),
            pl.BlockSpec((W_A, D), lambda m: (0, 0), **const),
            pl.BlockSpec((W_B, D), lambda m: (0, 0), **const),
            pl.BlockSpec((D, D), lambda m: (0, 0), **const),
        ],
        out_specs=pl.BlockSpec((tm, D), lambda m: (m, 0)),
        out_shape=jax.ShapeDtypeStruct((S, D), F32),
        compiler_params=_cparams(("parallel",)),
        name="merge_out",
    )(x, h_a, h_b, proj, proj, w_a, w_b, w_o)


def _mlp_kernel(x_ref, g_ref, wu_ref, wd_ref, o_ref, hn_ref):
    @pl.when(pl.program_id(1) == 0)
    def _():
        x = x_ref[...]
        hn_ref[...] = _rms(x, g_ref[...]).astype(BF16)
        o_ref[...] = x

    u = jnp.dot(hn_ref[...], wu_ref[...], preferred_element_type=F32)
    a = jnp.square(jnp.maximum(u, 0.0)).astype(BF16)
    o_ref[...] += jnp.dot(a, wd_ref[...], preferred_element_type=F32)


def _mlp(x, gain, w_up, w_down, tm, tf):
    S, D = x.shape
    FF = w_up.shape[1]
    return pl.pallas_call(
        _mlp_kernel,
        grid=(S // tm, FF // tf),
        in_specs=[
            pl.BlockSpec((tm, D), lambda m, f: (m, 0)),
            pl.BlockSpec((1, D), lambda m, f: (0, 0)),
            pl.BlockSpec((D, tf), lambda m, f: (0, f)),
            pl.BlockSpec((tf, D), lambda m, f: (f, 0)),
        ],
        out_specs=pl.BlockSpec((tm, D), lambda m, f: (m, 0)),
        out_shape=jax.ShapeDtypeStruct((S, D), F32),
        scratch_shapes=[pltpu.VMEM((tm, D), BF16)],
        compiler_params=_cparams(("parallel", "arbitrary")),
        name="mlp",
    )(x, gain, w_up, w_down)


def _tile(n, pref):
    t = min(n, pref)
    assert n % t == 0
    return t


def kernel(x, norm_mix, w_in, b_if, norm_h_mlstm, norm_q_moba, norm_k_moba,
           w_branch_a, w_branch_b, w_out, norm_mlp, w_up, w_down):
    B, S, D = x.shape
    assert B == 1 and S % BLOCK_B == 0 and D % LANE == 0
    depth = w_in.shape[0]
    c_if = C_QB
    n_main = w_in.shape[2] - N_IF
    assert n_main == C_G + 2 * D

    xs = x.reshape(S, D)
    for l in range(depth):
        w_main = jnp.concatenate([w_in[l, :, :c_if], w_in[l, :, c_if + N_IF:]], axis=1).astype(BF16)
        w_if = jnp.pad(w_in[l, :, c_if:c_if + N_IF], ((0, 0), (0, IF_PAD - N_IF))).astype(BF16)
        bias_if = jnp.pad(b_if[l].astype(F32), (0, IF_PAD - N_IF)).reshape(1, IF_PAD)

        proj, ifp = _in_proj(xs, norm_mix[l].reshape(1, D), w_main, w_if, bias_if,
                             _tile(S, 512), _tile(n_main, 1024))
        h_a = _mlstm(proj, ifp, norm_h_mlstm[l].reshape(1, W_A))
        qst, kaug, vt, kmean = _moba_prep(proj, norm_q_moba[l].reshape(1, DH_B),
                                          norm_k_moba[l].reshape(1, DH_B))
        h_b = _moba_attn(qst, kaug, vt, kmean)
        xs = _merge_out(xs, h_a, h_b, proj, w_branch_a[l].astype(BF16),
                        w_branch_b[l].astype(BF16), w_out[l].astype(BF16), _tile(S, 256))
        xs = _mlp(xs, norm_mlp[l].reshape(1, D), w_up[l].astype(BF16), w_down[l].astype(BF16),
                  _tile(S, 512), _tile(w_up.shape[2], 512))
    return xs.reshape(B, S, D)
```

```python
import functools

import jax
import jax.numpy as jnp
import numpy as np
from jax import lax
from jax.experimental import pallas as pl
from jax.experimental.pallas import tpu as pltpu

F32 = jnp.float32
BF16 = jnp.bfloat16

NH_A, DQK_A, DV_A, CHUNK_A = 4, 128, 256, 128
GATE_SOFTCAP = 15.0
NH_B, DH_B, BLOCK_B, TOPK_B = 8, 128, 256, 3
EPS = 1e-6

W_A = NH_A * DV_A
W_B = NH_B * DH_B
QK_A = NH_A * DQK_A
N_IF = 2 * NH_A
LANE = 128
IF_PAD = LANE
NEG_BIG = -1e30
VMEM_LIMIT = 56 * 1024 * 1024

C_QA, C_KA, C_VA, C_OA = 0, QK_A, 2 * QK_A, 2 * QK_A + W_A
C_QB = C_OA + W_A
C_KB = C_QB + W_B
C_VB = C_KB + W_B
C_G = C_VB + W_B


def _cparams(sem):
    return pltpu.CompilerParams(dimension_semantics=sem, vmem_limit_bytes=VMEM_LIMIT)


def _rms(x, gain):
    ms = jnp.mean(x * x, axis=-1, keepdims=True)
    return x * lax.rsqrt(ms + EPS) * gain


def _in_proj_kernel(x_ref, g_ref, w_ref, wif_ref, bif_ref, o_ref, oif_ref, xn_ref):
    @pl.when(pl.program_id(1) == 0)
    def _():
        xn = _rms(x_ref[...], g_ref[...]).astype(BF16)
        xn_ref[...] = xn
        oif_ref[...] = jnp.dot(xn, wif_ref[...], preferred_element_type=F32) + bif_ref[...]

    o_ref[...] = jnp.dot(xn_ref[...], w_ref[...], preferred_element_type=F32).astype(o_ref.dtype)


def _in_proj(x, gain, w_main, w_if, b_if, tm, tn):
    S, D = x.shape
    N = w_main.shape[1]
    return pl.pallas_call(
        _in_proj_kernel,
        grid=(S // tm, N // tn),
        in_specs=[
            pl.BlockSpec((tm, D), lambda m, n: (m, 0)),
            pl.BlockSpec((1, D), lambda m, n: (0, 0)),
            pl.BlockSpec((D, tn), lambda m, n: (0, n)),
            pl.BlockSpec((D, IF_PAD), lambda m, n: (0, 0)),
            pl.BlockSpec((1, IF_PAD), lambda m, n: (0, 0)),
        ],
        out_specs=[
            pl.BlockSpec((tm, tn), lambda m, n: (m, n)),
            pl.BlockSpec((tm, IF_PAD), lambda m, n: (m, 0)),
        ],
        out_shape=[
            jax.ShapeDtypeStruct((S, N), BF16),
            jax.ShapeDtypeStruct((S, IF_PAD), F32),
        ],
        scratch_shapes=[pltpu.VMEM((tm, D), BF16)],
        compiler_params=_cparams(("parallel", "arbitrary")),
        name="in_proj",
    )(x, gain, w_main, w_if, b_if)


def _mlstm_kernel(q_ref, k_ref, v_ref, oa_ref, if_ref, gain_ref, o_ref, c_ref, m_ref):
    L = CHUNK_A
    DVX = DV_A + LANE

    @pl.when(pl.program_id(0) == 0)
    def _():
        c_ref[...] = jnp.zeros_like(c_ref)
        m_ref[...] = jnp.zeros_like(m_ref)

    g = if_ref[...]
    gc = GATE_SOFTCAP * jnp.tanh(g / GATE_SOFTCAP)
    col = lax.broadcasted_iota(jnp.int32, (L, LANE), 1)
    row = lax.broadcasted_iota(jnp.int32, (L, LANE), 0)
    log_f = jnp.minimum(gc, 0.0) - jnp.log1p(jnp.exp(-jnp.abs(gc)))
    G = jnp.where(col < NH_A, gc, log_f)
    tril = (row >= col).astype(F32)
    Bc = jnp.dot(tril, G, preferred_element_type=F32, precision=lax.Precision.HIGHEST)
    RT = (G - pltpu.roll(Bc, LANE - NH_A, 1)).T
    causal = row >= col
    scale = DQK_A ** -0.5
    ones_col = jnp.where(col == 0, 1.0, 0.0).astype(BF16)

    for h in range(NH_A):
        qh = q_ref[:, h * DQK_A:(h + 1) * DQK_A]
        kh = k_ref[:, h * DQK_A:(h + 1) * DQK_A]
        vh = v_ref[:, h * DV_A:(h + 1) * DV_A]
        vext = jnp.concatenate([vh, ones_col], axis=1)
        b_col = Bc[:, NH_A + h:NH_A + h + 1]
        b_last = Bc[L - 1:L, NH_A + h:NH_A + h + 1]
        ig_col = G[:, h:h + 1]
        m_prev = m_ref[h:h + 1, 0:1]
        c_prev = c_ref[h]

        log_d = jnp.where(causal, b_col + RT[h:h + 1, :], -jnp.inf)
        m_inter = b_col + m_prev
        m_row = jnp.maximum(m_inter, jnp.max(log_d, axis=1, keepdims=True))
        dmat = jnp.exp(log_d - m_row)
        s = lax.dot_general(qh, kh, (((1,), (1,)), ((), ())), preferred_element_type=F32)
        sd = (s * (dmat * scale)).astype(BF16)
        inter = jnp.exp(m_inter - m_row) * scale
        num_ext = (jnp.dot(sd, vext, preferred_element_type=F32)
                   + inter * jnp.dot(qh, c_prev.astype(BF16), preferred_element_type=F32))
        num = num_ext[:, :DV_A]
        den = num_ext[:, DV_A:DV_A + 1]
        hval = num / jnp.maximum(jnp.abs(den), jnp.exp(-m_row))

        hn = _rms(hval, gain_ref[:, h * DV_A:(h + 1) * DV_A])
        og = jax.nn.sigmoid(oa_ref[:, h * DV_A:(h + 1) * DV_A].astype(F32))
        o_ref[:, h * DV_A:(h + 1) * DV_A] = (hn * og).astype(o_ref.dtype)

        log_w = b_last - b_col + ig_col
        m_new = jnp.maximum(b_last + m_prev, jnp.max(log_w, axis=0, keepdims=True))
        w = jnp.exp(log_w - m_new)
        decay = jnp.exp(b_last + m_prev - m_new)
        wv = (w * vext.astype(F32)).astype(BF16)
        c_ref[h] = decay * c_prev + lax.dot_general(
            kh, wv, (((0,), (0,)), ((), ())), preferred_element_type=F32)
        m_ref[h:h + 1, :] = jnp.broadcast_to(m_new, (1, LANE))


def _mlstm(proj, ifp, gain_h):
    S = proj.shape[0]
    L = CHUNK_A
    return pl.pallas_call(
        _mlstm_kernel,
        grid=(S // L,),
        in_specs=[
            pl.BlockSpec((L, QK_A), lambda c: (c, C_QA // QK_A)),
            pl.BlockSpec((L, QK_A), lambda c: (c, C_KA // QK_A)),
            pl.BlockSpec((L, W_A), lambda c: (c, C_VA // W_A)),
            pl.BlockSpec((L, W_A), lambda c: (c, C_OA // W_A)),
            pl.BlockSpec((L, IF_PAD), lambda c: (c, 0)),
            pl.BlockSpec((1, W_A), lambda c: (0, 0)),
        ],
        out_specs=pl.BlockSpec((L, W_A), lambda c: (c, 0)),
        out_shape=jax.ShapeDtypeStruct((S, W_A), BF16),
        scratch_shapes=[
            pltpu.VMEM((NH_A, DQK_A, DV_A + LANE), F32),
            pltpu.VMEM((8, LANE), F32),
        ],
        compiler_params=_cparams(("arbitrary",)),
        name="mlstm",
    )(proj, proj, proj, proj, ifp, gain_h)


LOG2E = 1.4426950408889634
N_SPLIT = 3
V_ROWS = DH_B + 16


def _alibi_slope(h):
    return 2.0 ** (-8.0 * (h + 1) / NH_B)


def _bf16_split_const(x):
    terms, rem = [], float(np.float32(x))
    for _ in range(N_SPLIT):
        t = float(np.float32(rem).astype(BF16))
        terms.append(t)
        rem -= t
    return terms


def _moba_prep_kernel(q_ref, k_ref, v_ref, gq_ref, gk_ref, qst_ref, kaug_ref, vt_ref, kmean_ref, *, n_blk):
    i = pl.program_id(0)
    T = BLOCK_B
    scale = DH_B ** -0.5 * LOG2E
    lane = lax.broadcasted_iota(jnp.int32, (T, LANE), 1)
    pos = lax.broadcasted_iota(jnp.int32, (T, LANE), 0).astype(F32)
    blk_f = i.astype(F32)
    row_is_blk = lax.broadcasted_iota(jnp.int32, (LANE, DH_B), 0) == i
    ones_rows = jnp.where(lax.broadcasted_iota(jnp.int32, (V_ROWS - DH_B, T), 0) == 0, 1.0, 0.0)

    @pl.when(i == 0)
    def _():
        kmean_ref[...] = jnp.zeros_like(kmean_ref)

    for h in range(NH_B):
        sl = slice(h * DH_B, (h + 1) * DH_B)
        qn = _rms(q_ref[:, sl].astype(F32), gq_ref[...])
        qst_ref[h] = (qn * scale).T.astype(qst_ref.dtype)
        kn = _rms(k_ref[:, sl].astype(F32), gk_ref[...])
        kmean_ref[:, sl] = jnp.where(row_is_blk, jnp.mean(kn, axis=0, keepdims=True), kmean_ref[:, sl])
        slope = _alibi_slope(h)
        extra = jnp.where(lane < n_blk + 3 * N_SPLIT, 1.0, 0.0)
        extra = jnp.where(lane < n_blk + 2 * N_SPLIT, slope * pos, extra)
        extra = jnp.where(lane < n_blk + N_SPLIT, slope * T * blk_f, extra)
        extra = jnp.where(lane < n_blk, jnp.where(lane == i, 1.0, 0.0), extra)
        kaug_ref[h, :, :DH_B] = kn.astype(kaug_ref.dtype)
        kaug_ref[h, :, DH_B:] = extra.astype(kaug_ref.dtype)
        vt = jnp.concatenate([v_ref[:, sl].astype(F32).T, ones_rows], axis=0)
        vt_ref[h] = vt.astype(vt_ref.dtype)


def _moba_prep(proj, gq, gk):
    S = proj.shape[0]
    T = BLOCK_B
    n_blk = S // T
    assert n_blk + 3 * N_SPLIT <= LANE
    kern = functools.partial(_moba_prep_kernel, n_blk=n_blk)
    return pl.pallas_call(
        kern,
        grid=(n_blk,),
        in_specs=[
            pl.BlockSpec((T, W_B), lambda i: (i, C_QB // W_B)),
            pl.BlockSpec((T, W_B), lambda i: (i, C_KB // W_B)),
            pl.BlockSpec((T, W_B), lambda i: (i, C_VB // W_B)),
            pl.BlockSpec((1, DH_B), lambda i: (0, 0)),
            pl.BlockSpec((1, DH_B), lambda i: (0, 0)),
        ],
        out_specs=[
            pl.BlockSpec((NH_B, DH_B, T), lambda i: (0, 0, i)),
            pl.BlockSpec((NH_B, T, DH_B + LANE), lambda i: (0, i, 0)),
            pl.BlockSpec((NH_B, None, V_ROWS, T), lambda i: (0, i, 0, 0)),
            pl.BlockSpec((LANE, W_B), lambda i: (0, 0)),
        ],
        out_shape=[
            jax.ShapeDtypeStruct((NH_B, DH_B, S), BF16),
            jax.ShapeDtypeStruct((NH_B, S, DH_B + LANE), BF16),
            jax.ShapeDtypeStruct((NH_B, n_blk, V_ROWS, T), BF16),
            jax.ShapeDtypeStruct((LANE, W_B), F32),
        ],
        compiler_params=_cparams(("arbitrary",)),
        name="moba_prep",
    )(proj, proj, proj, gq, gk)


CHUNK_B = 4


def _moba_attn_kernel(qst_ref, kaug_ref, vt_ref, kmean_ref, o_ref,
                      qaug_ref, s_ref, cmax_ref, m_ref, acc_ref, *, n_blk):
    h = pl.program_id(0)
    sb = pl.program_id(1)
    T = BLOCK_B
    W = CHUNK_B * T
    qst = qst_ref[...]

    gate = jnp.dot(kmean_ref[:n_blk, :].astype(BF16), qst, preferred_element_type=F32)
    blk = lax.broadcasted_iota(jnp.int32, (n_blk, W), 0)
    assert T & (T - 1) == 0
    tok_blk = jnp.right_shift(lax.broadcasted_iota(jnp.int32, (1, W), 1), T.bit_length() - 1)
    own = sb * CHUNK_B + tok_blk
    valid = blk < own
    g = jnp.where(valid, gate, -jnp.inf)
    sel = jnp.zeros((n_blk, W), F32)
    for _ in range(TOPK_B):
        mx = jnp.max(g, axis=0, keepdims=True)
        idx = jnp.min(jnp.where(g == mx, blk, n_blk), axis=0, keepdims=True)
        pick = blk == idx
        sel = jnp.where(pick, 1.0, sel)
        g = jnp.where(pick, -jnp.inf, g)
    keep = jnp.where(valid, sel, 0.0) + jnp.where(blk == own, 1.0, 0.0)
    selbias = jnp.where(keep > 0.0, 0.0, NEG_BIG)

    slope = jnp.exp2(jnp.full((1, W), -8.0 / NH_B, F32) * (h + 1).astype(F32))
    v3 = -slope * (T * own).astype(F32) * LOG2E
    v3_hi = v3.astype(BF16).astype(F32)
    v3_r = v3 - v3_hi
    v3_lo = v3_r.astype(BF16).astype(F32)
    v3_terms = (v3_hi, v3_lo, v3_r - v3_lo)
    c_terms = _bf16_split_const(LOG2E)
    n_tail = LANE - n_blk
    r = lax.broadcasted_iota(jnp.int32, (n_tail, W), 0)
    tail = jnp.zeros((n_tail, W), F32)
    for t in range(N_SPLIT):
        tail = jnp.where(r == t, c_terms[t], tail)
        tail = jnp.where(r == N_SPLIT + t, c_terms[t], tail)
        tail = jnp.where(r == 2 * N_SPLIT + t, v3_terms[t], tail)
    qaug_ref[:DH_B, :] = qst
    qaug_ref[DH_B:DH_B + n_blk, :] = selbias.astype(BF16)
    qaug_ref[DH_B + n_blk:, :] = tail.astype(BF16)

    def block_scores(cid, b):
        start = pl.multiple_of(cid * W + b * T, T)
        return jnp.dot(kaug_ref[pl.ds(start, T), :], qaug_ref[...],
                       preferred_element_type=F32)

    key = lax.broadcasted_iota(jnp.int32, (T, W), 0)
    tok = lax.broadcasted_iota(jnp.int32, (T, W), 1)
    cmax = None
    for b in range(CHUNK_B):
        sblk = jnp.where(key + b * T > tok, NEG_BIG, block_scores(sb, b))
        s_ref[b * T:(b + 1) * T, :] = sblk
        mb = jnp.max(sblk, axis=0, keepdims=True)
        cmax = mb if cmax is None else jnp.maximum(cmax, mb)
    cmax_ref[...] = cmax
    m_ref[...] = jnp.full(m_ref.shape, NEG_BIG, F32)
    acc_ref[...] = jnp.zeros_like(acc_ref)

    def step(cid, next_cid):
        m_prev = m_ref[...]
        m_new = jnp.maximum(m_prev, cmax_ref[...])
        alpha = jnp.exp2(m_prev - m_new)
        pv = jnp.zeros((V_ROWS, W), F32)
        cmax = None
        for b in range(CHUNK_B):
            rows = slice(b * T, (b + 1) * T)
            p = jnp.exp2(s_ref[rows, :] - m_new)
            pv = pv + jnp.dot(vt_ref[cid * CHUNK_B + b], p.astype(BF16), preferred_element_type=F32)
            if next_cid is not None:
                nb = block_scores(next_cid, b)
                s_ref[rows, :] = nb
                mb = jnp.max(nb, axis=0, keepdims=True)
                cmax = mb if cmax is None else jnp.maximum(cmax, mb)
        acc_ref[...] = alpha * acc_ref[...] + pv
        m_ref[...] = m_new
        if cmax is not None:
            cmax_ref[...] = cmax

    def body(n, carry):
        step(jnp.where(n == 0, sb, n - 1), n)
        return carry

    lax.fori_loop(0, sb, body, 0)
    step(jnp.where(sb == 0, sb, sb - 1), None)
    acc = acc_ref[...]
    o_ref[...] = (acc[:DH_B, :] / acc[DH_B:DH_B + 1, :]).T.astype(o_ref.dtype)


def _moba_attn(qst, kaug, vt, kmean):
    S = qst.shape[2]
    T = BLOCK_B
    W = CHUNK_B * T
    n_blk = S // T
    assert n_blk % CHUNK_B == 0
    kern = functools.partial(_moba_attn_kernel, n_blk=n_blk)
    return pl.pallas_call(
        kern,
        grid=(NH_B, n_blk // CHUNK_B),
        in_specs=[
            pl.BlockSpec((None, DH_B, W), lambda h, s: (h, 0, s)),
            pl.BlockSpec((None, S, DH_B + LANE), lambda h, s: (h, 0, 0)),
            pl.BlockSpec((None, n_blk, V_ROWS, T), lambda h, s: (h, 0, 0, 0)),
            pl.BlockSpec((LANE, DH_B), lambda h, s: (0, h)),
        ],
        out_specs=pl.BlockSpec((W, DH_B), lambda h, s: (s, h)),
        out_shape=jax.ShapeDtypeStruct((S, W_B), BF16),
        scratch_shapes=[
            pltpu.VMEM((DH_B + LANE, W), BF16),
            pltpu.VMEM((W, W), F32),
            pltpu.VMEM((1, W), F32),
            pltpu.VMEM((1, W), F32),
            pltpu.VMEM((V_ROWS, W), F32),
        ],
        compiler_params=_cparams(("parallel", "arbitrary")),
        name="moba_attn",
    )(qst, kaug, vt, kmean)


def _merge_out_kernel(x_ref, ha_ref, hb_ref, ga_ref, gb_ref, wa_ref, wb_ref, wo_ref, o_ref):
    ta = jnp.dot(ha_ref[...], wa_ref[...], preferred_element_type=F32)
    tb = jnp.dot(hb_ref[...], wb_ref[...], preferred_element_type=F32)
    merged = (jax.nn.sigmoid(ga_ref[...].astype(F32)) * ta
              + jax.nn.sigmoid(gb_ref[...].astype(F32)) * tb).astype(BF16)
    o_ref[...] = x_ref[...] + jnp.dot(merged, wo_ref[...], preferred_element_type=F32)


def _merge_out(x, h_a, h_b, proj, w_a, w_b, w_o, tm):
    S, D = x.shape
    const = dict(pipeline_mode=pl.Buffered(1))
    return pl.pallas_call(
        _merge_out_kernel,
        grid=(S // tm,),
        in_specs=[
            pl.BlockSpec((tm, D), lambda m: (m, 0)),
            pl.BlockSpec((tm, W_A), lambda m: (m, 0)),
            pl.BlockSpec((tm, W_B), lambda m: (m, 0)),
            pl.BlockSpec((tm, D), lambda m: (m, C_G // D)),
            pl.BlockSpec((tm, D), lambda m: (m, C_G // D + 1)),
            pl.BlockSpec((W_A, D), lambda m: (0, 0), **const),
            pl.BlockSpec((W_B, D), lambda m: (0, 0), **const),
            pl.BlockSpec((D, D), lambda m: (0, 0), **const),
        ],
        out_specs=pl.BlockSpec((tm, D), lambda m: (m, 0)),
        out_shape=jax.ShapeDtypeStruct((S, D), F32),
        compiler_params=_cparams(("parallel",)),
        name="merge_out",
    )(x, h_a, h_b, proj, proj, w_a, w_b, w_o)


def _mlp_kernel(x_ref, g_ref, wu_ref, wd_ref, o_ref, hn_ref):
    @pl.when(pl.program_id(1) == 0)
    def _():
        x = x_ref[...]
        hn_ref[...] = _rms(x, g_ref[...]).astype(BF16)
        o_ref[...] = x

    u = jnp.dot(hn_ref[...], wu_ref[...], preferred_element_type=F32)
    a = jnp.square(jnp.maximum(u, 0.0)).astype(BF16)
    o_ref[...] += jnp.dot(a, wd_ref[...], preferred_element_type=F32)


def _mlp(x, gain, w_up, w_down, tm, tf):
    S, D = x.shape
    FF = w_up.shape[1]
    return pl.pallas_call(
        _mlp_kernel,
        grid=(S // tm, FF // tf),
        in_specs=[
            pl.BlockSpec((tm, D), lambda m, f: (m, 0)),
            pl.BlockSpec((1, D), lambda m, f: (0, 0)),
            pl.BlockSpec((D, tf), lambda m, f: (0, f)),
            pl.BlockSpec((tf, D), lambda m, f: (f, 0)),
        ],
        out_specs=pl.BlockSpec((tm, D), lambda m, f: (m, 0)),
        out_shape=jax.ShapeDtypeStruct((S, D), F32),
        scratch_shapes=[pltpu.VMEM((tm, D), BF16)],
        compiler_params=_cparams(("parallel", "arbitrary")),
        name="mlp",
    )(x, gain, w_up, w_down)


def _tile(n, pref):
    t = min(n, pref)
    assert n % t == 0
    return t


def kernel(x, norm_mix, w_in, b_if, norm_h_mlstm, norm_q_moba, norm_k_moba,
           w_branch_a, w_branch_b, w_out, norm_mlp, w_up, w_down):
    B, S, D = x.shape
    assert B == 1 and S % BLOCK_B == 0 and D % LANE == 0
    depth = w_in.shape[0]
    c_if = C_QB
    n_main = w_in.shape[2] - N_IF
    assert n_main == C_G + 2 * D

    xs = x.reshape(S, D)
    for l in range(depth):
        w_main = jnp.concatenate([w_in[l, :, :c_if], w_in[l, :, c_if + N_IF:]], axis=1).astype(BF16)
        w_if = jnp.pad(w_in[l, :, c_if:c_if + N_IF], ((0, 0), (0, IF_PAD - N_IF))).astype(BF16)
        bias_if = jnp.pad(b_if[l].astype(F32), (0, IF_PAD - N_IF)).reshape(1, IF_PAD)

        proj, ifp = _in_proj(xs, norm_mix[l].reshape(1, D), w_main, w_if, bias_if,
                             _tile(S, 512), _tile(n_main, 1024))
        h_a = _mlstm(proj, ifp, norm_h_mlstm[l].reshape(1, W_A))
        qst, kaug, vt, kmean = _moba_prep(proj, norm_q_moba[l].reshape(1, DH_B),
                                          norm_k_moba[l].reshape(1, DH_B))
        h_b = _moba_attn(qst, kaug, vt, kmean)
        xs = _merge_out(xs, h_a, h_b, proj, w_branch_a[l].astype(BF16),
                        w_branch_b[l].astype(BF16), w_out[l].astype(BF16), _tile(S, 256))
        xs = _mlp(xs, norm_mlp[l].reshape(1, D), w_up[l].astype(BF16), w_down[l].astype(BF16),
                  _tile(S, 512), _tile(w_up.shape[2], 1024))
    return xs.reshape(B, S, D)
```

```python
import functools

import jax
import jax.numpy as jnp
import numpy as np
from jax import lax
from jax.experimental import pallas as pl
from jax.experimental.pallas import tpu as pltpu

F32 = jnp.float32
BF16 = jnp.bfloat16

NH_A, DQK_A, DV_A, CHUNK_A = 4, 128, 256, 128
GATE_SOFTCAP = 15.0
NH_B, DH_B, BLOCK_B, TOPK_B = 8, 128, 256, 3
EPS = 1e-6

W_A = NH_A * DV_A
W_B = NH_B * DH_B
QK_A = NH_A * DQK_A
N_IF = 2 * NH_A
LANE = 128
IF_PAD = LANE
NEG_BIG = -1e30
VMEM_LIMIT = 56 * 1024 * 1024

C_QA, C_KA, C_VA, C_OA = 0, QK_A, 2 * QK_A, 2 * QK_A + W_A
C_QB = C_OA + W_A
C_KB = C_QB + W_B
C_VB = C_KB + W_B
C_G = C_VB + W_B


def _cparams(sem):
    return pltpu.CompilerParams(dimension_semantics=sem, vmem_limit_bytes=VMEM_LIMIT)


def _rms(x, gain):
    ms = jnp.mean(x * x, axis=-1, keepdims=True)
    return x * lax.rsqrt(ms + EPS) * gain


def _in_proj_kernel(x_ref, g_ref, w_ref, wif_ref, bif_ref, o_ref, oif_ref, xn_ref):
    @pl.when(pl.program_id(1) == 0)
    def _():
        xn = _rms(x_ref[...], g_ref[...]).astype(BF16)
        xn_ref[...] = xn
        oif_ref[...] = jnp.dot(xn, wif_ref[...], preferred_element_type=F32) + bif_ref[...]

    o_ref[...] = jnp.dot(xn_ref[...], w_ref[...], preferred_element_type=F32).astype(o_ref.dtype)


def _in_proj(x, gain, w_main, w_if, b_if, layer, tm, tn):
    S, D = x.shape
    N = w_main.shape[2]
    return pl.pallas_call(
        _in_proj_kernel,
        grid=(S // tm, N // tn),
        in_specs=[
            pl.BlockSpec((tm, D), lambda m, n: (m, 0)),
            pl.BlockSpec((1, D), lambda m, n: (0, 0)),
            pl.BlockSpec((None, D, tn), lambda m, n: (layer, 0, n)),
            pl.BlockSpec((None, D, IF_PAD), lambda m, n: (layer, 0, 0)),
            pl.BlockSpec((1, IF_PAD), lambda m, n: (0, 0)),
        ],
        out_specs=[
            pl.BlockSpec((tm, tn), lambda m, n: (m, n)),
            pl.BlockSpec((tm, IF_PAD), lambda m, n: (m, 0)),
        ],
        out_shape=[
            jax.ShapeDtypeStruct((S, N), BF16),
            jax.ShapeDtypeStruct((S, IF_PAD), F32),
        ],
        scratch_shapes=[pltpu.VMEM((tm, D), BF16)],
        compiler_params=_cparams(("parallel", "arbitrary")),
        name="in_proj",
    )(x, gain, w_main, w_if, b_if)


def _mlstm_kernel(q_ref, k_ref, v_ref, oa_ref, if_ref, gain_ref, o_ref, c_ref, m_ref):
    L = CHUNK_A
    DVX = DV_A + LANE

    @pl.when(pl.program_id(0) == 0)
    def _():
        c_ref[...] = jnp.zeros_like(c_ref)
        m_ref[...] = jnp.zeros_like(m_ref)

    g = if_ref[...]
    gc = GATE_SOFTCAP * jnp.tanh(g / GATE_SOFTCAP)
    col = lax.broadcasted_iota(jnp.int32, (L, LANE), 1)
    row = lax.broadcasted_iota(jnp.int32, (L, LANE), 0)
    log_f = jnp.minimum(gc, 0.0) - jnp.log1p(jnp.exp(-jnp.abs(gc)))
    G = jnp.where(col < NH_A, gc, log_f)
    tril = (row >= col).astype(F32)
    Bc = jnp.dot(tril, G, preferred_element_type=F32, precision=lax.Precision.HIGHEST)
    RT = (G - pltpu.roll(Bc, LANE - NH_A, 1)).T
    causal = row >= col
    scale = DQK_A ** -0.5
    ones_col = jnp.where(col == 0, 1.0, 0.0).astype(BF16)

    for h in range(NH_A):
        qh = q_ref[:, h * DQK_A:(h + 1) * DQK_A]
        kh = k_ref[:, h * DQK_A:(h + 1) * DQK_A]
        vh = v_ref[:, h * DV_A:(h + 1) * DV_A]
        vext = jnp.concatenate([vh, ones_col], axis=1)
        b_col = Bc[:, NH_A + h:NH_A + h + 1]
        b_last = Bc[L - 1:L, NH_A + h:NH_A + h + 1]
        ig_col = G[:, h:h + 1]
        m_prev = m_ref[h:h + 1, 0:1]
        c_prev = c_ref[h]

        log_d = jnp.where(causal, b_col + RT[h:h + 1, :], -jnp.inf)
        m_inter = b_col + m_prev
        m_row = jnp.maximum(m_inter, jnp.max(log_d, axis=1, keepdims=True))
        dmat = jnp.exp(log_d - m_row)
        s = lax.dot_general(qh, kh, (((1,), (1,)), ((), ())), preferred_element_type=F32)
        sd = (s * (dmat * scale)).astype(BF16)
        inter = jnp.exp(m_inter - m_row) * scale
        num_ext = (jnp.dot(sd, vext, preferred_element_type=F32)
                   + inter * jnp.dot(qh, c_prev.astype(BF16), preferred_element_type=F32))
        num = num_ext[:, :DV_A]
        den = num_ext[:, DV_A:DV_A + 1]
        hval = num / jnp.maximum(jnp.abs(den), jnp.exp(-m_row))

        hn = _rms(hval, gain_ref[:, h * DV_A:(h + 1) * DV_A])
        og = jax.nn.sigmoid(oa_ref[:, h * DV_A:(h + 1) * DV_A].astype(F32))
        o_ref[:, h * DV_A:(h + 1) * DV_A] = (hn * og).astype(o_ref.dtype)

        log_w = b_last - b_col + ig_col
        m_new = jnp.maximum(b_last + m_prev, jnp.max(log_w, axis=0, keepdims=True))
        w = jnp.exp(log_w - m_new)
        decay = jnp.exp(b_last + m_prev - m_new)
        wv = (w * vext.astype(F32)).astype(BF16)
        c_ref[h] = decay * c_prev + lax.dot_general(
            kh, wv, (((0,), (0,)), ((), ())), preferred_element_type=F32)
        m_ref[h:h + 1, :] = jnp.broadcast_to(m_new, (1, LANE))


def _mlstm(proj, ifp, gain_h):
    S = proj.shape[0]
    L = CHUNK_A
    return pl.pallas_call(
        _mlstm_kernel,
        grid=(S // L,),
        in_specs=[
            pl.BlockSpec((L, QK_A), lambda c: (c, C_QA // QK_A)),
            pl.BlockSpec((L, QK_A), lambda c: (c, C_KA // QK_A)),
            pl.BlockSpec((L, W_A), lambda c: (c, C_VA // W_A)),
            pl.BlockSpec((L, W_A), lambda c: (c, C_OA // W_A)),
            pl.BlockSpec((L, IF_PAD), lambda c: (c, 0)),
            pl.BlockSpec((1, W_A), lambda c: (0, 0)),
        ],
        out_specs=pl.BlockSpec((L, W_A), lambda c: (c, 0)),
        out_shape=jax.ShapeDtypeStruct((S, W_A), BF16),
        scratch_shapes=[
            pltpu.VMEM((NH_A, DQK_A, DV_A + LANE), F32),
            pltpu.VMEM((8, LANE), F32),
        ],
        compiler_params=_cparams(("arbitrary",)),
        name="mlstm",
    )(proj, proj, proj, proj, ifp, gain_h)


LOG2E = 1.4426950408889634
N_SPLIT = 3
V_ROWS = DH_B + 16


def _alibi_slope(h):
    return 2.0 ** (-8.0 * (h + 1) / NH_B)


def _bf16_split_const(x):
    terms, rem = [], float(np.float32(x))
    for _ in range(N_SPLIT):
        t = float(np.float32(rem).astype(BF16))
        terms.append(t)
        rem -= t
    return terms


def _moba_prep_kernel(q_ref, k_ref, v_ref, gq_ref, gk_ref, qst_ref, kaug_ref, vt_ref, kmean_ref, *, n_blk):
    i = pl.program_id(0)
    T = BLOCK_B
    scale = DH_B ** -0.5 * LOG2E
    lane = lax.broadcasted_iota(jnp.int32, (T, LANE), 1)
    pos = lax.broadcasted_iota(jnp.int32, (T, LANE), 0).astype(F32)
    blk_f = i.astype(F32)
    row_is_blk = lax.broadcasted_iota(jnp.int32, (LANE, DH_B), 0) == i
    ones_rows = jnp.where(lax.broadcasted_iota(jnp.int32, (V_ROWS - DH_B, T), 0) == 0, 1.0, 0.0)

    @pl.when(i == 0)
    def _():
        kmean_ref[...] = jnp.zeros_like(kmean_ref)

    for h in range(NH_B):
        sl = slice(h * DH_B, (h + 1) * DH_B)
        qn = _rms(q_ref[:, sl].astype(F32), gq_ref[...])
        qst_ref[h] = (qn * scale).T.astype(qst_ref.dtype)
        kn = _rms(k_ref[:, sl].astype(F32), gk_ref[...])
        kmean_ref[:, sl] = jnp.where(row_is_blk, jnp.mean(kn, axis=0, keepdims=True), kmean_ref[:, sl])
        slope = _alibi_slope(h)
        extra = jnp.where(lane < n_blk + 3 * N_SPLIT, 1.0, 0.0)
        extra = jnp.where(lane < n_blk + 2 * N_SPLIT, slope * pos, extra)
        extra = jnp.where(lane < n_blk + N_SPLIT, slope * T * blk_f, extra)
        extra = jnp.where(lane < n_blk, jnp.where(lane == i, 1.0, 0.0), extra)
        kaug_ref[h, :, :DH_B] = kn.astype(kaug_ref.dtype)
        kaug_ref[h, :, DH_B:] = extra.astype(kaug_ref.dtype)
        vt = jnp.concatenate([v_ref[:, sl].astype(F32).T, ones_rows], axis=0)
        vt_ref[h] = vt.astype(vt_ref.dtype)


def _moba_prep(proj, gq, gk):
    S = proj.shape[0]
    T = BLOCK_B
    n_blk = S // T
    assert n_blk + 3 * N_SPLIT <= LANE
    kern = functools.partial(_moba_prep_kernel, n_blk=n_blk)
    return pl.pallas_call(
        kern,
        grid=(n_blk,),
        in_specs=[
            pl.BlockSpec((T, W_B), lambda i: (i, C_QB // W_B)),
            pl.BlockSpec((T, W_B), lambda i: (i, C_KB // W_B)),
            pl.BlockSpec((T, W_B), lambda i: (i, C_VB // W_B)),
            pl.BlockSpec((1, DH_B), lambda i: (0, 0)),
            pl.BlockSpec((1, DH_B), lambda i: (0, 0)),
        ],
        out_specs=[
            pl.BlockSpec((NH_B, DH_B, T), lambda i: (0, 0, i)),
            pl.BlockSpec((NH_B, T, DH_B + LANE), lambda i: (0, i, 0)),
            pl.BlockSpec((NH_B, None, V_ROWS, T), lambda i: (0, i, 0, 0)),
            pl.BlockSpec((LANE, W_B), lambda i: (0, 0)),
        ],
        out_shape=[
            jax.ShapeDtypeStruct((NH_B, DH_B, S), BF16),
            jax.ShapeDtypeStruct((NH_B, S, DH_B + LANE), BF16),
            jax.ShapeDtypeStruct((NH_B, n_blk, V_ROWS, T), BF16),
            jax.ShapeDtypeStruct((LANE, W_B), F32),
        ],
        compiler_params=_cparams(("arbitrary",)),
        name="moba_prep",
    )(proj, proj, proj, gq, gk)


CHUNK_B = 4


def _moba_attn_kernel(qst_ref, kaug_ref, vt_ref, kmean_ref, o_ref,
                      qaug_ref, s_ref, cmax_ref, m_ref, acc_ref, *, n_blk):
    h = pl.program_id(0)
    sb = pl.program_id(1)
    T = BLOCK_B
    W = CHUNK_B * T
    qst = qst_ref[...]

    gate = jnp.dot(kmean_ref[:n_blk, :].astype(BF16), qst, preferred_element_type=F32)
    blk = lax.broadcasted_iota(jnp.int32, (n_blk, W), 0)
    assert T & (T - 1) == 0
    tok_blk = jnp.right_shift(lax.broadcasted_iota(jnp.int32, (1, W), 1), T.bit_length() - 1)
    own = sb * CHUNK_B + tok_blk
    valid = blk < own
    g = jnp.where(valid, gate, -jnp.inf)
    sel = jnp.zeros((n_blk, W), F32)
    for _ in range(TOPK_B):
        mx = jnp.max(g, axis=0, keepdims=True)
        idx = jnp.min(jnp.where(g == mx, blk, n_blk), axis=0, keepdims=True)
        pick = blk == idx
        sel = jnp.where(pick, 1.0, sel)
        g = jnp.where(pick, -jnp.inf, g)
    keep = jnp.where(valid, sel, 0.0) + jnp.where(blk == own, 1.0, 0.0)
    selbias = jnp.where(keep > 0.0, 0.0, NEG_BIG)

    slope = jnp.exp2(jnp.full((1, W), -8.0 / NH_B, F32) * (h + 1).astype(F32))
    v3 = -slope * (T * own).astype(F32) * LOG2E
    v3_hi = v3.astype(BF16).astype(F32)
    v3_r = v3 - v3_hi
    v3_lo = v3_r.astype(BF16).astype(F32)
    v3_terms = (v3_hi, v3_lo, v3_r - v3_lo)
    c_terms = _bf16_split_const(LOG2E)
    n_tail = LANE - n_blk
    r = lax.broadcasted_iota(jnp.int32, (n_tail, W), 0)
    tail = jnp.zeros((n_tail, W), F32)
    for t in range(N_SPLIT):
        tail = jnp.where(r == t, c_terms[t], tail)
        tail = jnp.where(r == N_SPLIT + t, c_terms[t], tail)
        tail = jnp.where(r == 2 * N_SPLIT + t, v3_terms[t], tail)
    qaug_ref[:DH_B, :] = qst
    qaug_ref[DH_B:DH_B + n_blk, :] = selbias.astype(BF16)
    qaug_ref[DH_B + n_blk:, :] = tail.astype(BF16)

    def block_scores(cid, b):
        start = pl.multiple_of(cid * W + b * T, T)
        return jnp.dot(kaug_ref[pl.ds(start, T), :], qaug_ref[...],
                       preferred_element_type=F32)

    key = lax.broadcasted_iota(jnp.int32, (T, W), 0)
    tok = lax.broadcasted_iota(jnp.int32, (T, W), 1)
    cmax = None
    for b in range(CHUNK_B):
        sblk = jnp.where(key + b * T > tok, NEG_BIG, block_scores(sb, b))
        s_ref[b * T:(b + 1) * T, :] = sblk
        mb = jnp.max(sblk, axis=0, keepdims=True)
        cmax = mb if cmax is None else jnp.maximum(cmax, mb)
    cmax_ref[...] = cmax
    m_ref[...] = jnp.full(m_ref.shape, NEG_BIG, F32)
    acc_ref[...] = jnp.zeros_like(acc_ref)

    def step(cid, next_cid):
        m_prev = m_ref[...]
        m_new = jnp.maximum(m_prev, cmax_ref[...])
        alpha = jnp.exp2(m_prev - m_new)
        pv = jnp.zeros((V_ROWS, W), F32)
        cmax = None
        for b in range(CHUNK_B):
            rows = slice(b * T, (b + 1) * T)
            p = jnp.exp2(s_ref[rows, :] - m_new)
            pv = pv + jnp.dot(vt_ref[cid * CHUNK_B + b], p.astype(BF16), preferred_element_type=F32)
            if next_cid is not None:
                nb = block_scores(next_cid, b)
                s_ref[rows, :] = nb
                mb = jnp.max(nb, axis=0, keepdims=True)
                cmax = mb if cmax is None else jnp.maximum(cmax, mb)
        acc_ref[...] = alpha * acc_ref[...] + pv
        m_ref[...] = m_new
        if cmax is not None:
            cmax_ref[...] = cmax

    def body(n, carry):
        step(jnp.where(n == 0, sb, n - 1), n)
        return carry

    lax.fori_loop(0, sb, body, 0)
    step(jnp.where(sb == 0, sb, sb - 1), None)
    acc = acc_ref[...]
    o_ref[...] = (acc[:DH_B, :] / acc[DH_B:DH_B + 1, :]).T.astype(o_ref.dtype)


def _moba_attn(qst, kaug, vt, kmean):
    S = qst.shape[2]
    T = BLOCK_B
    W = CHUNK_B * T
    n_blk = S // T
    assert n_blk % CHUNK_B == 0
    kern = functools.partial(_moba_attn_kernel, n_blk=n_blk)
    return pl.pallas_call(
        kern,
        grid=(NH_B, n_blk // CHUNK_B),
        in_specs=[
            pl.BlockSpec((None, DH_B, W), lambda h, s: (h, 0, s)),
            pl.BlockSpec((None, S, DH_B + LANE), lambda h, s: (h, 0, 0)),
            pl.BlockSpec((None, n_blk, V_ROWS, T), lambda h, s: (h, 0, 0, 0)),
            pl.BlockSpec((LANE, DH_B), lambda h, s: (0, h)),
        ],
        out_specs=pl.BlockSpec((W, DH_B), lambda h, s: (s, h)),
        out_shape=jax.ShapeDtypeStruct((S, W_B), BF16),
        scratch_shapes=[
            pltpu.VMEM((DH_B + LANE, W), BF16),
            pltpu.VMEM((W, W), F32),
            pltpu.VMEM((1, W), F32),
            pltpu.VMEM((1, W), F32),
            pltpu.VMEM((V_ROWS, W), F32),
        ],
        compiler_params=_cparams(("parallel", "arbitrary")),
        name="moba_attn",
    )(qst, kaug, vt, kmean)


def _merge_out_kernel(x_ref, ha_ref, hb_ref, ga_ref, gb_ref, wa_ref, wb_ref, wo_ref, o_ref):
    ta = jnp.dot(ha_ref[...], wa_ref[...], preferred_element_type=F32)
    tb = jnp.dot(hb_ref[...], wb_ref[...], preferred_element_type=F32)
    merged = (jax.nn.sigmoid(ga_ref[...].astype(F32)) * ta
              + jax.nn.sigmoid(gb_ref[...].astype(F32)) * tb).astype(BF16)
    o_ref[...] = x_ref[...] + jnp.dot(merged, wo_ref[...], preferred_element_type=F32)


def _merge_out(x, h_a, h_b, proj, w_a, w_b, w_o, layer, tm):
    S, D = x.shape
    const = dict(pipeline_mode=pl.Buffered(1))
    return pl.pallas_call(
        _merge_out_kernel,
        grid=(S // tm,),
        in_specs=[
            pl.BlockSpec((tm, D), lambda m: (m, 0)),
            pl.BlockSpec((tm, W_A), lambda m: (m, 0)),
            pl.BlockSpec((tm, W_B), lambda m: (m, 0)),
            pl.BlockSpec((tm, D), lambda m: (m, C_G // D)),
            pl.BlockSpec((tm, D), lambda m: (m, C_G // D + 1)),
            pl.BlockSpec((None, W_A, D), lambda m: (layer, 0, 0), **const),
            pl.BlockSpec((None, W_B, D), lambda m: (layer, 0, 0), **const),
            pl.BlockSpec((None, D, D), lambda m: (layer, 0, 0), **const),
        ],
        out_specs=pl.BlockSpec((tm, D), lambda m: (m, 0)),
        out_shape=jax.ShapeDtypeStruct((S, D), F32),
        compiler_params=_cparams(("parallel",)),
        name="merge_out",
    )(x, h_a, h_b, proj, proj, w_a, w_b, w_o)


def _mlp_kernel(x_ref, g_ref, wu_ref, wd_ref, o_ref, hn_ref):
    @pl.when(pl.program_id(1) == 0)
    def _():
        x = x_ref[...]
        hn_ref[...] = _rms(x, g_ref[...]).astype(BF16)
        o_ref[...] = x

    u = jnp.dot(hn_ref[...], wu_ref[...], preferred_element_type=F32)
    a = jnp.square(jnp.maximum(u, 0.0)).astype(BF16)
    o_ref[...] += jnp.dot(a, wd_ref[...], preferred_element_type=F32)


def _mlp(x, gain, w_up, w_down, layer, tm, tf):
    S, D = x.shape
    FF = w_up.shape[2]
    return pl.pallas_call(
        _mlp_kernel,
        grid=(S // tm, FF // tf),
        in_specs=[
            pl.BlockSpec((tm, D), lambda m, f: (m, 0)),
            pl.BlockSpec((1, D), lambda m, f: (0, 0)),
            pl.BlockSpec((None, D, tf), lambda m, f: (layer, 0, f)),
            pl.BlockSpec((None, tf, D), lambda m, f: (layer, f, 0)),
        ],
        out_specs=pl.BlockSpec((tm, D), lambda m, f: (m, 0)),
        out_shape=jax.ShapeDtypeStruct((S, D), F32),
        scratch_shapes=[pltpu.VMEM((tm, D), BF16)],
        compiler_params=_cparams(("parallel", "arbitrary")),
        name="mlp",
    )(x, gain, w_up, w_down)


IN_PROJ_TM = 1024
MLP_TM, MLP_TF = 512, 1024


def _tile(n, pref):
    t = min(n, pref)
    assert n % t == 0
    return t


def kernel(x, norm_mix, w_in, b_if, norm_h_mlstm, norm_q_moba, norm_k_moba,
           w_branch_a, w_branch_b, w_out, norm_mlp, w_up, w_down):
    B, S, D = x.shape
    assert B == 1 and S % BLOCK_B == 0 and D % LANE == 0
    depth = w_in.shape[0]
    c_if = C_QB
    n_main = w_in.shape[2] - N_IF
    assert n_main == C_G + 2 * D

    w16 = w_in.astype(BF16)
    w_main = jnp.concatenate([w16[:, :, :c_if], w16[:, :, c_if + N_IF:]], axis=2)
    w_if = jnp.pad(w16[:, :, c_if:c_if + N_IF], ((0, 0), (0, 0), (0, IF_PAD - N_IF)))
    w_a, w_b, w_o = w_branch_a.astype(BF16), w_branch_b.astype(BF16), w_out.astype(BF16)
    w_u, w_d = w_up.astype(BF16), w_down.astype(BF16)

    xs = x.reshape(S, D)
    for l in range(depth):
        bias_if = jnp.pad(b_if[l].astype(F32), (0, IF_PAD - N_IF)).reshape(1, IF_PAD)
        proj, ifp = _in_proj(xs, norm_mix[l].reshape(1, D), w_main, w_if, bias_if, l,
                             _tile(S, IN_PROJ_TM), _tile(n_main, 1024))
        h_a = _mlstm(proj, ifp, norm_h_mlstm[l].reshape(1, W_A))
        qst, kaug, vt, kmean = _moba_prep(proj, norm_q_moba[l].reshape(1, DH_B),
                                          norm_k_moba[l].reshape(1, DH_B))
        h_b = _moba_attn(qst, kaug, vt, kmean)
        xs = _merge_out(xs, h_a, h_b, proj, w_a, w_b, w_o, l, _tile(S, 256))
        xs = _mlp(xs, norm_mlp[l].reshape(1, D), w_u, w_d, l, _tile(S, MLP_TM), _tile(w_up.shape[2], MLP_TF))
    return xs.reshape(B, S, D)
```

```python
import functools

import jax
import jax.numpy as jnp
import numpy as np
from jax import lax
from jax.experimental import pallas as pl
from jax.experimental.pallas import tpu as pltpu

F32 = jnp.float32
BF16 = jnp.bfloat16

NH_A, DQK_A, DV_A, CHUNK_A = 4, 128, 256, 128
GATE_SOFTCAP = 15.0
NH_B, DH_B, BLOCK_B, TOPK_B = 8, 128, 256, 3
EPS = 1e-6

W_A = NH_A * DV_A
W_B = NH_B * DH_B
QK_A = NH_A * DQK_A
N_IF = 2 * NH_A
LANE = 128
IF_PAD = LANE
NEG_BIG = -1e30
VMEM_LIMIT = 56 * 1024 * 1024

C_QA, C_KA, C_VA, C_OA = 0, QK_A, 2 * QK_A, 2 * QK_A + W_A
C_QB = C_OA + W_A
C_KB = C_QB + W_B
C_VB = C_KB + W_B
C_G = C_VB + W_B


def _cparams(sem):
    return pltpu.CompilerParams(dimension_semantics=sem, vmem_limit_bytes=VMEM_LIMIT)


def _rms(x, gain):
    ms = jnp.mean(x * x, axis=-1, keepdims=True)
    return x * lax.rsqrt(ms + EPS) * gain


def _w_repack_kernel(w_ref, main_ref, if_ref, *, c_if):
    n_cols = w_ref.shape[-1]
    main_ref[:, :c_if] = w_ref[:, :c_if].astype(main_ref.dtype)
    main_ref[:, c_if:] = w_ref[:, c_if + N_IF:n_cols].astype(main_ref.dtype)
    lane = lax.broadcasted_iota(jnp.int32, (w_ref.shape[0], IF_PAD), 1)
    if_ref[...] = jnp.where(lane < N_IF, w_ref[:, c_if:c_if + IF_PAD], 0.0).astype(if_ref.dtype)


def _w_repack(w_in, c_if, tr):
    depth, D, n_cols = w_in.shape
    n_main = n_cols - N_IF
    return pl.pallas_call(
        functools.partial(_w_repack_kernel, c_if=c_if),
        grid=(depth, D // tr),
        in_specs=[pl.BlockSpec((None, tr, n_cols), lambda l, r: (l, r, 0))],
        out_specs=[
            pl.BlockSpec((None, tr, n_main), lambda l, r: (l, r, 0)),
            pl.BlockSpec((None, tr, IF_PAD), lambda l, r: (l, r, 0)),
        ],
        out_shape=[
            jax.ShapeDtypeStruct((depth, D, n_main), BF16),
            jax.ShapeDtypeStruct((depth, D, IF_PAD), BF16),
        ],
        compiler_params=_cparams(("parallel", "parallel")),
        name="w_repack",
    )(w_in)


def _in_proj_kernel(x_ref, g_ref, w_ref, wif_ref, bif_ref, o_ref, oif_ref, xn_ref):
    @pl.when(pl.program_id(1) == 0)
    def _():
        xn = _rms(x_ref[...], g_ref[...]).astype(BF16)
        xn_ref[...] = xn
        oif_ref[...] = jnp.dot(xn, wif_ref[...], preferred_element_type=F32) + bif_ref[...]

    o_ref[...] = jnp.dot(xn_ref[...], w_ref[...], preferred_element_type=F32).astype(o_ref.dtype)


def _in_proj(x, gain, w_main, w_if, b_if, layer, tm, tn):
    S, D = x.shape
    N = w_main.shape[2]
    return pl.pallas_call(
        _in_proj_kernel,
        grid=(S // tm, N // tn),
        in_specs=[
            pl.BlockSpec((tm, D), lambda m, n: (m, 0)),
            pl.BlockSpec((1, D), lambda m, n: (0, 0)),
            pl.BlockSpec((None, D, tn), lambda m, n: (layer, 0, n)),
            pl.BlockSpec((None, D, IF_PAD), lambda m, n: (layer, 0, 0)),
            pl.BlockSpec((1, IF_PAD), lambda m, n: (0, 0)),
        ],
        out_specs=[
            pl.BlockSpec((tm, tn), lambda m, n: (m, n)),
            pl.BlockSpec((tm, IF_PAD), lambda m, n: (m, 0)),
        ],
        out_shape=[
            jax.ShapeDtypeStruct((S, N), BF16),
            jax.ShapeDtypeStruct((S, IF_PAD), F32),
        ],
        scratch_shapes=[pltpu.VMEM((tm, D), BF16)],
        compiler_params=_cparams(("parallel", "arbitrary")),
        name="in_proj",
    )(x, gain, w_main, w_if, b_if)


def _mlstm_kernel(q_ref, k_ref, v_ref, oa_ref, if_ref, gain_ref, o_ref, c_ref, m_ref):
    L = CHUNK_A
    DVX = DV_A + LANE

    @pl.when(pl.program_id(0) == 0)
    def _():
        c_ref[...] = jnp.zeros_like(c_ref)
        m_ref[...] = jnp.zeros_like(m_ref)

    g = if_ref[...]
    gc = GATE_SOFTCAP * jnp.tanh(g / GATE_SOFTCAP)
    col = lax.broadcasted_iota(jnp.int32, (L, LANE), 1)
    row = lax.broadcasted_iota(jnp.int32, (L, LANE), 0)
    log_f = jnp.minimum(gc, 0.0) - jnp.log1p(jnp.exp(-jnp.abs(gc)))
    G = jnp.where(col < NH_A, gc, log_f)
    tril = (row >= col).astype(F32)
    Bc = jnp.dot(tril, G, preferred_element_type=F32, precision=lax.Precision.HIGHEST)
    RT = (G - pltpu.roll(Bc, LANE - NH_A, 1)).T
    causal = row >= col
    scale = DQK_A ** -0.5
    ones_col = jnp.where(col == 0, 1.0, 0.0).astype(BF16)

    for h in range(NH_A):
        qh = q_ref[:, h * DQK_A:(h + 1) * DQK_A]
        kh = k_ref[:, h * DQK_A:(h + 1) * DQK_A]
        vh = v_ref[:, h * DV_A:(h + 1) * DV_A]
        vext = jnp.concatenate([vh, ones_col], axis=1)
        b_col = Bc[:, NH_A + h:NH_A + h + 1]
        b_last = Bc[L - 1:L, NH_A + h:NH_A + h + 1]
        ig_col = G[:, h:h + 1]
        m_prev = m_ref[h:h + 1, 0:1]
        c_prev = c_ref[h]

        log_d = jnp.where(causal, b_col + RT[h:h + 1, :], -jnp.inf)
        m_inter = b_col + m_prev
        m_row = jnp.maximum(m_inter, jnp.max(log_d, axis=1, keepdims=True))
        dmat = jnp.exp(log_d - m_row)
        s = lax.dot_general(qh, kh, (((1,), (1,)), ((), ())), preferred_element_type=F32)
        sd = (s * (dmat * scale)).astype(BF16)
        inter = jnp.exp(m_inter - m_row) * scale
        num_ext = (jnp.dot(sd, vext, preferred_element_type=F32)
                   + inter * jnp.dot(qh, c_prev.astype(BF16), preferred_element_type=F32))
        num = num_ext[:, :DV_A]
        den = num_ext[:, DV_A:DV_A + 1]
        hval = num / jnp.maximum(jnp.abs(den), jnp.exp(-m_row))

        hn = _rms(hval, gain_ref[:, h * DV_A:(h + 1) * DV_A])
        og = jax.nn.sigmoid(oa_ref[:, h * DV_A:(h + 1) * DV_A].astype(F32))
        o_ref[:, h * DV_A:(h + 1) * DV_A] = (hn * og).astype(o_ref.dtype)

        log_w = b_last - b_col + ig_col
        m_new = jnp.maximum(b_last + m_prev, jnp.max(log_w, axis=0, keepdims=True))
        w = jnp.exp(log_w - m_new)
        decay = jnp.exp(b_last + m_prev - m_new)
        wv = (w * vext.astype(F32)).astype(BF16)
        c_ref[h] = decay * c_prev + lax.dot_general(
            kh, wv, (((0,), (0,)), ((), ())), preferred_element_type=F32)
        m_ref[h:h + 1, :] = jnp.broadcast_to(m_new, (1, LANE))


def _mlstm(proj, ifp, gain_h):
    S = proj.shape[0]
    L = CHUNK_A
    return pl.pallas_call(
        _mlstm_kernel,
        grid=(S // L,),
        in_specs=[
            pl.BlockSpec((L, QK_A), lambda c: (c, C_QA // QK_A)),
            pl.BlockSpec((L, QK_A), lambda c: (c, C_KA // QK_A)),
            pl.BlockSpec((L, W_A), lambda c: (c, C_VA // W_A)),
            pl.BlockSpec((L, W_A), lambda c: (c, C_OA // W_A)),
            pl.BlockSpec((L, IF_PAD), lambda c: (c, 0)),
            pl.BlockSpec((1, W_A), lambda c: (0, 0)),
        ],
        out_specs=pl.BlockSpec((L, W_A), lambda c: (c, 0)),
        out_shape=jax.ShapeDtypeStruct((S, W_A), BF16),
        scratch_shapes=[
            pltpu.VMEM((NH_A, DQK_A, DV_A + LANE), F32),
            pltpu.VMEM((8, LANE), F32),
        ],
        compiler_params=_cparams(("arbitrary",)),
        name="mlstm",
    )(proj, proj, proj, proj, ifp, gain_h)


LOG2E = 1.4426950408889634
N_SPLIT = 3
V_ROWS = DH_B + 16


def _alibi_slope(h):
    return 2.0 ** (-8.0 * (h + 1) / NH_B)


def _bf16_split_const(x):
    terms, rem = [], float(np.float32(x))
    for _ in range(N_SPLIT):
        t = float(np.float32(rem).astype(BF16))
        terms.append(t)
        rem -= t
    return terms


def _moba_prep_kernel(q_ref, k_ref, v_ref, gq_ref, gk_ref, qst_ref, kaug_ref, vt_ref, kmean_ref, *, n_blk):
    i = pl.program_id(0)
    T = BLOCK_B
    scale = DH_B ** -0.5 * LOG2E
    lane = lax.broadcasted_iota(jnp.int32, (T, LANE), 1)
    pos = lax.broadcasted_iota(jnp.int32, (T, LANE), 0).astype(F32)
    blk_f = i.astype(F32)
    row_is_blk = lax.broadcasted_iota(jnp.int32, (LANE, DH_B), 0) == i
    ones_rows = jnp.where(lax.broadcasted_iota(jnp.int32, (V_ROWS - DH_B, T), 0) == 0, 1.0, 0.0)

    @pl.when(i == 0)
    def _():
        kmean_ref[...] = jnp.zeros_like(kmean_ref)

    for h in range(NH_B):
        sl = slice(h * DH_B, (h + 1) * DH_B)
        qn = _rms(q_ref[:, sl].astype(F32), gq_ref[...])
        qst_ref[h] = (qn * scale).T.astype(qst_ref.dtype)
        kn = _rms(k_ref[:, sl].astype(F32), gk_ref[...])
        kmean_ref[:, sl] = jnp.where(row_is_blk, jnp.mean(kn, axis=0, keepdims=True), kmean_ref[:, sl])
        slope = _alibi_slope(h)
        extra = jnp.where(lane < n_blk + 3 * N_SPLIT, 1.0, 0.0)
        extra = jnp.where(lane < n_blk + 2 * N_SPLIT, slope * pos, extra)
        extra = jnp.where(lane < n_blk + N_SPLIT, slope * T * blk_f, extra)
        extra = jnp.where(lane < n_blk, jnp.where(lane == i, 1.0, 0.0), extra)
        kaug_ref[h, :, :DH_B] = kn.astype(kaug_ref.dtype)
        kaug_ref[h, :, DH_B:] = extra.astype(kaug_ref.dtype)
        vt = jnp.concatenate([v_ref[:, sl].astype(F32).T, ones_rows], axis=0)
        vt_ref[h] = vt.astype(vt_ref.dtype)


def _moba_prep(proj, gq, gk):
    S = proj.shape[0]
    T = BLOCK_B
    n_blk = S // T
    assert n_blk + 3 * N_SPLIT <= LANE
    kern = functools.partial(_moba_prep_kernel, n_blk=n_blk)
    return pl.pallas_call(
        kern,
        grid=(n_blk,),
        in_specs=[
            pl.BlockSpec((T, W_B), lambda i: (i, C_QB // W_B)),
            pl.BlockSpec((T, W_B), lambda i: (i, C_KB // W_B)),
            pl.BlockSpec((T, W_B), lambda i: (i, C_VB // W_B)),
            pl.BlockSpec((1, DH_B), lambda i: (0, 0)),
            pl.BlockSpec((1, DH_B), lambda i: (0, 0)),
        ],
        out_specs=[
            pl.BlockSpec((NH_B, DH_B, T), lambda i: (0, 0, i)),
            pl.BlockSpec((NH_B, T, DH_B + LANE), lambda i: (0, i, 0)),
            pl.BlockSpec((NH_B, None, V_ROWS, T), lambda i: (0, i, 0, 0)),
            pl.BlockSpec((LANE, W_B), lambda i: (0, 0)),
        ],
        out_shape=[
            jax.ShapeDtypeStruct((NH_B, DH_B, S), BF16),
            jax.ShapeDtypeStruct((NH_B, S, DH_B + LANE), BF16),
            jax.ShapeDtypeStruct((NH_B, n_blk, V_ROWS, T), BF16),
            jax.ShapeDtypeStruct((LANE, W_B), F32),
        ],
        compiler_params=_cparams(("arbitrary",)),
        name="moba_prep",
    )(proj, proj, proj, gq, gk)


CHUNK_B = 4


def _moba_attn_kernel(qst_ref, kaug_ref, vt_ref, kmean_ref, o_ref,
                      qaug_ref, s_ref, cmax_ref, m_ref, acc_ref, *, n_blk):
    h = pl.program_id(0)
    sb = pl.program_id(1)
    T = BLOCK_B
    W = CHUNK_B * T
    qst = qst_ref[...]

    gate = jnp.dot(kmean_ref[:n_blk, :].astype(BF16), qst, preferred_element_type=F32)
    blk = lax.broadcasted_iota(jnp.int32, (n_blk, W), 0)
    assert T & (T - 1) == 0
    tok_blk = jnp.right_shift(lax.broadcasted_iota(jnp.int32, (1, W), 1), T.bit_length() - 1)
    own = sb * CHUNK_B + tok_blk
    valid = blk < own
    g = jnp.where(valid, gate, -jnp.inf)
    sel = jnp.zeros((n_blk, W), F32)
    for _ in range(TOPK_B):
        mx = jnp.max(g, axis=0, keepdims=True)
        idx = jnp.min(jnp.where(g == mx, blk, n_blk), axis=0, keepdims=True)
        pick = blk == idx
        sel = jnp.where(pick, 1.0, sel)
        g = jnp.where(pick, -jnp.inf, g)
    keep = jnp.where(valid, sel, 0.0) + jnp.where(blk == own, 1.0, 0.0)
    selbias = jnp.where(keep > 0.0, 0.0, NEG_BIG)

    slope = jnp.exp2(jnp.full((1, W), -8.0 / NH_B, F32) * (h + 1).astype(F32))
    v3 = -slope * (T * own).astype(F32) * LOG2E
    v3_hi = v3.astype(BF16).astype(F32)
    v3_r = v3 - v3_hi
    v3_lo = v3_r.astype(BF16).astype(F32)
    v3_terms = (v3_hi, v3_lo, v3_r - v3_lo)
    c_terms = _bf16_split_const(LOG2E)
    n_tail = LANE - n_blk
    r = lax.broadcasted_iota(jnp.int32, (n_tail, W), 0)
    tail = jnp.zeros((n_tail, W), F32)
    for t in range(N_SPLIT):
        tail = jnp.where(r == t, c_terms[t], tail)
        tail = jnp.where(r == N_SPLIT + t, c_terms[t], tail)
        tail = jnp.where(r == 2 * N_SPLIT + t, v3_terms[t], tail)
    qaug_ref[:DH_B, :] = qst
    qaug_ref[DH_B:DH_B + n_blk, :] = selbias.astype(BF16)
    qaug_ref[DH_B + n_blk:, :] = tail.astype(BF16)

    def block_scores(cid, b):
        start = pl.multiple_of(cid * W + b * T, T)
        return jnp.dot(kaug_ref[pl.ds(start, T), :], qaug_ref[...],
                       preferred_element_type=F32)

    key = lax.broadcasted_iota(jnp.int32, (T, W), 0)
    tok = lax.broadcasted_iota(jnp.int32, (T, W), 1)
    cmax = None
    for b in range(CHUNK_B):
        sblk = jnp.where(key + b * T > tok, NEG_BIG, block_scores(sb, b))
        s_ref[b * T:(b + 1) * T, :] = sblk
        mb = jnp.max(sblk, axis=0, keepdims=True)
        cmax = mb if cmax is None else jnp.maximum(cmax, mb)
    cmax_ref[...] = cmax
    m_ref[...] = jnp.full(m_ref.shape, NEG_BIG, F32)
    acc_ref[...] = jnp.zeros_like(acc_ref)

    def step(cid, next_cid):
        m_prev = m_ref[...]
        m_new = jnp.maximum(m_prev, cmax_ref[...])
        alpha = jnp.exp2(m_prev - m_new)
        pv = jnp.zeros((V_ROWS, W), F32)
        cmax = None
        for b in range(CHUNK_B):
            rows = slice(b * T, (b + 1) * T)
            p = jnp.exp2(s_ref[rows, :] - m_new)
            pv = pv + jnp.dot(vt_ref[cid * CHUNK_B + b], p.astype(BF16), preferred_element_type=F32)
            if next_cid is not None:
                nb = block_scores(next_cid, b)
                s_ref[rows, :] = nb
                mb = jnp.max(nb, axis=0, keepdims=True)
                cmax = mb if cmax is None else jnp.maximum(cmax, mb)
        acc_ref[...] = alpha * acc_ref[...] + pv
        m_ref[...] = m_new
        if cmax is not None:
            cmax_ref[...] = cmax

    def body(n, carry):
        step(jnp.where(n == 0, sb, n - 1), n)
        return carry

    lax.fori_loop(0, sb, body, 0)
    step(jnp.where(sb == 0, sb, sb - 1), None)
    acc = acc_ref[...]
    o_ref[...] = (acc[:DH_B, :] / acc[DH_B:DH_B + 1, :]).T.astype(o_ref.dtype)


def _moba_attn(qst, kaug, vt, kmean):
    S = qst.shape[2]
    T = BLOCK_B
    W = CHUNK_B * T
    n_blk = S // T
    assert n_blk % CHUNK_B == 0
    kern = functools.partial(_moba_attn_kernel, n_blk=n_blk)
    return pl.pallas_call(
        kern,
        grid=(NH_B, n_blk // CHUNK_B),
        in_specs=[
            pl.BlockSpec((None, DH_B, W), lambda h, s: (h, 0, s)),
            pl.BlockSpec((None, S, DH_B + LANE), lambda h, s: (h, 0, 0)),
            pl.BlockSpec((None, n_blk, V_ROWS, T), lambda h, s: (h, 0, 0, 0)),
            pl.BlockSpec((LANE, DH_B), lambda h, s: (0, h)),
        ],
        out_specs=pl.BlockSpec((W, DH_B), lambda h, s: (s, h)),
        out_shape=jax.ShapeDtypeStruct((S, W_B), BF16),
        scratch_shapes=[
            pltpu.VMEM((DH_B + LANE, W), BF16),
            pltpu.VMEM((W, W), F32),
            pltpu.VMEM((1, W), F32),
            pltpu.VMEM((1, W), F32),
            pltpu.VMEM((V_ROWS, W), F32),
        ],
        compiler_params=_cparams(("parallel", "arbitrary")),
        name="moba_attn",
    )(qst, kaug, vt, kmean)


def _merge_out_kernel(x_ref, ha_ref, hb_ref, ga_ref, gb_ref, wa_ref, wb_ref, wo_ref, o_ref):
    ta = jnp.dot(ha_ref[...], wa_ref[...], preferred_element_type=F32)
    tb = jnp.dot(hb_ref[...], wb_ref[...], preferred_element_type=F32)
    merged = (jax.nn.sigmoid(ga_ref[...].astype(F32)) * ta
              + jax.nn.sigmoid(gb_ref[...].astype(F32)) * tb).astype(BF16)
    o_ref[...] = x_ref[...] + jnp.dot(merged, wo_ref[...], preferred_element_type=F32)


def _merge_out(x, h_a, h_b, proj, w_a, w_b, w_o, layer, tm):
    S, D = x.shape
    const = dict(pipeline_mode=pl.Buffered(1))
    return pl.pallas_call(
        _merge_out_kernel,
        grid=(S // tm,),
        in_specs=[
            pl.BlockSpec((tm, D), lambda m: (m, 0)),
            pl.BlockSpec((tm, W_A), lambda m: (m, 0)),
            pl.BlockSpec((tm, W_B), lambda m: (m, 0)),
            pl.BlockSpec((tm, D), lambda m: (m, C_G // D)),
            pl.BlockSpec((tm, D), lambda m: (m, C_G // D + 1)),
            pl.BlockSpec((None, W_A, D), lambda m: (layer, 0, 0), **const),
            pl.BlockSpec((None, W_B, D), lambda m: (layer, 0, 0), **const),
            pl.BlockSpec((None, D, D), lambda m: (layer, 0, 0), **const),
        ],
        out_specs=pl.BlockSpec((tm, D), lambda m: (m, 0)),
        out_shape=jax.ShapeDtypeStruct((S, D), F32),
        compiler_params=_cparams(("parallel",)),
        name="merge_out",
    )(x, h_a, h_b, proj, proj, w_a, w_b, w_o)


def _mlp_kernel(x_ref, g_ref, wu_ref, wd_ref, o_ref, hn_ref):
    @pl.when(pl.program_id(1) == 0)
    def _():
        x = x_ref[...]
        hn_ref[...] = _rms(x, g_ref[...]).astype(BF16)
        o_ref[...] = x

    u = jnp.dot(hn_ref[...], wu_ref[...], preferred_element_type=F32)
    a = jnp.square(jnp.maximum(u, 0.0)).astype(BF16)
    o_ref[...] += jnp.dot(a, wd_ref[...], preferred_element_type=F32)


def _mlp(x, gain, w_up, w_down, layer, tm, tf):
    S, D = x.shape
    FF = w_up.shape[2]
    return pl.pallas_call(
        _mlp_kernel,
        grid=(S // tm, FF // tf),
        in_specs=[
            pl.BlockSpec((tm, D), lambda m, f: (m, 0)),
            pl.BlockSpec((1, D), lambda m, f: (0, 0)),
            pl.BlockSpec((None, D, tf), lambda m, f: (layer, 0, f)),
            pl.BlockSpec((None, tf, D), lambda m, f: (layer, f, 0)),
        ],
        out_specs=pl.BlockSpec((tm, D), lambda m, f: (m, 0)),
        out_shape=jax.ShapeDtypeStruct((S, D), F32),
        scratch_shapes=[pltpu.VMEM((tm, D), BF16)],
        compiler_params=_cparams(("parallel", "arbitrary")),
        name="mlp",
    )(x, gain, w_up, w_down)


IN_PROJ_TM, IN_PROJ_TN = 1024, 2048
MERGE_TM = 512
MLP_TM, MLP_TF = 512, 1024


def _tile(n, pref):
    t = min(n, pref)
    while n % t:
        t -= LANE
    assert t > 0
    return t


def kernel(x, norm_mix, w_in, b_if, norm_h_mlstm, norm_q_moba, norm_k_moba,
           w_branch_a, w_branch_b, w_out, norm_mlp, w_up, w_down):
    B, S, D = x.shape
    assert B == 1 and S % BLOCK_B == 0 and D % LANE == 0
    depth = w_in.shape[0]
    c_if = C_QB
    n_main = w_in.shape[2] - N_IF
    assert n_main == C_G + 2 * D

    w_main, w_if = _w_repack(w_in, c_if, _tile(D, 256))
    w_a, w_b, w_o = w_branch_a.astype(BF16), w_branch_b.astype(BF16), w_out.astype(BF16)
    w_u, w_d = w_up.astype(BF16), w_down.astype(BF16)

    xs = x.reshape(S, D)
    for l in range(depth):
        bias_if = jnp.pad(b_if[l].astype(F32), (0, IF_PAD - N_IF)).reshape(1, IF_PAD)
        proj, ifp = _in_proj(xs, norm_mix[l].reshape(1, D), w_main, w_if, bias_if, l,
                             _tile(S, IN_PROJ_TM), _tile(n_main, IN_PROJ_TN))
        h_a = _mlstm(proj, ifp, norm_h_mlstm[l].reshape(1, W_A))
        qst, kaug, vt, kmean = _moba_prep(proj, norm_q_moba[l].reshape(1, DH_B),
                                          norm_k_moba[l].reshape(1, DH_B))
        h_b = _moba_attn(qst, kaug, vt, kmean)
        xs = _merge_out(xs, h_a, h_b, proj, w_a, w_b, w_o, l, _tile(S, MERGE_TM))
        xs = _mlp(xs, norm_mlp[l].reshape(1, D), w_u, w_d, l, _tile(S, MLP_TM), _tile(w_up.shape[2], MLP_TF))
    return xs.reshape(B, S, D)
```

```python
import functools

import jax
import jax.numpy as jnp
import numpy as np
from jax import lax
from jax.experimental import pallas as pl
from jax.experimental.pallas import tpu as pltpu

F32 = jnp.float32
BF16 = jnp.bfloat16

NH_A, DQK_A, DV_A, CHUNK_A = 4, 128, 256, 128
GATE_SOFTCAP = 15.0
NH_B, DH_B, BLOCK_B, TOPK_B = 8, 128, 256, 3
EPS = 1e-6

W_A = NH_A * DV_A
W_B = NH_B * DH_B
QK_A = NH_A * DQK_A
N_IF = 2 * NH_A
LANE = 128
IF_PAD = LANE
NEG_BIG = -1e30
VMEM_LIMIT = 56 * 1024 * 1024

C_QA, C_KA, C_VA, C_OA = 0, QK_A, 2 * QK_A, 2 * QK_A + W_A
C_QB = C_OA + W_A
C_KB = C_QB + W_B
C_VB = C_KB + W_B
C_G = C_VB + W_B


def _cparams(sem):
    return pltpu.CompilerParams(dimension_semantics=sem, vmem_limit_bytes=VMEM_LIMIT)


def _rms(x, gain):
    ms = jnp.mean(x * x, axis=-1, keepdims=True)
    return x * lax.rsqrt(ms + EPS) * gain


def _w_repack_kernel(wt_ref, main_ref, if_ref, *, c_if):
    n_cols = wt_ref.shape[0]
    main_ref[:c_if, :] = wt_ref[:c_if, :].astype(main_ref.dtype)
    main_ref[c_if:, :] = wt_ref[c_if + N_IF:n_cols, :].astype(main_ref.dtype)
    row = lax.broadcasted_iota(jnp.int32, if_ref.shape, 0)
    if_ref[...] = jnp.where(row < N_IF, wt_ref[c_if:c_if + IF_PAD, :], 0.0).astype(if_ref.dtype)


def _w_repack(w_in_t, c_if, td):
    depth, n_cols, D = w_in_t.shape
    n_main = n_cols - N_IF
    return pl.pallas_call(
        functools.partial(_w_repack_kernel, c_if=c_if),
        grid=(depth, D // td),
        in_specs=[pl.BlockSpec((None, n_cols, td), lambda l, d: (l, 0, d))],
        out_specs=[
            pl.BlockSpec((None, n_main, td), lambda l, d: (l, 0, d)),
            pl.BlockSpec((None, IF_PAD, td), lambda l, d: (l, 0, d)),
        ],
        out_shape=[
            jax.ShapeDtypeStruct((depth, n_main, D), BF16),
            jax.ShapeDtypeStruct((depth, IF_PAD, D), BF16),
        ],
        compiler_params=_cparams(("parallel", "parallel")),
        name="w_repack",
    )(w_in_t)


_NT = (((1,), (1,)), ((), ()))


def _in_proj_kernel(x_ref, g_ref, w_ref, wif_ref, bif_ref, o_ref, oif_ref, xn_ref):
    @pl.when(pl.program_id(1) == 0)
    def _():
        xn = _rms(x_ref[...], g_ref[...]).astype(BF16)
        xn_ref[...] = xn
        oif_ref[...] = lax.dot_general(xn, wif_ref[...], _NT, preferred_element_type=F32) + bif_ref[...]

    o_ref[...] = lax.dot_general(xn_ref[...], w_ref[...], _NT,
                                 preferred_element_type=F32).astype(o_ref.dtype)


def _in_proj(x, gain, w_main_t, w_if_t, b_if, layer, tm, tn):
    S, D = x.shape
    N = w_main_t.shape[1]
    return pl.pallas_call(
        _in_proj_kernel,
        grid=(S // tm, N // tn),
        in_specs=[
            pl.BlockSpec((tm, D), lambda m, n: (m, 0)),
            pl.BlockSpec((1, D), lambda m, n: (0, 0)),
            pl.BlockSpec((None, tn, D), lambda m, n: (layer, n, 0)),
            pl.BlockSpec((None, IF_PAD, D), lambda m, n: (layer, 0, 0)),
            pl.BlockSpec((1, IF_PAD), lambda m, n: (0, 0)),
        ],
        out_specs=[
            pl.BlockSpec((tm, tn), lambda m, n: (m, n)),
            pl.BlockSpec((tm, IF_PAD), lambda m, n: (m, 0)),
        ],
        out_shape=[
            jax.ShapeDtypeStruct((S, N), BF16),
            jax.ShapeDtypeStruct((S, IF_PAD), F32),
        ],
        scratch_shapes=[pltpu.VMEM((tm, D), BF16)],
        compiler_params=_cparams(("parallel", "arbitrary")),
        name="in_proj",
    )(x, gain, w_main_t, w_if_t, b_if)


def _mlstm_kernel(q_ref, k_ref, v_ref, oa_ref, if_ref, gain_ref, o_ref, c_ref, m_ref):
    L = CHUNK_A
    DVX = DV_A + LANE

    @pl.when(pl.program_id(0) == 0)
    def _():
        c_ref[...] = jnp.zeros_like(c_ref)
        m_ref[...] = jnp.zeros_like(m_ref)

    g = if_ref[...]
    gc = GATE_SOFTCAP * jnp.tanh(g / GATE_SOFTCAP)
    col = lax.broadcasted_iota(jnp.int32, (L, LANE), 1)
    row = lax.broadcasted_iota(jnp.int32, (L, LANE), 0)
    log_f = jnp.minimum(gc, 0.0) - jnp.log1p(jnp.exp(-jnp.abs(gc)))
    G = jnp.where(col < NH_A, gc, log_f)
    tril = (row >= col).astype(F32)
    Bc = jnp.dot(tril, G, preferred_element_type=F32, precision=lax.Precision.HIGHEST)
    RT = (G - pltpu.roll(Bc, LANE - NH_A, 1)).T
    causal = row >= col
    scale = DQK_A ** -0.5
    ones_col = jnp.where(col == 0, 1.0, 0.0).astype(BF16)

    for h in range(NH_A):
        qh = q_ref[:, h * DQK_A:(h + 1) * DQK_A]
        kh = k_ref[:, h * DQK_A:(h + 1) * DQK_A]
        vh = v_ref[:, h * DV_A:(h + 1) * DV_A]
        vext = jnp.concatenate([vh, ones_col], axis=1)
        b_col = Bc[:, NH_A + h:NH_A + h + 1]
        b_last = Bc[L - 1:L, NH_A + h:NH_A + h + 1]
        ig_col = G[:, h:h + 1]
        m_prev = m_ref[h:h + 1, 0:1]
        c_prev = c_ref[h]

        log_d = jnp.where(causal, b_col + RT[h:h + 1, :], -jnp.inf)
        m_inter = b_col + m_prev
        m_row = jnp.maximum(m_inter, jnp.max(log_d, axis=1, keepdims=True))
        dmat = jnp.exp(log_d - m_row)
        s = lax.dot_general(qh, kh, (((1,), (1,)), ((), ())), preferred_element_type=F32)
        sd = (s * (dmat * scale)).astype(BF16)
        inter = jnp.exp(m_inter - m_row) * scale
        num_ext = (jnp.dot(sd, vext, preferred_element_type=F32)
                   + inter * jnp.dot(qh, c_prev.astype(BF16), preferred_element_type=F32))
        num = num_ext[:, :DV_A]
        den = num_ext[:, DV_A:DV_A + 1]
        hval = num / jnp.maximum(jnp.abs(den), jnp.exp(-m_row))

        hn = _rms(hval, gain_ref[:, h * DV_A:(h + 1) * DV_A])
        og = jax.nn.sigmoid(oa_ref[:, h * DV_A:(h + 1) * DV_A].astype(F32))
        o_ref[:, h * DV_A:(h + 1) * DV_A] = (hn * og).astype(o_ref.dtype)

        log_w = b_last - b_col + ig_col
        m_new = jnp.maximum(b_last + m_prev, jnp.max(log_w, axis=0, keepdims=True))
        w = jnp.exp(log_w - m_new)
        decay = jnp.exp(b_last + m_prev - m_new)
        wv = (w * vext.astype(F32)).astype(BF16)
        c_ref[h] = decay * c_prev + lax.dot_general(
            kh, wv, (((0,), (0,)), ((), ())), preferred_element_type=F32)
        m_ref[h:h + 1, :] = jnp.broadcast_to(m_new, (1, LANE))


def _mlstm(proj, ifp, gain_h):
    S = proj.shape[0]
    L = CHUNK_A
    return pl.pallas_call(
        _mlstm_kernel,
        grid=(S // L,),
        in_specs=[
            pl.BlockSpec((L, QK_A), lambda c: (c, C_QA // QK_A)),
            pl.BlockSpec((L, QK_A), lambda c: (c, C_KA // QK_A)),
            pl.BlockSpec((L, W_A), lambda c: (c, C_VA // W_A)),
            pl.BlockSpec((L, W_A), lambda c: (c, C_OA // W_A)),
            pl.BlockSpec((L, IF_PAD), lambda c: (c, 0)),
            pl.BlockSpec((1, W_A), lambda c: (0, 0)),
        ],
        out_specs=pl.BlockSpec((L, W_A), lambda c: (c, 0)),
        out_shape=jax.ShapeDtypeStruct((S, W_A), BF16),
        scratch_shapes=[
            pltpu.VMEM((NH_A, DQK_A, DV_A + LANE), F32),
            pltpu.VMEM((8, LANE), F32),
        ],
        compiler_params=_cparams(("arbitrary",)),
        name="mlstm",
    )(proj, proj, proj, proj, ifp, gain_h)


LOG2E = 1.4426950408889634
N_SPLIT = 3
V_ROWS = DH_B + 16


def _alibi_slope(h):
    return 2.0 ** (-8.0 * (h + 1) / NH_B)


def _bf16_split_const(x):
    terms, rem = [], float(np.float32(x))
    for _ in range(N_SPLIT):
        t = float(np.float32(rem).astype(BF16))
        terms.append(t)
        rem -= t
    return terms


def _moba_prep_kernel(q_ref, k_ref, v_ref, gq_ref, gk_ref, qst_ref, kaug_ref, vt_ref, kmean_ref, *, n_blk):
    i = pl.program_id(0)
    T = BLOCK_B
    scale = DH_B ** -0.5 * LOG2E
    lane = lax.broadcasted_iota(jnp.int32, (T, LANE), 1)
    pos = lax.broadcasted_iota(jnp.int32, (T, LANE), 0).astype(F32)
    blk_f = i.astype(F32)
    row_is_blk = lax.broadcasted_iota(jnp.int32, (LANE, DH_B), 0) == i
    ones_rows = jnp.where(lax.broadcasted_iota(jnp.int32, (V_ROWS - DH_B, T), 0) == 0, 1.0, 0.0)

    @pl.when(i == 0)
    def _():
        kmean_ref[...] = jnp.zeros_like(kmean_ref)

    for h in range(NH_B):
        sl = slice(h * DH_B, (h + 1) * DH_B)
        qn = _rms(q_ref[:, sl].astype(F32), gq_ref[...])
        qst_ref[h] = (qn * scale).T.astype(qst_ref.dtype)
        kn = _rms(k_ref[:, sl].astype(F32), gk_ref[...])
        kmean_ref[:, sl] = jnp.where(row_is_blk, jnp.mean(kn, axis=0, keepdims=True), kmean_ref[:, sl])
        slope = _alibi_slope(h)
        extra = jnp.where(lane < n_blk + 3 * N_SPLIT, 1.0, 0.0)
        extra = jnp.where(lane < n_blk + 2 * N_SPLIT, slope * pos, extra)
        extra = jnp.where(lane < n_blk + N_SPLIT, slope * T * blk_f, extra)
        extra = jnp.where(lane < n_blk, jnp.where(lane == i, 1.0, 0.0), extra)
        kaug_ref[h, :, :DH_B] = kn.astype(kaug_ref.dtype)
        kaug_ref[h, :, DH_B:] = extra.astype(kaug_ref.dtype)
        vt = jnp.concatenate([v_ref[:, sl].astype(F32).T, ones_rows], axis=0)
        vt_ref[h] = vt.astype(vt_ref.dtype)


def _moba_prep(proj, gq, gk):
    S = proj.shape[0]
    T = BLOCK_B
    n_blk = S // T
    assert n_blk + 3 * N_SPLIT <= LANE
    kern = functools.partial(_moba_prep_kernel, n_blk=n_blk)
    return pl.pallas_call(
        kern,
        grid=(n_blk,),
        in_specs=[
            pl.BlockSpec((T, W_B), lambda i: (i, C_QB // W_B)),
            pl.BlockSpec((T, W_B), lambda i: (i, C_KB // W_B)),
            pl.BlockSpec((T, W_B), lambda i: (i, C_VB // W_B)),
            pl.BlockSpec((1, DH_B), lambda i: (0, 0)),
            pl.BlockSpec((1, DH_B), lambda i: (0, 0)),
        ],
        out_specs=[
            pl.BlockSpec((NH_B, DH_B, T), lambda i: (0, 0, i)),
            pl.BlockSpec((NH_B, T, DH_B + LANE), lambda i: (0, i, 0)),
            pl.BlockSpec((NH_B, None, V_ROWS, T), lambda i: (0, i, 0, 0)),
            pl.BlockSpec((LANE, W_B), lambda i: (0, 0)),
        ],
        out_shape=[
            jax.ShapeDtypeStruct((NH_B, DH_B, S), BF16),
            jax.ShapeDtypeStruct((NH_B, S, DH_B + LANE), BF16),
            jax.ShapeDtypeStruct((NH_B, n_blk, V_ROWS, T), BF16),
            jax.ShapeDtypeStruct((LANE, W_B), F32),
        ],
        compiler_params=_cparams(("arbitrary",)),
        name="moba_prep",
    )(proj, proj, proj, gq, gk)


CHUNK_B = 4


def _moba_attn_kernel(qst_ref, kaug_ref, vt_ref, kmean_ref, o_ref,
                      qaug_ref, s_ref, cmax_ref, m_ref, acc_ref, *, n_blk):
    h = pl.program_id(0)
    sb = pl.program_id(1)
    T = BLOCK_B
    W = CHUNK_B * T
    qst = qst_ref[...]

    gate = jnp.dot(kmean_ref[:n_blk, :].astype(BF16), qst, preferred_element_type=F32)
    blk = lax.broadcasted_iota(jnp.int32, (n_blk, W), 0)
    assert T & (T - 1) == 0
    tok_blk = jnp.right_shift(lax.broadcasted_iota(jnp.int32, (1, W), 1), T.bit_length() - 1)
    own = sb * CHUNK_B + tok_blk
    valid = blk < own
    g = jnp.where(valid, gate, -jnp.inf)
    sel = jnp.zeros((n_blk, W), F32)
    for _ in range(TOPK_B):
        mx = jnp.max(g, axis=0, keepdims=True)
        idx = jnp.min(jnp.where(g == mx, blk, n_blk), axis=0, keepdims=True)
        pick = blk == idx
        sel = jnp.where(pick, 1.0, sel)
        g = jnp.where(pick, -jnp.inf, g)
    keep = jnp.where(valid, sel, 0.0) + jnp.where(blk == own, 1.0, 0.0)
    selbias = jnp.where(keep > 0.0, 0.0, NEG_BIG)

    slope = jnp.exp2(jnp.full((1, W), -8.0 / NH_B, F32) * (h + 1).astype(F32))
    v3 = -slope * (T * own).astype(F32) * LOG2E
    v3_hi = v3.astype(BF16).astype(F32)
    v3_r = v3 - v3_hi
    v3_lo = v3_r.astype(BF16).astype(F32)
    v3_terms = (v3_hi, v3_lo, v3_r - v3_lo)
    c_terms = _bf16_split_const(LOG2E)
    n_tail = LANE - n_blk
    r = lax.broadcasted_iota(jnp.int32, (n_tail, W), 0)
    tail = jnp.zeros((n_tail, W), F32)
    for t in range(N_SPLIT):
        tail = jnp.where(r == t, c_terms[t], tail)
        tail = jnp.where(r == N_SPLIT + t, c_terms[t], tail)
        tail = jnp.where(r == 2 * N_SPLIT + t, v3_terms[t], tail)
    qaug_ref[:DH_B, :] = qst
    qaug_ref[DH_B:DH_B + n_blk, :] = selbias.astype(BF16)
    qaug_ref[DH_B + n_blk:, :] = tail.astype(BF16)

    def block_scores(cid, b):
        start = pl.multiple_of(cid * W + b * T, T)
        return jnp.dot(kaug_ref[pl.ds(start, T), :], qaug_ref[...],
                       preferred_element_type=F32)

    key = lax.broadcasted_iota(jnp.int32, (T, W), 0)
    tok = lax.broadcasted_iota(jnp.int32, (T, W), 1)
    cmax = None
    for b in range(CHUNK_B):
        sblk = jnp.where(key + b * T > tok, NEG_BIG, block_scores(sb, b))
        s_ref[b * T:(b + 1) * T, :] = sblk
        mb = jnp.max(sblk, axis=0, keepdims=True)
        cmax = mb if cmax is None else jnp.maximum(cmax, mb)
    cmax_ref[...] = cmax
    m_ref[...] = jnp.full(m_ref.shape, NEG_BIG, F32)
    acc_ref[...] = jnp.zeros_like(acc_ref)

    def step(cid, next_cid):
        m_prev = m_ref[...]
        m_new = jnp.maximum(m_prev, cmax_ref[...])
        alpha = jnp.exp2(m_prev - m_new)
        pv = jnp.zeros((V_ROWS, W), F32)
        cmax = None
        for b in range(CHUNK_B):
            rows = slice(b * T, (b + 1) * T)
            p = jnp.exp2(s_ref[rows, :] - m_new)
            pv = pv + jnp.dot(vt_ref[cid * CHUNK_B + b], p.astype(BF16), preferred_element_type=F32)
            if next_cid is not None:
                nb = block_scores(next_cid, b)
                s_ref[rows, :] = nb
                mb = jnp.max(nb, axis=0, keepdims=True)
                cmax = mb if cmax is None else jnp.maximum(cmax, mb)
        acc_ref[...] = alpha * acc_ref[...] + pv
        m_ref[...] = m_new
        if cmax is not None:
            cmax_ref[...] = cmax

    def body(n, carry):
        step(jnp.where(n == 0, sb, n - 1), n)
        return carry

    lax.fori_loop(0, sb, body, 0)
    step(jnp.where(sb == 0, sb, sb - 1), None)
    acc = acc_ref[...]
    o_ref[...] = (acc[:DH_B, :] / acc[DH_B:DH_B + 1, :]).T.astype(o_ref.dtype)


def _moba_attn(qst, kaug, vt, kmean):
    S = qst.shape[2]
    T = BLOCK_B
    W = CHUNK_B * T
    n_blk = S // T
    assert n_blk % CHUNK_B == 0
    kern = functools.partial(_moba_attn_kernel, n_blk=n_blk)
    return pl.pallas_call(
        kern,
        grid=(NH_B, n_blk // CHUNK_B),
        in_specs=[
            pl.BlockSpec((None, DH_B, W), lambda h, s: (h, 0, s)),
            pl.BlockSpec((None, S, DH_B + LANE), lambda h, s: (h, 0, 0)),
            pl.BlockSpec((None, n_blk, V_ROWS, T), lambda h, s: (h, 0, 0, 0)),
            pl.BlockSpec((LANE, DH_B), lambda h, s: (0, h)),
        ],
        out_specs=pl.BlockSpec((W, DH_B), lambda h, s: (s, h)),
        out_shape=jax.ShapeDtypeStruct((S, W_B), BF16),
        scratch_shapes=[
            pltpu.VMEM((DH_B + LANE, W), BF16),
            pltpu.VMEM((W, W), F32),
            pltpu.VMEM((1, W), F32),
            pltpu.VMEM((1, W), F32),
            pltpu.VMEM((V_ROWS, W), F32),
        ],
        compiler_params=_cparams(("parallel", "arbitrary")),
        name="moba_attn",
    )(qst, kaug, vt, kmean)


def _merge_out_kernel(x_ref, ha_ref, hb_ref, ga_ref, gb_ref, wa_ref, wb_ref, wo_ref, o_ref):
    ta = jnp.dot(ha_ref[...], wa_ref[...], preferred_element_type=F32)
    tb = jnp.dot(hb_ref[...], wb_ref[...], preferred_element_type=F32)
    merged = (jax.nn.sigmoid(ga_ref[...].astype(F32)) * ta
              + jax.nn.sigmoid(gb_ref[...].astype(F32)) * tb).astype(BF16)
    o_ref[...] = x_ref[...] + jnp.dot(merged, wo_ref[...], preferred_element_type=F32)


def _merge_out(x, h_a, h_b, proj, w_a, w_b, w_o, layer, tm):
    S, D = x.shape
    const = dict(pipeline_mode=pl.Buffered(1))
    return pl.pallas_call(
        _merge_out_kernel,
        grid=(S // tm,),
        in_specs=[
            pl.BlockSpec((tm, D), lambda m: (m, 0)),
            pl.BlockSpec((tm, W_A), lambda m: (m, 0)),
            pl.BlockSpec((tm, W_B), lambda m: (m, 0)),
            pl.BlockSpec((tm, D), lambda m: (m, C_G // D)),
            pl.BlockSpec((tm, D), lambda m: (m, C_G // D + 1)),
            pl.BlockSpec((None, W_A, D), lambda m: (layer, 0, 0), **const),
            pl.BlockSpec((None, W_B, D), lambda m: (layer, 0, 0), **const),
            pl.BlockSpec((None, D, D), lambda m: (layer, 0, 0), **const),
        ],
        out_specs=pl.BlockSpec((tm, D), lambda m: (m, 0)),
        out_shape=jax.ShapeDtypeStruct((S, D), F32),
        compiler_params=_cparams(("parallel",)),
        name="merge_out",
    )(x, h_a, h_b, proj, proj, w_a, w_b, w_o)


def _mlp_kernel(x_ref, g_ref, wu_ref, wd_ref, o_ref, hn_ref):
    @pl.when(pl.program_id(1) == 0)
    def _():
        x = x_ref[...]
        hn_ref[...] = _rms(x, g_ref[...]).astype(BF16)
        o_ref[...] = x

    u = jnp.dot(hn_ref[...], wu_ref[...], preferred_element_type=F32)
    a = jnp.square(jnp.maximum(u, 0.0)).astype(BF16)
    o_ref[...] += jnp.dot(a, wd_ref[...], preferred_element_type=F32)


def _mlp(x, gain, w_up, w_down, layer, tm, tf):
    S, D = x.shape
    FF = w_up.shape[2]
    return pl.pallas_call(
        _mlp_kernel,
        grid=(S // tm, FF // tf),
        in_specs=[
            pl.BlockSpec((tm, D), lambda m, f: (m, 0)),
            pl.BlockSpec((1, D), lambda m, f: (0, 0)),
            pl.BlockSpec((None, D, tf), lambda m, f: (layer, 0, f)),
            pl.BlockSpec((None, tf, D), lambda m, f: (layer, f, 0)),
        ],
        out_specs=pl.BlockSpec((tm, D), lambda m, f: (m, 0)),
        out_shape=jax.ShapeDtypeStruct((S, D), F32),
        scratch_shapes=[pltpu.VMEM((tm, D), BF16)],
        compiler_params=_cparams(("parallel", "arbitrary")),
        name="mlp",
    )(x, gain, w_up, w_down)


IN_PROJ_TM, IN_PROJ_TN = 1024, 2048
MERGE_TM = 512
MLP_TM, MLP_TF = 512, 1024


def _tile(n, pref):
    t = min(n, pref)
    while n % t:
        t -= LANE
    assert t > 0
    return t


def kernel(x, norm_mix, w_in, b_if, norm_h_mlstm, norm_q_moba, norm_k_moba,
           w_branch_a, w_branch_b, w_out, norm_mlp, w_up, w_down):
    B, S, D = x.shape
    assert B == 1 and S % BLOCK_B == 0 and D % LANE == 0
    depth = w_in.shape[0]
    c_if = C_QB
    n_main = w_in.shape[2] - N_IF
    assert n_main == C_G + 2 * D

    w_main, w_if = _w_repack(jnp.swapaxes(w_in, 1, 2), c_if, _tile(D, 256))
    w_a, w_b, w_o = w_branch_a.astype(BF16), w_branch_b.astype(BF16), w_out.astype(BF16)
    w_u, w_d = w_up.astype(BF16), w_down.astype(BF16)

    xs = x.reshape(S, D)
    for l in range(depth):
        bias_if = jnp.pad(b_if[l].astype(F32), (0, IF_PAD - N_IF)).reshape(1, IF_PAD)
        proj, ifp = _in_proj(xs, norm_mix[l].reshape(1, D), w_main, w_if, bias_if, l,
                             _tile(S, IN_PROJ_TM), _tile(n_main, IN_PROJ_TN))
        h_a = _mlstm(proj, ifp, norm_h_mlstm[l].reshape(1, W_A))
        qst, kaug, vt, kmean = _moba_prep(proj, norm_q_moba[l].reshape(1, DH_B),
                                          norm_k_moba[l].reshape(1, DH_B))
        h_b = _moba_attn(qst, kaug, vt, kmean)
        xs = _merge_out(xs, h_a, h_b, proj, w_a, w_b, w_o, l, _tile(S, MERGE_TM))
        xs = _mlp(xs, norm_mlp[l].reshape(1, D), w_u, w_d, l, _tile(S, MLP_TM), _tile(w_up.shape[2], MLP_TF))
    return xs.reshape(B, S, D)
```

```python
import functools

import jax
import jax.numpy as jnp
import numpy as np
from jax import lax
from jax.experimental import pallas as pl
from jax.experimental.pallas import tpu as pltpu

F32 = jnp.float32
BF16 = jnp.bfloat16

NH_A, DQK_A, DV_A, CHUNK_A = 4, 128, 256, 128
GATE_SOFTCAP = 15.0
NH_B, DH_B, BLOCK_B, TOPK_B = 8, 128, 256, 3
EPS = 1e-6

W_A = NH_A * DV_A
W_B = NH_B * DH_B
QK_A = NH_A * DQK_A
N_IF = 2 * NH_A
LANE = 128
IF_PAD = LANE
NEG_BIG = -1e30
VMEM_LIMIT = 56 * 1024 * 1024

C_QA, C_KA, C_VA, C_OA = 0, QK_A, 2 * QK_A, 2 * QK_A + W_A
C_QB = C_OA + W_A
C_KB = C_QB + W_B
C_VB = C_KB + W_B
C_G = C_VB + W_B


def _cparams(sem):
    return pltpu.CompilerParams(dimension_semantics=sem, vmem_limit_bytes=VMEM_LIMIT)


def _rms(x, gain):
    ms = jnp.mean(x * x, axis=-1, keepdims=True)
    return x * lax.rsqrt(ms + EPS) * gain


def _w_repack_kernel(wt_ref, main_ref, if_ref, *, c_if):
    n_cols = wt_ref.shape[0]
    main_ref[:c_if, :] = wt_ref[:c_if, :].astype(main_ref.dtype)
    main_ref[c_if:, :] = wt_ref[c_if + N_IF:n_cols, :].astype(main_ref.dtype)
    row = lax.broadcasted_iota(jnp.int32, if_ref.shape, 0)
    if_ref[...] = jnp.where(row < N_IF, wt_ref[c_if:c_if + IF_PAD, :], 0.0).astype(if_ref.dtype)


def _w_repack(w_in_t, c_if, td):
    depth, n_cols, D = w_in_t.shape
    n_main = n_cols - N_IF
    return pl.pallas_call(
        functools.partial(_w_repack_kernel, c_if=c_if),
        grid=(depth, D // td),
        in_specs=[pl.BlockSpec((None, n_cols, td), lambda l, d: (l, 0, d))],
        out_specs=[
            pl.BlockSpec((None, n_main, td), lambda l, d: (l, 0, d)),
            pl.BlockSpec((None, IF_PAD, td), lambda l, d: (l, 0, d)),
        ],
        out_shape=[
            jax.ShapeDtypeStruct((depth, n_main, D), BF16),
            jax.ShapeDtypeStruct((depth, IF_PAD, D), BF16),
        ],
        compiler_params=_cparams(("parallel", "parallel")),
        name="w_repack",
    )(w_in_t)


_NT = (((1,), (1,)), ((), ()))


def _in_proj_kernel(x_ref, g_ref, w_ref, wif_ref, bif_ref, o_ref, oif_ref, xn_ref):
    @pl.when(pl.program_id(1) == 0)
    def _():
        xn = _rms(x_ref[...], g_ref[...]).astype(BF16)
        xn_ref[...] = xn
        oif_ref[...] = lax.dot_general(xn, wif_ref[...], _NT, preferred_element_type=F32) + bif_ref[...]

    o_ref[...] = lax.dot_general(xn_ref[...], w_ref[...], _NT,
                                 preferred_element_type=F32).astype(o_ref.dtype)


def _in_proj(x, gain, w_main_t, w_if_t, b_if, layer, tm, tn):
    S, D = x.shape
    N = w_main_t.shape[1]
    return pl.pallas_call(
        _in_proj_kernel,
        grid=(S // tm, N // tn),
        in_specs=[
            pl.BlockSpec((tm, D), lambda m, n: (m, 0)),
            pl.BlockSpec((1, D), lambda m, n: (0, 0)),
            pl.BlockSpec((None, tn, D), lambda m, n: (layer, n, 0)),
            pl.BlockSpec((None, IF_PAD, D), lambda m, n: (layer, 0, 0)),
            pl.BlockSpec((1, IF_PAD), lambda m, n: (0, 0)),
        ],
        out_specs=[
            pl.BlockSpec((tm, tn), lambda m, n: (m, n)),
            pl.BlockSpec((tm, IF_PAD), lambda m, n: (m, 0)),
        ],
        out_shape=[
            jax.ShapeDtypeStruct((S, N), BF16),
            jax.ShapeDtypeStruct((S, IF_PAD), F32),
        ],
        scratch_shapes=[pltpu.VMEM((tm, D), BF16)],
        compiler_params=_cparams(("parallel", "arbitrary")),
        name="in_proj",
    )(x, gain, w_main_t, w_if_t, b_if)


def _mlstm_step(first, q_ref, k_ref, v_ref, oa_ref, if_ref, gain_ref, o_ref, c_ref, m_ref):
    L = CHUNK_A

    @pl.when(first)
    def _():
        c_ref[...] = jnp.zeros_like(c_ref)
        m_ref[...] = jnp.zeros_like(m_ref)

    g = if_ref[...]
    gc = GATE_SOFTCAP * jnp.tanh(g / GATE_SOFTCAP)
    col = lax.broadcasted_iota(jnp.int32, (L, LANE), 1)
    row = lax.broadcasted_iota(jnp.int32, (L, LANE), 0)
    log_f = jnp.minimum(gc, 0.0) - jnp.log1p(jnp.exp(-jnp.abs(gc)))
    G = jnp.where(col < NH_A, gc, log_f)
    tril = (row >= col).astype(F32)
    Bc = jnp.dot(tril, G, preferred_element_type=F32, precision=lax.Precision.HIGHEST)
    RT = (G - pltpu.roll(Bc, LANE - NH_A, 1)).T
    causal = row >= col
    scale = DQK_A ** -0.5
    ones_col = jnp.where(col == 0, 1.0, 0.0).astype(BF16)

    for h in range(NH_A):
        qh = q_ref[:, h * DQK_A:(h + 1) * DQK_A]
        kh = k_ref[:, h * DQK_A:(h + 1) * DQK_A]
        vh = v_ref[:, h * DV_A:(h + 1) * DV_A]
        vext = jnp.concatenate([vh, ones_col], axis=1)
        b_col = Bc[:, NH_A + h:NH_A + h + 1]
        b_last = Bc[L - 1:L, NH_A + h:NH_A + h + 1]
        ig_col = G[:, h:h + 1]
        m_prev = m_ref[h:h + 1, 0:1]
        c_prev = c_ref[h]

        log_d = jnp.where(causal, b_col + RT[h:h + 1, :], -jnp.inf)
        m_inter = b_col + m_prev
        m_row = jnp.maximum(m_inter, jnp.max(log_d, axis=1, keepdims=True))
        dmat = jnp.exp(log_d - m_row)
        s = lax.dot_general(qh, kh, (((1,), (1,)), ((), ())), preferred_element_type=F32)
        sd = (s * (dmat * scale)).astype(BF16)
        inter = jnp.exp(m_inter - m_row) * scale
        num_ext = (jnp.dot(sd, vext, preferred_element_type=F32)
                   + inter * jnp.dot(qh, c_prev.astype(BF16), preferred_element_type=F32))
        num = num_ext[:, :DV_A]
        den = num_ext[:, DV_A:DV_A + 1]
        hval = num / jnp.maximum(jnp.abs(den), jnp.exp(-m_row))

        hn = _rms(hval, gain_ref[:, h * DV_A:(h + 1) * DV_A])
        og = jax.nn.sigmoid(oa_ref[:, h * DV_A:(h + 1) * DV_A].astype(F32))
        o_ref[:, h * DV_A:(h + 1) * DV_A] = (hn * og).astype(o_ref.dtype)

        log_w = b_last - b_col + ig_col
        m_new = jnp.maximum(b_last + m_prev, jnp.max(log_w, axis=0, keepdims=True))
        w = jnp.exp(log_w - m_new)
        decay = jnp.exp(b_last + m_prev - m_new)
        wv = (w * vext.astype(F32)).astype(BF16)
        c_ref[h] = decay * c_prev + lax.dot_general(
            kh, wv, (((0,), (0,)), ((), ())), preferred_element_type=F32)
        m_ref[h:h + 1, :] = jnp.broadcast_to(m_new, (1, LANE))


LOG2E = 1.4426950408889634
N_SPLIT = 3
V_ROWS = DH_B + 16


def _alibi_slope(h):
    return 2.0 ** (-8.0 * (h + 1) / NH_B)


def _bf16_split_const(x):
    terms, rem = [], float(np.float32(x))
    for _ in range(N_SPLIT):
        t = float(np.float32(rem).astype(BF16))
        terms.append(t)
        rem -= t
    return terms


def _moba_prep_kernel(q_ref, k_ref, v_ref, gq_ref, gk_ref, qst_ref, kaug_ref, vt_ref, kmean_ref, *, n_blk):
    i = pl.program_id(0)
    T = BLOCK_B
    scale = DH_B ** -0.5 * LOG2E
    lane = lax.broadcasted_iota(jnp.int32, (T, LANE), 1)
    pos = lax.broadcasted_iota(jnp.int32, (T, LANE), 0).astype(F32)
    blk_f = i.astype(F32)
    row_is_blk = lax.broadcasted_iota(jnp.int32, (LANE, DH_B), 0) == i
    ones_rows = jnp.where(lax.broadcasted_iota(jnp.int32, (V_ROWS - DH_B, T), 0) == 0, 1.0, 0.0)

    @pl.when(i == 0)
    def _():
        kmean_ref[...] = jnp.zeros_like(kmean_ref)

    for h in range(NH_B):
        sl = slice(h * DH_B, (h + 1) * DH_B)
        qn = _rms(q_ref[:, sl].astype(F32), gq_ref[...])
        qst_ref[h] = (qn * scale).T.astype(qst_ref.dtype)
        kn = _rms(k_ref[:, sl].astype(F32), gk_ref[...])
        kmean_ref[:, sl] = jnp.where(row_is_blk, jnp.mean(kn, axis=0, keepdims=True), kmean_ref[:, sl])
        slope = _alibi_slope(h)
        extra = jnp.where(lane < n_blk + 3 * N_SPLIT, 1.0, 0.0)
        extra = jnp.where(lane < n_blk + 2 * N_SPLIT, slope * pos, extra)
        extra = jnp.where(lane < n_blk + N_SPLIT, slope * T * blk_f, extra)
        extra = jnp.where(lane < n_blk, jnp.where(lane == i, 1.0, 0.0), extra)
        kaug_ref[h, :, :DH_B] = kn.astype(kaug_ref.dtype)
        kaug_ref[h, :, DH_B:] = extra.astype(kaug_ref.dtype)
        vt = jnp.concatenate([v_ref[:, sl].astype(F32).T, ones_rows], axis=0)
        vt_ref[h] = vt.astype(vt_ref.dtype)


def _moba_prep(proj, gq, gk):
    S = proj.shape[0]
    T = BLOCK_B
    n_blk = S // T
    assert n_blk + 3 * N_SPLIT <= LANE
    kern = functools.partial(_moba_prep_kernel, n_blk=n_blk)
    return pl.pallas_call(
        kern,
        grid=(n_blk,),
        in_specs=[
            pl.BlockSpec((T, W_B), lambda i: (i, C_QB // W_B)),
            pl.BlockSpec((T, W_B), lambda i: (i, C_KB // W_B)),
            pl.BlockSpec((T, W_B), lambda i: (i, C_VB // W_B)),
            pl.BlockSpec((1, DH_B), lambda i: (0, 0)),
            pl.BlockSpec((1, DH_B), lambda i: (0, 0)),
        ],
        out_specs=[
            pl.BlockSpec((NH_B, DH_B, T), lambda i: (0, 0, i)),
            pl.BlockSpec((NH_B, T, DH_B + LANE), lambda i: (0, i, 0)),
            pl.BlockSpec((NH_B, None, V_ROWS, T), lambda i: (0, i, 0, 0)),
            pl.BlockSpec((LANE, W_B), lambda i: (0, 0)),
        ],
        out_shape=[
            jax.ShapeDtypeStruct((NH_B, DH_B, S), BF16),
            jax.ShapeDtypeStruct((NH_B, S, DH_B + LANE), BF16),
            jax.ShapeDtypeStruct((NH_B, n_blk, V_ROWS, T), BF16),
            jax.ShapeDtypeStruct((LANE, W_B), F32),
        ],
        compiler_params=_cparams(("arbitrary",)),
        name="moba_prep",
    )(proj, proj, proj, gq, gk)


CHUNK_B = 4


def _branches_kernel(qst_ref, kaug_ref, vt_ref, kmean_ref,
                     qa_ref, ka_ref, va_ref, oa_ref, if_ref, gain_ref,
                     o_ref, ha_ref,
                     qaug_ref, s_ref, cmax_ref, m_ref, acc_ref, c_ref, ma_ref, *, n_blk):
    h = pl.program_id(0)
    sb = pl.program_id(1)
    _mlstm_step((h == 0) & (sb == 0), qa_ref, ka_ref, va_ref, oa_ref, if_ref, gain_ref,
                ha_ref, c_ref, ma_ref)
    T = BLOCK_B
    W = CHUNK_B * T
    qst = qst_ref[...]

    gate = jnp.dot(kmean_ref[:n_blk, :].astype(BF16), qst, preferred_element_type=F32)
    blk = lax.broadcasted_iota(jnp.int32, (n_blk, W), 0)
    assert T & (T - 1) == 0
    tok_blk = jnp.right_shift(lax.broadcasted_iota(jnp.int32, (1, W), 1), T.bit_length() - 1)
    own = sb * CHUNK_B + tok_blk
    valid = blk < own
    g = jnp.where(valid, gate, -jnp.inf)
    sel = jnp.zeros((n_blk, W), F32)
    for _ in range(TOPK_B):
        mx = jnp.max(g, axis=0, keepdims=True)
        idx = jnp.min(jnp.where(g == mx, blk, n_blk), axis=0, keepdims=True)
        pick = blk == idx
        sel = jnp.where(pick, 1.0, sel)
        g = jnp.where(pick, -jnp.inf, g)
    keep = jnp.where(valid, sel, 0.0) + jnp.where(blk == own, 1.0, 0.0)
    selbias = jnp.where(keep > 0.0, 0.0, NEG_BIG)

    slope = jnp.exp2(jnp.full((1, W), -8.0 / NH_B, F32) * (h + 1).astype(F32))
    v3 = -slope * (T * own).astype(F32) * LOG2E
    v3_hi = v3.astype(BF16).astype(F32)
    v3_r = v3 - v3_hi
    v3_lo = v3_r.astype(BF16).astype(F32)
    v3_terms = (v3_hi, v3_lo, v3_r - v3_lo)
    c_terms = _bf16_split_const(LOG2E)
    n_tail = LANE - n_blk
    r = lax.broadcasted_iota(jnp.int32, (n_tail, W), 0)
    tail = jnp.zeros((n_tail, W), F32)
    for t in range(N_SPLIT):
        tail = jnp.where(r == t, c_terms[t], tail)
        tail = jnp.where(r == N_SPLIT + t, c_terms[t], tail)
        tail = jnp.where(r == 2 * N_SPLIT + t, v3_terms[t], tail)
    qaug_ref[:DH_B, :] = qst
    qaug_ref[DH_B:DH_B + n_blk, :] = selbias.astype(BF16)
    qaug_ref[DH_B + n_blk:, :] = tail.astype(BF16)

    def block_scores(cid, b):
        start = pl.multiple_of(cid * W + b * T, T)
        return jnp.dot(kaug_ref[pl.ds(start, T), :], qaug_ref[...],
                       preferred_element_type=F32)

    key = lax.broadcasted_iota(jnp.int32, (T, W), 0)
    tok = lax.broadcasted_iota(jnp.int32, (T, W), 1)
    cmax = None
    for b in range(CHUNK_B):
        sblk = jnp.where(key + b * T > tok, NEG_BIG, block_scores(sb, b))
        s_ref[b * T:(b + 1) * T, :] = sblk
        mb = jnp.max(sblk, axis=0, keepdims=True)
        cmax = mb if cmax is None else jnp.maximum(cmax, mb)
    cmax_ref[...] = cmax
    m_ref[...] = jnp.full(m_ref.shape, NEG_BIG, F32)
    acc_ref[...] = jnp.zeros_like(acc_ref)

    def step(cid, next_cid):
        m_prev = m_ref[...]
        m_new = jnp.maximum(m_prev, cmax_ref[...])
        alpha = jnp.exp2(m_prev - m_new)
        pv = jnp.zeros((V_ROWS, W), F32)
        cmax = None
        for b in range(CHUNK_B):
            rows = slice(b * T, (b + 1) * T)
            p = jnp.exp2(s_ref[rows, :] - m_new)
            pv = pv + jnp.dot(vt_ref[cid * CHUNK_B + b], p.astype(BF16), preferred_element_type=F32)
            if next_cid is not None:
                nb = block_scores(next_cid, b)
                s_ref[rows, :] = nb
                mb = jnp.max(nb, axis=0, keepdims=True)
                cmax = mb if cmax is None else jnp.maximum(cmax, mb)
        acc_ref[...] = alpha * acc_ref[...] + pv
        m_ref[...] = m_new
        if cmax is not None:
            cmax_ref[...] = cmax

    def body(n, carry):
        step(jnp.where(n == 0, sb, n - 1), n)
        return carry

    lax.fori_loop(0, sb, body, 0)
    step(jnp.where(sb == 0, sb, sb - 1), None)
    acc = acc_ref[...]
    o_ref[...] = (acc[:DH_B, :] / acc[DH_B:DH_B + 1, :]).T.astype(o_ref.dtype)


def _branches(qst, kaug, vt, kmean, proj, ifp, gain_h):
    S = qst.shape[2]
    T = BLOCK_B
    W = CHUNK_B * T
    L = CHUNK_A
    n_blk = S // T
    n_sb = n_blk // CHUNK_B
    assert n_blk % CHUNK_B == 0 and NH_B * n_sb * L == S
    kern = functools.partial(_branches_kernel, n_blk=n_blk)

    def chunk(col):
        return lambda h, s: (h * n_sb + s, col)

    return pl.pallas_call(
        kern,
        grid=(NH_B, n_sb),
        in_specs=[
            pl.BlockSpec((None, DH_B, W), lambda h, s: (h, 0, s)),
            pl.BlockSpec((None, S, DH_B + LANE), lambda h, s: (h, 0, 0)),
            pl.BlockSpec((None, n_blk, V_ROWS, T), lambda h, s: (h, 0, 0, 0)),
            pl.BlockSpec((LANE, DH_B), lambda h, s: (0, h)),
            pl.BlockSpec((L, QK_A), chunk(C_QA // QK_A)),
            pl.BlockSpec((L, QK_A), chunk(C_KA // QK_A)),
            pl.BlockSpec((L, W_A), chunk(C_VA // W_A)),
            pl.BlockSpec((L, W_A), chunk(C_OA // W_A)),
            pl.BlockSpec((L, IF_PAD), chunk(0)),
            pl.BlockSpec((1, W_A), lambda h, s: (0, 0)),
        ],
        out_specs=[
            pl.BlockSpec((W, DH_B), lambda h, s: (s, h)),
            pl.BlockSpec((L, W_A), chunk(0)),
        ],
        out_shape=[
            jax.ShapeDtypeStruct((S, W_B), BF16),
            jax.ShapeDtypeStruct((S, W_A), BF16),
        ],
        scratch_shapes=[
            pltpu.VMEM((DH_B + LANE, W), BF16),
            pltpu.VMEM((W, W), F32),
            pltpu.VMEM((1, W), F32),
            pltpu.VMEM((1, W), F32),
            pltpu.VMEM((V_ROWS, W), F32),
            pltpu.VMEM((NH_A, DQK_A, DV_A + LANE), F32),
            pltpu.VMEM((8, LANE), F32),
        ],
        compiler_params=_cparams(("arbitrary", "arbitrary")),
        name="branches",
    )(qst, kaug, vt, kmean, proj, proj, proj, proj, ifp, gain_h)


def _merge_out_kernel(x_ref, ha_ref, hb_ref, ga_ref, gb_ref, wa_ref, wb_ref, wo_ref, o_ref):
    ta = jnp.dot(ha_ref[...], wa_ref[...], preferred_element_type=F32)
    tb = jnp.dot(hb_ref[...], wb_ref[...], preferred_element_type=F32)
    merged = (jax.nn.sigmoid(ga_ref[...].astype(F32)) * ta
              + jax.nn.sigmoid(gb_ref[...].astype(F32)) * tb).astype(BF16)
    o_ref[...] = x_ref[...] + jnp.dot(merged, wo_ref[...], preferred_element_type=F32)


def _merge_out(x, h_a, h_b, proj, w_a, w_b, w_o, layer, tm):
    S, D = x.shape
    const = dict(pipeline_mode=pl.Buffered(1))
    return pl.pallas_call(
        _merge_out_kernel,
        grid=(S // tm,),
        in_specs=[
            pl.BlockSpec((tm, D), lambda m: (m, 0)),
            pl.BlockSpec((tm, W_A), lambda m: (m, 0)),
            pl.BlockSpec((tm, W_B), lambda m: (m, 0)),
            pl.BlockSpec((tm, D), lambda m: (m, C_G // D)),
            pl.BlockSpec((tm, D), lambda m: (m, C_G // D + 1)),
            pl.BlockSpec((None, W_A, D), lambda m: (layer, 0, 0), **const),
            pl.BlockSpec((None, W_B, D), lambda m: (layer, 0, 0), **const),
            pl.BlockSpec((None, D, D), lambda m: (layer, 0, 0), **const),
        ],
        out_specs=pl.BlockSpec((tm, D), lambda m: (m, 0)),
        out_shape=jax.ShapeDtypeStruct((S, D), F32),
        compiler_params=_cparams(("parallel",)),
        name="merge_out",
    )(x, h_a, h_b, proj, proj, w_a, w_b, w_o)


def _mlp_kernel(x_ref, g_ref, wu_ref, wd_ref, o_ref, hn_ref):
    @pl.when(pl.program_id(1) == 0)
    def _():
        x = x_ref[...]
        hn_ref[...] = _rms(x, g_ref[...]).astype(BF16)
        o_ref[...] = x

    u = jnp.dot(hn_ref[...], wu_ref[...], preferred_element_type=F32)
    a = jnp.square(jnp.maximum(u, 0.0)).astype(BF16)
    o_ref[...] += jnp.dot(a, wd_ref[...], preferred_element_type=F32)


def _mlp(x, gain, w_up, w_down, layer, tm, tf):
    S, D = x.shape
    FF = w_up.shape[2]
    return pl.pallas_call(
        _mlp_kernel,
        grid=(S // tm, FF // tf),
        in_specs=[
            pl.BlockSpec((tm, D), lambda m, f: (m, 0)),
            pl.BlockSpec((1, D), lambda m, f: (0, 0)),
            pl.BlockSpec((None, D, tf), lambda m, f: (layer, 0, f)),
            pl.BlockSpec((None, tf, D), lambda m, f: (layer, f, 0)),
        ],
        out_specs=pl.BlockSpec((tm, D), lambda m, f: (m, 0)),
        out_shape=jax.ShapeDtypeStruct((S, D), F32),
        scratch_shapes=[pltpu.VMEM((tm, D), BF16)],
        compiler_params=_cparams(("parallel", "arbitrary")),
        name="mlp",
    )(x, gain, w_up, w_down)


IN_PROJ_TM, IN_PROJ_TN = 1024, 2048
MERGE_TM = 512
MLP_TM, MLP_TF = 512, 1024


def _tile(n, pref):
    t = min(n, pref)
    while n % t:
        t -= LANE
    assert t > 0
    return t


def kernel(x, norm_mix, w_in, b_if, norm_h_mlstm, norm_q_moba, norm_k_moba,
           w_branch_a, w_branch_b, w_out, norm_mlp, w_up, w_down):
    B, S, D = x.shape
    assert B == 1 and S % BLOCK_B == 0 and D % LANE == 0
    depth = w_in.shape[0]
    c_if = C_QB
    n_main = w_in.shape[2] - N_IF
    assert n_main == C_G + 2 * D

    w_main, w_if = _w_repack(jnp.swapaxes(w_in, 1, 2), c_if, _tile(D, 256))
    w_a, w_b, w_o = w_branch_a.astype(BF16), w_branch_b.astype(BF16), w_out.astype(BF16)
    w_u, w_d = w_up.astype(BF16), w_down.astype(BF16)

    xs = x.reshape(S, D)
    for l in range(depth):
        bias_if = jnp.pad(b_if[l].astype(F32), (0, IF_PAD - N_IF)).reshape(1, IF_PAD)
        proj, ifp = _in_proj(xs, norm_mix[l].reshape(1, D), w_main, w_if, bias_if, l,
                             _tile(S, IN_PROJ_TM), _tile(n_main, IN_PROJ_TN))
        qst, kaug, vt, kmean = _moba_prep(proj, norm_q_moba[l].reshape(1, DH_B),
                                          norm_k_moba[l].reshape(1, DH_B))
        h_b, h_a = _branches(qst, kaug, vt, kmean, proj, ifp, norm_h_mlstm[l].reshape(1, W_A))
        xs = _merge_out(xs, h_a, h_b, proj, w_a, w_b, w_o, l, _tile(S, MERGE_TM))
        xs = _mlp(xs, norm_mlp[l].reshape(1, D), w_u, w_d, l, _tile(S, MLP_TM), _tile(w_up.shape[2], MLP_TF))
    return xs.reshape(B, S, D)
```

```python
import functools

import jax
import jax.numpy as jnp
import numpy as np
from jax import lax
from jax.experimental import pallas as pl
from jax.experimental.pallas import tpu as pltpu

F32 = jnp.float32
BF16 = jnp.bfloat16

NH_A, DQK_A, DV_A, CHUNK_A = 4, 128, 256, 128
GATE_SOFTCAP = 15.0
NH_B, DH_B, BLOCK_B, TOPK_B = 8, 128, 256, 3
EPS = 1e-6

W_A = NH_A * DV_A
W_B = NH_B * DH_B
QK_A = NH_A * DQK_A
N_IF = 2 * NH_A
LANE = 128
IF_PAD = LANE
NEG_BIG = -1e30
VMEM_LIMIT = 56 * 1024 * 1024

C_QA, C_KA, C_VA, C_OA = 0, QK_A, 2 * QK_A, 2 * QK_A + W_A
C_QB = C_OA + W_A
C_KB = C_QB + W_B
C_VB = C_KB + W_B
C_G = C_VB + W_B


def _cparams(sem):
    return pltpu.CompilerParams(dimension_semantics=sem, vmem_limit_bytes=VMEM_LIMIT)


def _rms(x, gain):
    ms = jnp.mean(x * x, axis=-1, keepdims=True)
    return x * lax.rsqrt(ms + EPS) * gain


def _w_repack_kernel(wt_ref, main_ref, if_ref, *, c_if):
    n_cols = wt_ref.shape[0]
    main_ref[:c_if, :] = wt_ref[:c_if, :].astype(main_ref.dtype)
    main_ref[c_if:, :] = wt_ref[c_if + N_IF:n_cols, :].astype(main_ref.dtype)
    row = lax.broadcasted_iota(jnp.int32, if_ref.shape, 0)
    if_ref[...] = jnp.where(row < N_IF, wt_ref[c_if:c_if + IF_PAD, :], 0.0).astype(if_ref.dtype)


def _w_repack(w_in_t, c_if, td):
    depth, n_cols, D = w_in_t.shape
    n_main = n_cols - N_IF
    return pl.pallas_call(
        functools.partial(_w_repack_kernel, c_if=c_if),
        grid=(depth, D // td),
        in_specs=[pl.BlockSpec((None, n_cols, td), lambda l, d: (l, 0, d))],
        out_specs=[
            pl.BlockSpec((None, n_main, td), lambda l, d: (l, 0, d)),
            pl.BlockSpec((None, IF_PAD, td), lambda l, d: (l, 0, d)),
        ],
        out_shape=[
            jax.ShapeDtypeStruct((depth, n_main, D), BF16),
            jax.ShapeDtypeStruct((depth, IF_PAD, D), BF16),
        ],
        compiler_params=_cparams(("parallel", "parallel")),
        name="w_repack",
    )(w_in_t)


_NT = (((1,), (1,)), ((), ()))


def _in_proj_kernel(x_ref, g_ref, w_ref, wif_ref, bif_ref, o_ref, oif_ref, xn_ref):
    @pl.when(pl.program_id(1) == 0)
    def _():
        xn = _rms(x_ref[...], g_ref[...]).astype(BF16)
        xn_ref[...] = xn
        oif_ref[...] = lax.dot_general(xn, wif_ref[...], _NT, preferred_element_type=F32) + bif_ref[...]

    o_ref[...] = lax.dot_general(xn_ref[...], w_ref[...], _NT,
                                 preferred_element_type=F32).astype(o_ref.dtype)


def _in_proj(x, gain, w_main_t, w_if_t, b_if, layer, tm, tn):
    S, D = x.shape
    N = w_main_t.shape[1]
    return pl.pallas_call(
        _in_proj_kernel,
        grid=(S // tm, N // tn),
        in_specs=[
            pl.BlockSpec((tm, D), lambda m, n: (m, 0)),
            pl.BlockSpec((1, D), lambda m, n: (0, 0)),
            pl.BlockSpec((None, tn, D), lambda m, n: (layer, n, 0)),
            pl.BlockSpec((None, IF_PAD, D), lambda m, n: (layer, 0, 0)),
            pl.BlockSpec((1, IF_PAD), lambda m, n: (0, 0)),
        ],
        out_specs=[
            pl.BlockSpec((tm, tn), lambda m, n: (m, n)),
            pl.BlockSpec((tm, IF_PAD), lambda m, n: (m, 0)),
        ],
        out_shape=[
            jax.ShapeDtypeStruct((S, N), BF16),
            jax.ShapeDtypeStruct((S, IF_PAD), F32),
        ],
        scratch_shapes=[pltpu.VMEM((tm, D), BF16)],
        compiler_params=_cparams(("parallel", "arbitrary")),
        name="in_proj",
    )(x, gain, w_main_t, w_if_t, b_if)


def _mlstm_step(first, q_ref, k_ref, v_ref, oa_ref, if_ref, gain_ref, o_ref, c_ref, m_ref):
    L = CHUNK_A

    @pl.when(first)
    def _():
        c_ref[...] = jnp.zeros_like(c_ref)
        m_ref[...] = jnp.zeros_like(m_ref)

    g = if_ref[...]
    gc = GATE_SOFTCAP * jnp.tanh(g / GATE_SOFTCAP)
    col = lax.broadcasted_iota(jnp.int32, (L, LANE), 1)
    row = lax.broadcasted_iota(jnp.int32, (L, LANE), 0)
    log_f = jnp.minimum(gc, 0.0) - jnp.log1p(jnp.exp(-jnp.abs(gc)))
    G = jnp.where(col < NH_A, gc, log_f)
    tril = (row >= col).astype(F32)
    Bc = jnp.dot(tril, G, preferred_element_type=F32, precision=lax.Precision.HIGHEST)
    RT = (G - pltpu.roll(Bc, LANE - NH_A, 1)).T
    causal = row >= col
    scale = DQK_A ** -0.5
    ones_col = jnp.where(col == 0, 1.0, 0.0).astype(BF16)

    def head(h):
        qh = q_ref[:, h * DQK_A:(h + 1) * DQK_A]
        kh = k_ref[:, h * DQK_A:(h + 1) * DQK_A]
        vh = v_ref[:, h * DV_A:(h + 1) * DV_A]
        vext = jnp.concatenate([vh, ones_col], axis=1)
        b_col = Bc[:, NH_A + h:NH_A + h + 1]
        b_last = Bc[L - 1:L, NH_A + h:NH_A + h + 1]
        ig_col = G[:, h:h + 1]
        m_prev = m_ref[h:h + 1, 0:1]
        c_prev = c_ref[h]

        log_d = jnp.where(causal, b_col + RT[h:h + 1, :], -jnp.inf)
        m_inter = b_col + m_prev
        m_row = jnp.maximum(m_inter, jnp.max(log_d, axis=1, keepdims=True))
        dmat = jnp.exp(log_d - m_row)
        s = lax.dot_general(qh, kh, (((1,), (1,)), ((), ())), preferred_element_type=F32)
        sd = (s * (dmat * scale)).astype(BF16)
        inter = jnp.exp(m_inter - m_row) * scale
        num_ext = (jnp.dot(sd, vext, preferred_element_type=F32)
                   + inter * jnp.dot(qh, c_prev.astype(BF16), preferred_element_type=F32))
        num = num_ext[:, :DV_A]
        den = num_ext[:, DV_A:DV_A + 1]
        hval = num / jnp.maximum(jnp.abs(den), jnp.exp(-m_row))

        hn = _rms(hval, gain_ref[:, h * DV_A:(h + 1) * DV_A])
        og = jax.nn.sigmoid(oa_ref[:, h * DV_A:(h + 1) * DV_A].astype(F32))
        o_ref[:, h * DV_A:(h + 1) * DV_A] = (hn * og).astype(o_ref.dtype)

        log_w = b_last - b_col + ig_col
        m_new = jnp.maximum(b_last + m_prev, jnp.max(log_w, axis=0, keepdims=True))
        w = jnp.exp(log_w - m_new)
        decay = jnp.exp(b_last + m_prev - m_new)
        wv = (w * vext.astype(F32)).astype(BF16)
        c_ref[h] = decay * c_prev + lax.dot_general(
            kh, wv, (((0,), (0,)), ((), ())), preferred_element_type=F32)
        m_ref[h:h + 1, :] = jnp.broadcast_to(m_new, (1, LANE))

    return [functools.partial(head, h) for h in range(NH_A)]


LOG2E = 1.4426950408889634
N_SPLIT = 3
V_ROWS = DH_B + 16


def _alibi_slope(h):
    return 2.0 ** (-8.0 * (h + 1) / NH_B)


def _bf16_split_const(x):
    terms, rem = [], float(np.float32(x))
    for _ in range(N_SPLIT):
        t = float(np.float32(rem).astype(BF16))
        terms.append(t)
        rem -= t
    return terms


def _moba_prep_kernel(q_ref, k_ref, v_ref, gq_ref, gk_ref, qst_ref, kaug_ref, vt_ref, kmean_ref, *, n_blk):
    i = pl.program_id(0)
    T = BLOCK_B
    scale = DH_B ** -0.5 * LOG2E
    lane = lax.broadcasted_iota(jnp.int32, (T, LANE), 1)
    pos = lax.broadcasted_iota(jnp.int32, (T, LANE), 0).astype(F32)
    blk_f = i.astype(F32)
    row_is_blk = lax.broadcasted_iota(jnp.int32, (LANE, DH_B), 0) == i
    ones_rows = jnp.where(lax.broadcasted_iota(jnp.int32, (V_ROWS - DH_B, T), 0) == 0, 1.0, 0.0)

    @pl.when(i == 0)
    def _():
        kmean_ref[...] = jnp.zeros_like(kmean_ref)

    for h in range(NH_B):
        sl = slice(h * DH_B, (h + 1) * DH_B)
        qn = _rms(q_ref[:, sl].astype(F32), gq_ref[...])
        qst_ref[h] = (qn * scale).T.astype(qst_ref.dtype)
        kn = _rms(k_ref[:, sl].astype(F32), gk_ref[...])
        kmean_ref[:, sl] = jnp.where(row_is_blk, jnp.mean(kn, axis=0, keepdims=True), kmean_ref[:, sl])
        slope = _alibi_slope(h)
        extra = jnp.where(lane < n_blk + 3 * N_SPLIT, 1.0, 0.0)
        extra = jnp.where(lane < n_blk + 2 * N_SPLIT, slope * pos, extra)
        extra = jnp.where(lane < n_blk + N_SPLIT, slope * T * blk_f, extra)
        extra = jnp.where(lane < n_blk, jnp.where(lane == i, 1.0, 0.0), extra)
        kaug_ref[h, :, :DH_B] = kn.astype(kaug_ref.dtype)
        kaug_ref[h, :, DH_B:] = extra.astype(kaug_ref.dtype)
        vt = jnp.concatenate([v_ref[:, sl].astype(F32).T, ones_rows], axis=0)
        vt_ref[h] = vt.astype(vt_ref.dtype)


def _moba_prep(proj, gq, gk):
    S = proj.shape[0]
    T = BLOCK_B
    n_blk = S // T
    assert n_blk + 3 * N_SPLIT <= LANE
    kern = functools.partial(_moba_prep_kernel, n_blk=n_blk)
    return pl.pallas_call(
        kern,
        grid=(n_blk,),
        in_specs=[
            pl.BlockSpec((T, W_B), lambda i: (i, C_QB // W_B)),
            pl.BlockSpec((T, W_B), lambda i: (i, C_KB // W_B)),
            pl.BlockSpec((T, W_B), lambda i: (i, C_VB // W_B)),
            pl.BlockSpec((1, DH_B), lambda i: (0, 0)),
            pl.BlockSpec((1, DH_B), lambda i: (0, 0)),
        ],
        out_specs=[
            pl.BlockSpec((NH_B, DH_B, T), lambda i: (0, 0, i)),
            pl.BlockSpec((NH_B, T, DH_B + LANE), lambda i: (0, i, 0)),
            pl.BlockSpec((NH_B, None, V_ROWS, T), lambda i: (0, i, 0, 0)),
            pl.BlockSpec((LANE, W_B), lambda i: (0, 0)),
        ],
        out_shape=[
            jax.ShapeDtypeStruct((NH_B, DH_B, S), BF16),
            jax.ShapeDtypeStruct((NH_B, S, DH_B + LANE), BF16),
            jax.ShapeDtypeStruct((NH_B, n_blk, V_ROWS, T), BF16),
            jax.ShapeDtypeStruct((LANE, W_B), F32),
        ],
        compiler_params=_cparams(("arbitrary",)),
        name="moba_prep",
    )(proj, proj, proj, gq, gk)


CHUNK_B = 4


def _branches_kernel(qst_ref, kaug_ref, vt_ref, kmean_ref,
                     qa_ref, ka_ref, va_ref, oa_ref, if_ref, gain_ref,
                     o_ref, ha_ref,
                     qaug_ref, s_ref, cmax_ref, m_ref, acc_ref, c_ref, ma_ref, *, n_blk):
    h = pl.program_id(0)
    sb = pl.program_id(1)
    mlstm_heads = _mlstm_step((h == 0) & (sb == 0), qa_ref, ka_ref, va_ref, oa_ref, if_ref, gain_ref,
                              ha_ref, c_ref, ma_ref)
    T = BLOCK_B
    W = CHUNK_B * T
    qst = qst_ref[...]

    gate = jnp.dot(kmean_ref[:n_blk, :].astype(BF16), qst, preferred_element_type=F32)
    blk = lax.broadcasted_iota(jnp.int32, (n_blk, W), 0)
    assert T & (T - 1) == 0
    tok_blk = jnp.right_shift(lax.broadcasted_iota(jnp.int32, (1, W), 1), T.bit_length() - 1)
    own = sb * CHUNK_B + tok_blk
    valid = blk < own
    g = jnp.where(valid, gate, -jnp.inf)
    sel = jnp.zeros((n_blk, W), F32)
    for _ in range(TOPK_B):
        mx = jnp.max(g, axis=0, keepdims=True)
        idx = jnp.min(jnp.where(g == mx, blk, n_blk), axis=0, keepdims=True)
        pick = blk == idx
        sel = jnp.where(pick, 1.0, sel)
        g = jnp.where(pick, -jnp.inf, g)
    keep = jnp.where(valid, sel, 0.0) + jnp.where(blk == own, 1.0, 0.0)
    selbias = jnp.where(keep > 0.0, 0.0, NEG_BIG)

    slope = jnp.exp2(jnp.full((1, W), -8.0 / NH_B, F32) * (h + 1).astype(F32))
    v3 = -slope * (T * own).astype(F32) * LOG2E
    v3_hi = v3.astype(BF16).astype(F32)
    v3_r = v3 - v3_hi
    v3_lo = v3_r.astype(BF16).astype(F32)
    v3_terms = (v3_hi, v3_lo, v3_r - v3_lo)
    c_terms = _bf16_split_const(LOG2E)
    n_tail = LANE - n_blk
    r = lax.broadcasted_iota(jnp.int32, (n_tail, W), 0)
    tail = jnp.zeros((n_tail, W), F32)
    for t in range(N_SPLIT):
        tail = jnp.where(r == t, c_terms[t], tail)
        tail = jnp.where(r == N_SPLIT + t, c_terms[t], tail)
        tail = jnp.where(r == 2 * N_SPLIT + t, v3_terms[t], tail)
    qaug_ref[:DH_B, :] = qst
    qaug_ref[DH_B:DH_B + n_blk, :] = selbias.astype(BF16)
    qaug_ref[DH_B + n_blk:, :] = tail.astype(BF16)

    def block_scores(cid, b):
        start = pl.multiple_of(cid * W + b * T, T)
        return jnp.dot(kaug_ref[pl.ds(start, T), :], qaug_ref[...],
                       preferred_element_type=F32)

    key = lax.broadcasted_iota(jnp.int32, (T, W), 0)
    tok = lax.broadcasted_iota(jnp.int32, (T, W), 1)
    cmax = None
    for b in range(CHUNK_B):
        sblk = jnp.where(key + b * T > tok, NEG_BIG, block_scores(sb, b))
        s_ref[b * T:(b + 1) * T, :] = sblk
        mb = jnp.max(sblk, axis=0, keepdims=True)
        cmax = mb if cmax is None else jnp.maximum(cmax, mb)
        if mlstm_heads:
            mlstm_heads.pop(0)()
    while mlstm_heads:
        mlstm_heads.pop(0)()
    cmax_ref[...] = cmax
    m_ref[...] = jnp.full(m_ref.shape, NEG_BIG, F32)
    acc_ref[...] = jnp.zeros_like(acc_ref)

    def step(cid, next_cid):
        m_prev = m_ref[...]
        m_new = jnp.maximum(m_prev, cmax_ref[...])
        alpha = jnp.exp2(m_prev - m_new)
        pv = jnp.zeros((V_ROWS, W), F32)
        cmax = None
        for b in range(CHUNK_B):
            rows = slice(b * T, (b + 1) * T)
            p = jnp.exp2(s_ref[rows, :] - m_new)
            pv = pv + jnp.dot(vt_ref[cid * CHUNK_B + b], p.astype(BF16), preferred_element_type=F32)
            if next_cid is not None:
                nb = block_scores(next_cid, b)
                s_ref[rows, :] = nb
                mb = jnp.max(nb, axis=0, keepdims=True)
                cmax = mb if cmax is None else jnp.maximum(cmax, mb)
        acc_ref[...] = alpha * acc_ref[...] + pv
        m_ref[...] = m_new
        if cmax is not None:
            cmax_ref[...] = cmax

    def body(n, carry):
        step(jnp.where(n == 0, sb, n - 1), n)
        return carry

    lax.fori_loop(0, sb, body, 0)
    step(jnp.where(sb == 0, sb, sb - 1), None)
    acc = acc_ref[...]
    o_ref[...] = (acc[:DH_B, :] / acc[DH_B:DH_B + 1, :]).T.astype(o_ref.dtype)


def _branches(qst, kaug, vt, kmean, proj, ifp, gain_h):
    S = qst.shape[2]
    T = BLOCK_B
    W = CHUNK_B * T
    L = CHUNK_A
    n_blk = S // T
    n_sb = n_blk // CHUNK_B
    assert n_blk % CHUNK_B == 0 and NH_B * n_sb * L == S
    kern = functools.partial(_branches_kernel, n_blk=n_blk)

    def chunk(col):
        return lambda h, s: (h * n_sb + s, col)

    return pl.pallas_call(
        kern,
        grid=(NH_B, n_sb),
        in_specs=[
            pl.BlockSpec((None, DH_B, W), lambda h, s: (h, 0, s)),
            pl.BlockSpec((None, S, DH_B + LANE), lambda h, s: (h, 0, 0)),
            pl.BlockSpec((None, n_blk, V_ROWS, T), lambda h, s: (h, 0, 0, 0)),
            pl.BlockSpec((LANE, DH_B), lambda h, s: (0, h)),
            pl.BlockSpec((L, QK_A), chunk(C_QA // QK_A)),
            pl.BlockSpec((L, QK_A), chunk(C_KA // QK_A)),
            pl.BlockSpec((L, W_A), chunk(C_VA // W_A)),
            pl.BlockSpec((L, W_A), chunk(C_OA // W_A)),
            pl.BlockSpec((L, IF_PAD), chunk(0)),
            pl.BlockSpec((1, W_A), lambda h, s: (0, 0)),
        ],
        out_specs=[
            pl.BlockSpec((W, DH_B), lambda h, s: (s, h)),
            pl.BlockSpec((L, W_A), chunk(0)),
        ],
        out_shape=[
            jax.ShapeDtypeStruct((S, W_B), BF16),
            jax.ShapeDtypeStruct((S, W_A), BF16),
        ],
        scratch_shapes=[
            pltpu.VMEM((DH_B + LANE, W), BF16),
            pltpu.VMEM((W, W), F32),
            pltpu.VMEM((1, W), F32),
            pltpu.VMEM((1, W), F32),
            pltpu.VMEM((V_ROWS, W), F32),
            pltpu.VMEM((NH_A, DQK_A, DV_A + LANE), F32),
            pltpu.VMEM((8, LANE), F32),
        ],
        compiler_params=_cparams(("arbitrary", "arbitrary")),
        name="branches",
    )(qst, kaug, vt, kmean, proj, proj, proj, proj, ifp, gain_h)


def _merge_out_kernel(x_ref, ha_ref, hb_ref, ga_ref, gb_ref, wa_ref, wb_ref, wo_ref, o_ref):
    ta = jnp.dot(ha_ref[...], wa_ref[...], preferred_element_type=F32)
    tb = jnp.dot(hb_ref[...], wb_ref[...], preferred_element_type=F32)
    merged = (jax.nn.sigmoid(ga_ref[...].astype(F32)) * ta
              + jax.nn.sigmoid(gb_ref[...].astype(F32)) * tb).astype(BF16)
    o_ref[...] = x_ref[...] + jnp.dot(merged, wo_ref[...], preferred_element_type=F32)


def _merge_out(x, h_a, h_b, proj, w_a, w_b, w_o, layer, tm):
    S, D = x.shape
    const = dict(pipeline_mode=pl.Buffered(1))
    return pl.pallas_call(
        _merge_out_kernel,
        grid=(S // tm,),
        in_specs=[
            pl.BlockSpec((tm, D), lambda m: (m, 0)),
            pl.BlockSpec((tm, W_A), lambda m: (m, 0)),
            pl.BlockSpec((tm, W_B), lambda m: (m, 0)),
            pl.BlockSpec((tm, D), lambda m: (m, C_G // D)),
            pl.BlockSpec((tm, D), lambda m: (m, C_G // D + 1)),
            pl.BlockSpec((None, W_A, D), lambda m: (layer, 0, 0), **const),
            pl.BlockSpec((None, W_B, D), lambda m: (layer, 0, 0), **const),
            pl.BlockSpec((None, D, D), lambda m: (layer, 0, 0), **const),
        ],
        out_specs=pl.BlockSpec((tm, D), lambda m: (m, 0)),
        out_shape=jax.ShapeDtypeStruct((S, D), F32),
        compiler_params=_cparams(("parallel",)),
        name="merge_out",
    )(x, h_a, h_b, proj, proj, w_a, w_b, w_o)


def _mlp_kernel(x_ref, g_ref, wu_ref, wd_ref, o_ref, hn_ref):
    @pl.when(pl.program_id(1) == 0)
    def _():
        x = x_ref[...]
        hn_ref[...] = _rms(x, g_ref[...]).astype(BF16)
        o_ref[...] = x

    u = jnp.dot(hn_ref[...], wu_ref[...], preferred_element_type=F32)
    a = jnp.square(jnp.maximum(u, 0.0)).astype(BF16)
    o_ref[...] += jnp.dot(a, wd_ref[...], preferred_element_type=F32)


def _mlp(x, gain, w_up, w_down, layer, tm, tf):
    S, D = x.shape
    FF = w_up.shape[2]
    return pl.pallas_call(
        _mlp_kernel,
        grid=(S // tm, FF // tf),
        in_specs=[
            pl.BlockSpec((tm, D), lambda m, f: (m, 0)),
            pl.BlockSpec((1, D), lambda m, f: (0, 0)),
            pl.BlockSpec((None, D, tf), lambda m, f: (layer, 0, f)),
            pl.BlockSpec((None, tf, D), lambda m, f: (layer, f, 0)),
        ],
        out_specs=pl.BlockSpec((tm, D), lambda m, f: (m, 0)),
        out_shape=jax.ShapeDtypeStruct((S, D), F32),
        scratch_shapes=[pltpu.VMEM((tm, D), BF16)],
        compiler_params=_cparams(("parallel", "arbitrary")),
        name="mlp",
    )(x, gain, w_up, w_down)


IN_PROJ_TM, IN_PROJ_TN = 1024, 2048
MERGE_TM = 512
MLP_TM, MLP_TF = 1024, 512


def _tile(n, pref):
    t = min(n, pref)
    while n % t:
        t -= LANE
    assert t > 0
    return t


def kernel(x, norm_mix, w_in, b_if, norm_h_mlstm, norm_q_moba, norm_k_moba,
           w_branch_a, w_branch_b, w_out, norm_mlp, w_up, w_down):
    B, S, D = x.shape
    assert B == 1 and S % BLOCK_B == 0 and D % LANE == 0
    depth = w_in.shape[0]
    c_if = C_QB
    n_main = w_in.shape[2] - N_IF
    assert n_main == C_G + 2 * D

    w_main, w_if = _w_repack(jnp.swapaxes(w_in, 1, 2), c_if, _tile(D, 256))
    w_a, w_b, w_o = w_branch_a.astype(BF16), w_branch_b.astype(BF16), w_out.astype(BF16)
    w_u, w_d = w_up.astype(BF16), w_down.astype(BF16)

    xs = x.reshape(S, D)
    for l in range(depth):
        bias_if = jnp.pad(b_if[l].astype(F32), (0, IF_PAD - N_IF)).reshape(1, IF_PAD)
        proj, ifp = _in_proj(xs, norm_mix[l].reshape(1, D), w_main, w_if, bias_if, l,
                             _tile(S, IN_PROJ_TM), _tile(n_main, IN_PROJ_TN))
        qst, kaug, vt, kmean = _moba_prep(proj, norm_q_moba[l].reshape(1, DH_B),
                                          norm_k_moba[l].reshape(1, DH_B))
        h_b, h_a = _branches(qst, kaug, vt, kmean, proj, ifp, norm_h_mlstm[l].reshape(1, W_A))
        xs = _merge_out(xs, h_a, h_b, proj, w_a, w_b, w_o, l, _tile(S, MERGE_TM))
        xs = _mlp(xs, norm_mlp[l].reshape(1, D), w_u, w_d, l, _tile(S, MLP_TM), _tile(w_up.shape[2], MLP_TF))
    return xs.reshape(B, S, D)
```

```python
import functools

import jax
import jax.numpy as jnp
import numpy as np
from jax import lax
from jax.experimental import pallas as pl
from jax.experimental.pallas import tpu as pltpu

F32 = jnp.float32
BF16 = jnp.bfloat16

NH_A, DQK_A, DV_A, CHUNK_A = 4, 128, 256, 128
GATE_SOFTCAP = 15.0
NH_B, DH_B, BLOCK_B, TOPK_B = 8, 128, 256, 3
EPS = 1e-6

W_A = NH_A * DV_A
W_B = NH_B * DH_B
QK_A = NH_A * DQK_A
N_IF = 2 * NH_A
LANE = 128
IF_PAD = LANE
NEG_BIG = -1e30
VMEM_LIMIT = 56 * 1024 * 1024

C_QA, C_KA, C_VA, C_OA = 0, QK_A, 2 * QK_A, 2 * QK_A + W_A
C_QB = C_OA + W_A
C_KB = C_QB + W_B
C_VB = C_KB + W_B
C_G = C_VB + W_B


def _cparams(sem, vmem_limit=VMEM_LIMIT):
    return pltpu.CompilerParams(dimension_semantics=sem, vmem_limit_bytes=vmem_limit)


def _rms(x, gain):
    ms = jnp.mean(x * x, axis=-1, keepdims=True)
    return x * lax.rsqrt(ms + EPS) * gain


def _w_repack_kernel(wt_ref, main_ref, if_ref, *, c_if):
    n_cols = wt_ref.shape[0]
    main_ref[:c_if, :] = wt_ref[:c_if, :].astype(main_ref.dtype)
    main_ref[c_if:, :] = wt_ref[c_if + N_IF:n_cols, :].astype(main_ref.dtype)
    row = lax.broadcasted_iota(jnp.int32, if_ref.shape, 0)
    if_ref[...] = jnp.where(row < N_IF, wt_ref[c_if:c_if + IF_PAD, :], 0.0).astype(if_ref.dtype)


def _w_repack(w_in_t, c_if, td):
    depth, n_cols, D = w_in_t.shape
    n_main = n_cols - N_IF
    return pl.pallas_call(
        functools.partial(_w_repack_kernel, c_if=c_if),
        grid=(depth, D // td),
        in_specs=[pl.BlockSpec((None, n_cols, td), lambda l, d: (l, 0, d))],
        out_specs=[
            pl.BlockSpec((None, n_main, td), lambda l, d: (l, 0, d)),
            pl.BlockSpec((None, IF_PAD, td), lambda l, d: (l, 0, d)),
        ],
        out_shape=[
            jax.ShapeDtypeStruct((depth, n_main, D), BF16),
            jax.ShapeDtypeStruct((depth, IF_PAD, D), BF16),
        ],
        compiler_params=_cparams(("parallel", "parallel")),
        name="w_repack",
    )(w_in_t)


_NT = (((1,), (1,)), ((), ()))


def _in_proj_kernel(x_ref, g_ref, w_ref, wif_ref, bif_ref, o_ref, oif_ref, xn_ref):
    @pl.when(pl.program_id(1) == 0)
    def _():
        xn = _rms(x_ref[...], g_ref[...]).astype(BF16)
        xn_ref[...] = xn
        oif_ref[...] = lax.dot_general(xn, wif_ref[...], _NT, preferred_element_type=F32) + bif_ref[...]

    o_ref[...] = lax.dot_general(xn_ref[...], w_ref[...], _NT,
                                 preferred_element_type=F32).astype(o_ref.dtype)


def _in_proj(x, gain, w_main_t, w_if_t, b_if, layer, tm, tn):
    S, D = x.shape
    N = w_main_t.shape[1]
    return pl.pallas_call(
        _in_proj_kernel,
        grid=(S // tm, N // tn),
        in_specs=[
            pl.BlockSpec((tm, D), lambda m, n: (m, 0)),
            pl.BlockSpec((1, D), lambda m, n: (0, 0)),
            pl.BlockSpec((None, tn, D), lambda m, n: (layer, n, 0)),
            pl.BlockSpec((None, IF_PAD, D), lambda m, n: (layer, 0, 0)),
            pl.BlockSpec((1, IF_PAD), lambda m, n: (0, 0)),
        ],
        out_specs=[
            pl.BlockSpec((tm, tn), lambda m, n: (m, n)),
            pl.BlockSpec((tm, IF_PAD), lambda m, n: (m, 0)),
        ],
        out_shape=[
            jax.ShapeDtypeStruct((S, N), BF16),
            jax.ShapeDtypeStruct((S, IF_PAD), F32),
        ],
        scratch_shapes=[pltpu.VMEM((tm, D), BF16)],
        compiler_params=_cparams(("parallel", "arbitrary")),
        name="in_proj",
    )(x, gain, w_main_t, w_if_t, b_if)


def _mlstm_step(first, q_ref, k_ref, v_ref, oa_ref, if_ref, gain_ref, o_ref, c_ref, m_ref):
    L = CHUNK_A

    @pl.when(first)
    def _():
        c_ref[...] = jnp.zeros_like(c_ref)
        m_ref[...] = jnp.zeros_like(m_ref)

    g = if_ref[...]
    gc = GATE_SOFTCAP * jnp.tanh(g / GATE_SOFTCAP)
    col = lax.broadcasted_iota(jnp.int32, (L, LANE), 1)
    row = lax.broadcasted_iota(jnp.int32, (L, LANE), 0)
    log_f = jnp.minimum(gc, 0.0) - jnp.log1p(jnp.exp(-jnp.abs(gc)))
    G = jnp.where(col < NH_A, gc, log_f)
    tril = (row >= col).astype(F32)
    Bc = jnp.dot(tril, G, preferred_element_type=F32, precision=lax.Precision.HIGHEST)
    RT = (G - pltpu.roll(Bc, LANE - NH_A, 1)).T
    causal = row >= col
    scale = DQK_A ** -0.5
    ones_col = jnp.where(col == 0, 1.0, 0.0).astype(BF16)

    def head(h):
        qh = q_ref[:, h * DQK_A:(h + 1) * DQK_A]
        kh = k_ref[:, h * DQK_A:(h + 1) * DQK_A]
        vh = v_ref[:, h * DV_A:(h + 1) * DV_A]
        vext = jnp.concatenate([vh, ones_col], axis=1)
        b_col = Bc[:, NH_A + h:NH_A + h + 1]
        b_last = Bc[L - 1:L, NH_A + h:NH_A + h + 1]
        ig_col = G[:, h:h + 1]
        m_prev = m_ref[h:h + 1, 0:1]
        c_prev = c_ref[h]

        log_d = jnp.where(causal, b_col + RT[h:h + 1, :], -jnp.inf)
        m_inter = b_col + m_prev
        m_row = jnp.maximum(m_inter, jnp.max(log_d, axis=1, keepdims=True))
        dmat = jnp.exp(log_d - m_row)
        s = lax.dot_general(qh, kh, (((1,), (1,)), ((), ())), preferred_element_type=F32)
        sd = (s * (dmat * scale)).astype(BF16)
        inter = jnp.exp(m_inter - m_row) * scale
        num_ext = (jnp.dot(sd, vext, preferred_element_type=F32)
                   + inter * jnp.dot(qh, c_prev.astype(BF16), preferred_element_type=F32))
        num = num_ext[:, :DV_A]
        den = num_ext[:, DV_A:DV_A + 1]
        hval = num / jnp.maximum(jnp.abs(den), jnp.exp(-m_row))

        hn = _rms(hval, gain_ref[:, h * DV_A:(h + 1) * DV_A])
        og = jax.nn.sigmoid(oa_ref[:, h * DV_A:(h + 1) * DV_A].astype(F32))
        o_ref[:, h * DV_A:(h + 1) * DV_A] = (hn * og).astype(o_ref.dtype)

        log_w = b_last - b_col + ig_col
        m_new = jnp.maximum(b_last + m_prev, jnp.max(log_w, axis=0, keepdims=True))
        w = jnp.exp(log_w - m_new)
        decay = jnp.exp(b_last + m_prev - m_new)
        wv = (w * vext.astype(F32)).astype(BF16)
        c_ref[h] = decay * c_prev + lax.dot_general(
            kh, wv, (((0,), (0,)), ((), ())), preferred_element_type=F32)
        m_ref[h:h + 1, :] = jnp.broadcast_to(m_new, (1, LANE))

    return [functools.partial(head, h) for h in range(NH_A)]


LOG2E = 1.4426950408889634
N_SPLIT = 3
V_ROWS = DH_B + 16


def _alibi_slope(h):
    return 2.0 ** (-8.0 * (h + 1) / NH_B)


def _bf16_split_const(x):
    terms, rem = [], float(np.float32(x))
    for _ in range(N_SPLIT):
        t = float(np.float32(rem).astype(BF16))
        terms.append(t)
        rem -= t
    return terms


def _moba_prep_kernel(q_ref, k_ref, v_ref, gq_ref, gk_ref, qst_ref, kaug_ref, vt_ref, kmean_ref, *, n_blk):
    i = pl.program_id(0)
    T = BLOCK_B
    scale = DH_B ** -0.5 * LOG2E
    lane = lax.broadcasted_iota(jnp.int32, (T, LANE), 1)
    pos = lax.broadcasted_iota(jnp.int32, (T, LANE), 0).astype(F32)
    blk_f = i.astype(F32)
    row_is_blk = lax.broadcasted_iota(jnp.int32, (LANE, DH_B), 0) == i
    ones_rows = jnp.where(lax.broadcasted_iota(jnp.int32, (V_ROWS - DH_B, T), 0) == 0, 1.0, 0.0)

    @pl.when(i == 0)
    def _():
        kmean_ref[...] = jnp.zeros_like(kmean_ref)

    for h in range(NH_B):
        sl = slice(h * DH_B, (h + 1) * DH_B)
        qn = _rms(q_ref[:, sl].astype(F32), gq_ref[...])
        qst_ref[h] = (qn * scale).T.astype(qst_ref.dtype)
        kn = _rms(k_ref[:, sl].astype(F32), gk_ref[...])
        kmean_ref[:, sl] = jnp.where(row_is_blk, jnp.mean(kn, axis=0, keepdims=True), kmean_ref[:, sl])
        slope = _alibi_slope(h)
        extra = jnp.where(lane < n_blk + 3 * N_SPLIT, 1.0, 0.0)
        extra = jnp.where(lane < n_blk + 2 * N_SPLIT, slope * pos, extra)
        extra = jnp.where(lane < n_blk + N_SPLIT, slope * T * blk_f, extra)
        extra = jnp.where(lane < n_blk, jnp.where(lane == i, 1.0, 0.0), extra)
        kaug_ref[h, :, :DH_B] = kn.astype(kaug_ref.dtype)
        kaug_ref[h, :, DH_B:] = extra.astype(kaug_ref.dtype)
        vt = jnp.concatenate([v_ref[:, sl].astype(F32).T, ones_rows], axis=0)
        vt_ref[h] = vt.astype(vt_ref.dtype)


def _moba_prep(proj, gq, gk):
    S = proj.shape[0]
    T = BLOCK_B
    n_blk = S // T
    assert n_blk + 3 * N_SPLIT <= LANE
    kern = functools.partial(_moba_prep_kernel, n_blk=n_blk)
    return pl.pallas_call(
        kern,
        grid=(n_blk,),
        in_specs=[
            pl.BlockSpec((T, W_B), lambda i: (i, C_QB // W_B)),
            pl.BlockSpec((T, W_B), lambda i: (i, C_KB // W_B)),
            pl.BlockSpec((T, W_B), lambda i: (i, C_VB // W_B)),
            pl.BlockSpec((1, DH_B), lambda i: (0, 0)),
            pl.BlockSpec((1, DH_B), lambda i: (0, 0)),
        ],
        out_specs=[
            pl.BlockSpec((NH_B, DH_B, T), lambda i: (0, 0, i)),
            pl.BlockSpec((NH_B, T, DH_B + LANE), lambda i: (0, i, 0)),
            pl.BlockSpec((NH_B, None, V_ROWS, T), lambda i: (0, i, 0, 0)),
            pl.BlockSpec((LANE, W_B), lambda i: (0, 0)),
        ],
        out_shape=[
            jax.ShapeDtypeStruct((NH_B, DH_B, S), BF16),
            jax.ShapeDtypeStruct((NH_B, S, DH_B + LANE), BF16),
            jax.ShapeDtypeStruct((NH_B, n_blk, V_ROWS, T), BF16),
            jax.ShapeDtypeStruct((LANE, W_B), F32),
        ],
        compiler_params=_cparams(("arbitrary",)),
        name="moba_prep",
    )(proj, proj, proj, gq, gk)


CHUNK_B = 4


def _branches_kernel(qst_ref, kaug_ref, vt_ref, kmean_ref,
                     qa_ref, ka_ref, va_ref, oa_ref, if_ref, gain_ref,
                     o_ref, ha_ref,
                     qaug_ref, s_ref, cmax_ref, m_ref, acc_ref, c_ref, ma_ref, *, n_blk):
    h = pl.program_id(0)
    sb = pl.program_id(1)
    mlstm_heads = _mlstm_step((h == 0) & (sb == 0), qa_ref, ka_ref, va_ref, oa_ref, if_ref, gain_ref,
                              ha_ref, c_ref, ma_ref)
    T = BLOCK_B
    W = CHUNK_B * T
    qst = qst_ref[...]

    gate = jnp.dot(kmean_ref[:n_blk, :].astype(BF16), qst, preferred_element_type=F32)
    blk = lax.broadcasted_iota(jnp.int32, (n_blk, W), 0)
    assert T & (T - 1) == 0
    tok_blk = jnp.right_shift(lax.broadcasted_iota(jnp.int32, (1, W), 1), T.bit_length() - 1)
    own = sb * CHUNK_B + tok_blk
    valid = blk < own
    g = jnp.where(valid, gate, -jnp.inf)
    sel = jnp.zeros((n_blk, W), F32)
    for _ in range(TOPK_B):
        mx = jnp.max(g, axis=0, keepdims=True)
        idx = jnp.min(jnp.where(g == mx, blk, n_blk), axis=0, keepdims=True)
        pick = blk == idx
        sel = jnp.where(pick, 1.0, sel)
        g = jnp.where(pick, -jnp.inf, g)
    keep = jnp.where(valid, sel, 0.0) + jnp.where(blk == own, 1.0, 0.0)
    selbias = jnp.where(keep > 0.0, 0.0, NEG_BIG)

    slope = jnp.exp2(jnp.full((1, W), -8.0 / NH_B, F32) * (h + 1).astype(F32))
    v3 = -slope * (T * own).astype(F32) * LOG2E
    v3_hi = v3.astype(BF16).astype(F32)
    v3_r = v3 - v3_hi
    v3_lo = v3_r.astype(BF16).astype(F32)
    v3_terms = (v3_hi, v3_lo, v3_r - v3_lo)
    c_terms = _bf16_split_const(LOG2E)
    n_tail = LANE - n_blk
    r = lax.broadcasted_iota(jnp.int32, (n_tail, W), 0)
    tail = jnp.zeros((n_tail, W), F32)
    for t in range(N_SPLIT):
        tail = jnp.where(r == t, c_terms[t], tail)
        tail = jnp.where(r == N_SPLIT + t, c_terms[t], tail)
        tail = jnp.where(r == 2 * N_SPLIT + t, v3_terms[t], tail)
    qaug_ref[:DH_B, :] = qst
    qaug_ref[DH_B:DH_B + n_blk, :] = selbias.astype(BF16)
    qaug_ref[DH_B + n_blk:, :] = tail.astype(BF16)

    def block_scores(cid, b):
        start = pl.multiple_of(cid * W + b * T, T)
        return jnp.dot(kaug_ref[pl.ds(start, T), :], qaug_ref[...],
                       preferred_element_type=F32)

    key = lax.broadcasted_iota(jnp.int32, (T, W), 0)
    tok = lax.broadcasted_iota(jnp.int32, (T, W), 1)
    cmax = None
    for b in range(CHUNK_B):
        sblk = jnp.where(key + b * T > tok, NEG_BIG, block_scores(sb, b))
        s_ref[b * T:(b + 1) * T, :] = sblk
        mb = jnp.max(sblk, axis=0, keepdims=True)
        cmax = mb if cmax is None else jnp.maximum(cmax, mb)
        if mlstm_heads:
            mlstm_heads.pop(0)()
    while mlstm_heads:
        mlstm_heads.pop(0)()
    cmax_ref[...] = cmax
    m_ref[...] = jnp.full(m_ref.shape, NEG_BIG, F32)
    acc_ref[...] = jnp.zeros_like(acc_ref)

    def step(cid, next_cid):
        m_prev = m_ref[...]
        m_new = jnp.maximum(m_prev, cmax_ref[...])
        alpha = jnp.exp2(m_prev - m_new)
        pv = jnp.zeros((V_ROWS, W), F32)
        cmax = None
        for b in range(CHUNK_B):
            rows = slice(b * T, (b + 1) * T)
            p = jnp.exp2(s_ref[rows, :] - m_new)
            pv = pv + jnp.dot(vt_ref[cid * CHUNK_B + b], p.astype(BF16), preferred_element_type=F32)
            if next_cid is not None:
                nb = block_scores(next_cid, b)
                s_ref[rows, :] = nb
                mb = jnp.max(nb, axis=0, keepdims=True)
                cmax = mb if cmax is None else jnp.maximum(cmax, mb)
        acc_ref[...] = alpha * acc_ref[...] + pv
        m_ref[...] = m_new
        if cmax is not None:
            cmax_ref[...] = cmax

    def body(n, carry):
        step(jnp.where(n == 0, sb, n - 1), n)
        return carry

    lax.fori_loop(0, sb, body, 0)
    step(jnp.where(sb == 0, sb, sb - 1), None)
    acc = acc_ref[...]
    o_ref[...] = (acc[:DH_B, :] / acc[DH_B:DH_B + 1, :]).T.astype(o_ref.dtype)


def _branches(qst, kaug, vt, kmean, proj, ifp, gain_h):
    S = qst.shape[2]
    T = BLOCK_B
    W = CHUNK_B * T
    L = CHUNK_A
    n_blk = S // T
    n_sb = n_blk // CHUNK_B
    assert n_blk % CHUNK_B == 0 and NH_B * n_sb * L == S
    kern = functools.partial(_branches_kernel, n_blk=n_blk)

    def chunk(col):
        return lambda h, s: (h * n_sb + s, col)

    return pl.pallas_call(
        kern,
        grid=(NH_B, n_sb),
        in_specs=[
            pl.BlockSpec((None, DH_B, W), lambda h, s: (h, 0, s)),
            pl.BlockSpec((None, S, DH_B + LANE), lambda h, s: (h, 0, 0)),
            pl.BlockSpec((None, n_blk, V_ROWS, T), lambda h, s: (h, 0, 0, 0)),
            pl.BlockSpec((LANE, DH_B), lambda h, s: (0, h)),
            pl.BlockSpec((L, QK_A), chunk(C_QA // QK_A)),
            pl.BlockSpec((L, QK_A), chunk(C_KA // QK_A)),
            pl.BlockSpec((L, W_A), chunk(C_VA // W_A)),
            pl.BlockSpec((L, W_A), chunk(C_OA // W_A)),
            pl.BlockSpec((L, IF_PAD), chunk(0)),
            pl.BlockSpec((1, W_A), lambda h, s: (0, 0)),
        ],
        out_specs=[
            pl.BlockSpec((W, DH_B), lambda h, s: (s, h)),
            pl.BlockSpec((L, W_A), chunk(0)),
        ],
        out_shape=[
            jax.ShapeDtypeStruct((S, W_B), BF16),
            jax.ShapeDtypeStruct((S, W_A), BF16),
        ],
        scratch_shapes=[
            pltpu.VMEM((DH_B + LANE, W), BF16),
            pltpu.VMEM((W, W), F32),
            pltpu.VMEM((1, W), F32),
            pltpu.VMEM((1, W), F32),
            pltpu.VMEM((V_ROWS, W), F32),
            pltpu.VMEM((NH_A, DQK_A, DV_A + LANE), F32),
            pltpu.VMEM((8, LANE), F32),
        ],
        compiler_params=_cparams(("arbitrary", "arbitrary")),
        name="branches",
    )(qst, kaug, vt, kmean, proj, proj, proj, proj, ifp, gain_h)


def _merge_out_kernel(x_ref, ha_ref, hb_ref, ga_ref, gb_ref, wa_ref, wb_ref, wo_ref, o_ref):
    ta = jnp.dot(ha_ref[...], wa_ref[...], preferred_element_type=F32)
    tb = jnp.dot(hb_ref[...], wb_ref[...], preferred_element_type=F32)
    merged = (jax.nn.sigmoid(ga_ref[...].astype(F32)) * ta
              + jax.nn.sigmoid(gb_ref[...].astype(F32)) * tb).astype(BF16)
    o_ref[...] = x_ref[...] + jnp.dot(merged, wo_ref[...], preferred_element_type=F32)


def _merge_out(x, h_a, h_b, proj, w_a, w_b, w_o, layer, tm):
    S, D = x.shape
    const = dict(pipeline_mode=pl.Buffered(1))
    return pl.pallas_call(
        _merge_out_kernel,
        grid=(S // tm,),
        in_specs=[
            pl.BlockSpec((tm, D), lambda m: (m, 0)),
            pl.BlockSpec((tm, W_A), lambda m: (m, 0)),
            pl.BlockSpec((tm, W_B), lambda m: (m, 0)),
            pl.BlockSpec((tm, D), lambda m: (m, C_G // D)),
            pl.BlockSpec((tm, D), lambda m: (m, C_G // D + 1)),
            pl.BlockSpec((None, W_A, D), lambda m: (layer, 0, 0), **const),
            pl.BlockSpec((None, W_B, D), lambda m: (layer, 0, 0), **const),
            pl.BlockSpec((None, D, D), lambda m: (layer, 0, 0), **const),
        ],
        out_specs=pl.BlockSpec((tm, D), lambda m: (m, 0)),
        out_shape=jax.ShapeDtypeStruct((S, D), F32),
        compiler_params=_cparams(("parallel",)),
        name="merge_out",
    )(x, h_a, h_b, proj, proj, w_a, w_b, w_o)


def _mlp_kernel(x_ref, g_ref, wu_ref, wd_ref, o_ref, hn_ref):
    @pl.when(pl.program_id(1) == 0)
    def _():
        x = x_ref[...]
        hn_ref[...] = _rms(x, g_ref[...]).astype(BF16)
        o_ref[...] = x

    u = jnp.dot(hn_ref[...], wu_ref[...], preferred_element_type=F32)
    a = jnp.square(jnp.maximum(u, 0.0)).astype(BF16)
    o_ref[...] += jnp.dot(a, wd_ref[...], preferred_element_type=F32)


def _mlp(x, gain, w_up, w_down, layer, tm, tf):
    S, D = x.shape
    FF = w_up.shape[2]
    return pl.pallas_call(
        _mlp_kernel,
        grid=(S // tm, FF // tf),
        in_specs=[
            pl.BlockSpec((tm, D), lambda m, f: (m, 0)),
            pl.BlockSpec((1, D), lambda m, f: (0, 0)),
            pl.BlockSpec((None, D, tf), lambda m, f: (layer, 0, f)),
            pl.BlockSpec((None, tf, D), lambda m, f: (layer, f, 0)),
        ],
        out_specs=pl.BlockSpec((tm, D), lambda m, f: (m, 0)),
        out_shape=jax.ShapeDtypeStruct((S, D), F32),
        scratch_shapes=[pltpu.VMEM((tm, D), BF16)],
        compiler_params=_cparams(("parallel", "arbitrary"), MLP_VMEM_LIMIT),
        name="mlp",
    )(x, gain, w_up, w_down)


IN_PROJ_TM, IN_PROJ_TN = 1024, 2048
MERGE_TM = 512
MLP_TM, MLP_TF = 512, 2048
MLP_VMEM_LIMIT = 60 * 1024 * 1024


def _tile(n, pref):
    t = min(n, pref)
    while n % t:
        t -= LANE
    assert t > 0
    return t


def kernel(x, norm_mix, w_in, b_if, norm_h_mlstm, norm_q_moba, norm_k_moba,
           w_branch_a, w_branch_b, w_out, norm_mlp, w_up, w_down):
    B, S, D = x.shape
    assert B == 1 and S % BLOCK_B == 0 and D % LANE == 0
    depth = w_in.shape[0]
    c_if = C_QB
    n_main = w_in.shape[2] - N_IF
    assert n_main == C_G + 2 * D

    w_main, w_if = _w_repack(jnp.swapaxes(w_in, 1, 2), c_if, _tile(D, 256))
    w_a, w_b, w_o = w_branch_a.astype(BF16), w_branch_b.astype(BF16), w_out.astype(BF16)
    w_u, w_d = w_up.astype(BF16), w_down.astype(BF16)

    xs = x.reshape(S, D)
    for l in range(depth):
        bias_if = jnp.pad(b_if[l].astype(F32), (0, IF_PAD - N_IF)).reshape(1, IF_PAD)
        proj, ifp = _in_proj(xs, norm_mix[l].reshape(1, D), w_main, w_if, bias_if, l,
                             _tile(S, IN_PROJ_TM), _tile(n_main, IN_PROJ_TN))
        qst, kaug, vt, kmean = _moba_prep(proj, norm_q_moba[l].reshape(1, DH_B),
                                          norm_k_moba[l].reshape(1, DH_B))
        h_b, h_a = _branches(qst, kaug, vt, kmean, proj, ifp, norm_h_mlstm[l].reshape(1, W_A))
        xs = _merge_out(xs, h_a, h_b, proj, w_a, w_b, w_o, l, _tile(S, MERGE_TM))
        xs = _mlp(xs, norm_mlp[l].reshape(1, D), w_u, w_d, l, _tile(S, MLP_TM), _tile(w_up.shape[2], MLP_TF))
    return xs.reshape(B, S, D)
```

```python
import functools

import jax
import jax.numpy as jnp
import numpy as np
from jax import lax
from jax.experimental import pallas as pl
from jax.experimental.pallas import tpu as pltpu

F32 = jnp.float32
BF16 = jnp.bfloat16

NH_A, DQK_A, DV_A, CHUNK_A = 4, 128, 256, 128
GATE_SOFTCAP = 15.0
NH_B, DH_B, BLOCK_B, TOPK_B = 8, 128, 256, 3
EPS = 1e-6

W_A = NH_A * DV_A
W_B = NH_B * DH_B
QK_A = NH_A * DQK_A
N_IF = 2 * NH_A
LANE = 128
IF_PAD = LANE
NEG_BIG = -1e30
VMEM_LIMIT = 56 * 1024 * 1024

C_QA, C_KA, C_VA, C_OA = 0, QK_A, 2 * QK_A, 2 * QK_A + W_A
C_QB = C_OA + W_A
C_KB = C_QB + W_B
C_VB = C_KB + W_B
C_G = C_VB + W_B


def _cparams(sem, vmem_limit=VMEM_LIMIT):
    return pltpu.CompilerParams(dimension_semantics=sem, vmem_limit_bytes=vmem_limit)


def _rms(x, gain):
    ms = jnp.mean(x * x, axis=-1, keepdims=True)
    return x * lax.rsqrt(ms + EPS) * gain


def _w_repack_kernel(wt_ref, main_ref, if_ref, *, c_if):
    n_cols = wt_ref.shape[0]
    main_ref[:c_if, :] = wt_ref[:c_if, :].astype(main_ref.dtype)
    main_ref[c_if:, :] = wt_ref[c_if + N_IF:n_cols, :].astype(main_ref.dtype)
    row = lax.broadcasted_iota(jnp.int32, if_ref.shape, 0)
    if_ref[...] = jnp.where(row < N_IF, wt_ref[c_if:c_if + IF_PAD, :], 0.0).astype(if_ref.dtype)


def _w_repack(w_in_t, c_if, td):
    depth, n_cols, D = w_in_t.shape
    n_main = n_cols - N_IF
    return pl.pallas_call(
        functools.partial(_w_repack_kernel, c_if=c_if),
        grid=(depth, D // td),
        in_specs=[pl.BlockSpec((None, n_cols, td), lambda l, d: (l, 0, d))],
        out_specs=[
            pl.BlockSpec((None, n_main, td), lambda l, d: (l, 0, d)),
            pl.BlockSpec((None, IF_PAD, td), lambda l, d: (l, 0, d)),
        ],
        out_shape=[
            jax.ShapeDtypeStruct((depth, n_main, D), BF16),
            jax.ShapeDtypeStruct((depth, IF_PAD, D), BF16),
        ],
        compiler_params=_cparams(("parallel", "parallel")),
        name="w_repack",
    )(w_in_t)


_NT = (((1,), (1,)), ((), ()))


def _in_proj_kernel(x_ref, g_ref, w_ref, wif_ref, bif_ref, o_ref, oif_ref, xn_ref):
    @pl.when(pl.program_id(1) == 0)
    def _():
        xn = _rms(x_ref[...], g_ref[...]).astype(BF16)
        xn_ref[...] = xn
        oif_ref[...] = lax.dot_general(xn, wif_ref[...], _NT, preferred_element_type=F32) + bif_ref[...]

    o_ref[...] = lax.dot_general(xn_ref[...], w_ref[...], _NT,
                                 preferred_element_type=F32).astype(o_ref.dtype)


def _in_proj(x, gain, w_main_t, w_if_t, b_if, layer, tm, tn):
    S, D = x.shape
    N = w_main_t.shape[1]
    return pl.pallas_call(
        _in_proj_kernel,
        grid=(S // tm, N // tn),
        in_specs=[
            pl.BlockSpec((tm, D), lambda m, n: (m, 0)),
            pl.BlockSpec((1, D), lambda m, n: (0, 0)),
            pl.BlockSpec((None, tn, D), lambda m, n: (layer, n, 0)),
            pl.BlockSpec((None, IF_PAD, D), lambda m, n: (layer, 0, 0)),
            pl.BlockSpec((1, IF_PAD), lambda m, n: (0, 0)),
        ],
        out_specs=[
            pl.BlockSpec((tm, tn), lambda m, n: (m, n)),
            pl.BlockSpec((tm, IF_PAD), lambda m, n: (m, 0)),
        ],
        out_shape=[
            jax.ShapeDtypeStruct((S, N), BF16),
            jax.ShapeDtypeStruct((S, IF_PAD), F32),
        ],
        scratch_shapes=[pltpu.VMEM((tm, D), BF16)],
        compiler_params=_cparams(("parallel", "arbitrary")),
        name="in_proj",
    )(x, gain, w_main_t, w_if_t, b_if)


def _mlstm_step(first, q_ref, k_ref, v_ref, oa_ref, if_ref, gain_ref, o_ref, c_ref, m_ref):
    L = CHUNK_A

    @pl.when(first)
    def _():
        c_ref[...] = jnp.zeros_like(c_ref)
        m_ref[...] = jnp.zeros_like(m_ref)

    g = if_ref[...]
    gc = GATE_SOFTCAP * jnp.tanh(g / GATE_SOFTCAP)
    col = lax.broadcasted_iota(jnp.int32, (L, LANE), 1)
    row = lax.broadcasted_iota(jnp.int32, (L, LANE), 0)
    log_f = jnp.minimum(gc, 0.0) - jnp.log1p(jnp.exp(-jnp.abs(gc)))
    G = jnp.where(col < NH_A, gc, log_f)
    tril = (row >= col).astype(F32)
    Bc = jnp.dot(tril, G, preferred_element_type=F32, precision=lax.Precision.HIGHEST)
    RT = (G - pltpu.roll(Bc, LANE - NH_A, 1)).T
    causal = row >= col
    scale = DQK_A ** -0.5
    ones_col = jnp.where(col == 0, 1.0, 0.0).astype(BF16)

    def head(h):
        qh = q_ref[:, h * DQK_A:(h + 1) * DQK_A]
        kh = k_ref[:, h * DQK_A:(h + 1) * DQK_A]
        vh = v_ref[:, h * DV_A:(h + 1) * DV_A]
        vext = jnp.concatenate([vh, ones_col], axis=1)
        b_col = Bc[:, NH_A + h:NH_A + h + 1]
        b_last = Bc[L - 1:L, NH_A + h:NH_A + h + 1]
        ig_col = G[:, h:h + 1]
        m_prev = m_ref[h:h + 1, 0:1]
        c_prev = c_ref[h]

        log_d = jnp.where(causal, b_col + RT[h:h + 1, :], -jnp.inf)
        m_inter = b_col + m_prev
        m_row = jnp.maximum(m_inter, jnp.max(log_d, axis=1, keepdims=True))
        dmat = jnp.exp(log_d - m_row)
        s = lax.dot_general(qh, kh, (((1,), (1,)), ((), ())), preferred_element_type=F32)
        sd = (s * (dmat * scale)).astype(BF16)
        inter = jnp.exp(m_inter - m_row) * scale
        num_ext = (jnp.dot(sd, vext, preferred_element_type=F32)
                   + inter * jnp.dot(qh, c_prev.astype(BF16), preferred_element_type=F32))
        num = num_ext[:, :DV_A]
        den = num_ext[:, DV_A:DV_A + 1]
        hval = num / jnp.maximum(jnp.abs(den), jnp.exp(-m_row))

        hn = _rms(hval, gain_ref[:, h * DV_A:(h + 1) * DV_A])
        og = jax.nn.sigmoid(oa_ref[:, h * DV_A:(h + 1) * DV_A].astype(F32))
        o_ref[:, h * DV_A:(h + 1) * DV_A] = (hn * og).astype(o_ref.dtype)

        log_w = b_last - b_col + ig_col
        m_new = jnp.maximum(b_last + m_prev, jnp.max(log_w, axis=0, keepdims=True))
        w = jnp.exp(log_w - m_new)
        decay = jnp.exp(b_last + m_prev - m_new)
        wv = (w * vext.astype(F32)).astype(BF16)
        c_ref[h] = decay * c_prev + lax.dot_general(
            kh, wv, (((0,), (0,)), ((), ())), preferred_element_type=F32)
        m_ref[h:h + 1, :] = jnp.broadcast_to(m_new, (1, LANE))

    return [functools.partial(head, h) for h in range(NH_A)]


LOG2E = 1.4426950408889634
N_SPLIT = 3
V_ROWS = DH_B + 16


def _alibi_slope(h):
    return 2.0 ** (-8.0 * (h + 1) / NH_B)


def _bf16_split_const(x):
    terms, rem = [], float(np.float32(x))
    for _ in range(N_SPLIT):
        t = float(np.float32(rem).astype(BF16))
        terms.append(t)
        rem -= t
    return terms


def _moba_prep_kernel(q_ref, k_ref, v_ref, gq_ref, gk_ref, qst_ref, kaug_ref, vt_ref, kmean_ref, *, n_blk):
    i = pl.program_id(0)
    T = BLOCK_B
    scale = DH_B ** -0.5 * LOG2E
    lane = lax.broadcasted_iota(jnp.int32, (T, LANE), 1)
    pos = lax.broadcasted_iota(jnp.int32, (T, LANE), 0).astype(F32)
    blk_f = i.astype(F32)
    row_is_blk = lax.broadcasted_iota(jnp.int32, (LANE, DH_B), 0) == i
    ones_rows = jnp.where(lax.broadcasted_iota(jnp.int32, (V_ROWS - DH_B, T), 0) == 0, 1.0, 0.0)

    @pl.when(i == 0)
    def _():
        kmean_ref[...] = jnp.zeros_like(kmean_ref)

    for h in range(NH_B):
        sl = slice(h * DH_B, (h + 1) * DH_B)
        qn = _rms(q_ref[:, sl].astype(F32), gq_ref[...])
        qst_ref[h] = (qn * scale).T.astype(qst_ref.dtype)
        kn = _rms(k_ref[:, sl].astype(F32), gk_ref[...])
        kmean_ref[:, sl] = jnp.where(row_is_blk, jnp.mean(kn, axis=0, keepdims=True), kmean_ref[:, sl])
        slope = _alibi_slope(h)
        extra = jnp.where(lane < n_blk + 3 * N_SPLIT, 1.0, 0.0)
        extra = jnp.where(lane < n_blk + 2 * N_SPLIT, slope * pos, extra)
        extra = jnp.where(lane < n_blk + N_SPLIT, slope * T * blk_f, extra)
        extra = jnp.where(lane < n_blk, jnp.where(lane == i, 1.0, 0.0), extra)
        kaug_ref[h, :, :DH_B] = kn.astype(kaug_ref.dtype)
        kaug_ref[h, :, DH_B:] = extra.astype(kaug_ref.dtype)
        vt = jnp.concatenate([v_ref[:, sl].astype(F32).T, ones_rows], axis=0)
        vt_ref[h] = vt.astype(vt_ref.dtype)


def _moba_prep(proj, gq, gk):
    S = proj.shape[0]
    T = BLOCK_B
    n_blk = S // T
    assert n_blk + 3 * N_SPLIT <= LANE
    kern = functools.partial(_moba_prep_kernel, n_blk=n_blk)
    return pl.pallas_call(
        kern,
        grid=(n_blk,),
        in_specs=[
            pl.BlockSpec((T, W_B), lambda i: (i, C_QB // W_B)),
            pl.BlockSpec((T, W_B), lambda i: (i, C_KB // W_B)),
            pl.BlockSpec((T, W_B), lambda i: (i, C_VB // W_B)),
            pl.BlockSpec((1, DH_B), lambda i: (0, 0)),
            pl.BlockSpec((1, DH_B), lambda i: (0, 0)),
        ],
        out_specs=[
            pl.BlockSpec((NH_B, DH_B, T), lambda i: (0, 0, i)),
            pl.BlockSpec((NH_B, T, DH_B + LANE), lambda i: (0, i, 0)),
            pl.BlockSpec((NH_B, None, V_ROWS, T), lambda i: (0, i, 0, 0)),
            pl.BlockSpec((LANE, W_B), lambda i: (0, 0)),
        ],
        out_shape=[
            jax.ShapeDtypeStruct((NH_B, DH_B, S), BF16),
            jax.ShapeDtypeStruct((NH_B, S, DH_B + LANE), BF16),
            jax.ShapeDtypeStruct((NH_B, n_blk, V_ROWS, T), BF16),
            jax.ShapeDtypeStruct((LANE, W_B), F32),
        ],
        compiler_params=_cparams(("arbitrary",)),
        name="moba_prep",
    )(proj, proj, proj, gq, gk)


CHUNK_B = 4


def _branches_kernel(qst0_ref, qstn_ref, kaug_ref, vt_ref, kmean_ref,
                     qa_ref, ka_ref, va_ref, oa_ref, if_ref, gain_ref,
                     o_ref, ha_ref,
                     qaug_ref, s_ref, cmax_ref, m_ref, acc_ref, c_ref, ma_ref, *, n_blk):
    h = pl.program_id(0)
    sb = pl.program_id(1)
    n_sb = n_blk // CHUNK_B
    T = BLOCK_B
    W = CHUNK_B * T
    assert T & (T - 1) == 0
    cur = sb % 2
    key = lax.broadcasted_iota(jnp.int32, (T, W), 0)
    tok = lax.broadcasted_iota(jnp.int32, (T, W), 1)

    def write_qaug(slot, qst, sblock):
        gate = jnp.dot(kmean_ref[:n_blk, :].astype(BF16), qst, preferred_element_type=F32)
        blk = lax.broadcasted_iota(jnp.int32, (n_blk, W), 0)
        tok_blk = jnp.right_shift(lax.broadcasted_iota(jnp.int32, (1, W), 1), T.bit_length() - 1)
        own = sblock * CHUNK_B + tok_blk
        valid = blk < own
        g = jnp.where(valid, gate, -jnp.inf)
        sel = jnp.zeros((n_blk, W), F32)
        for _ in range(TOPK_B):
            mx = jnp.max(g, axis=0, keepdims=True)
            idx = jnp.min(jnp.where(g == mx, blk, n_blk), axis=0, keepdims=True)
            pick = blk == idx
            sel = jnp.where(pick, 1.0, sel)
            g = jnp.where(pick, -jnp.inf, g)
        keep = jnp.where(valid, sel, 0.0) + jnp.where(blk == own, 1.0, 0.0)
        selbias = jnp.where(keep > 0.0, 0.0, NEG_BIG)

        slope = jnp.exp2(jnp.full((1, W), -8.0 / NH_B, F32) * (h + 1).astype(F32))
        v3 = -slope * (T * own).astype(F32) * LOG2E
        v3_hi = v3.astype(BF16).astype(F32)
        v3_r = v3 - v3_hi
        v3_lo = v3_r.astype(BF16).astype(F32)
        v3_terms = (v3_hi, v3_lo, v3_r - v3_lo)
        c_terms = _bf16_split_const(LOG2E)
        n_tail = LANE - n_blk
        r = lax.broadcasted_iota(jnp.int32, (n_tail, W), 0)
        tail = jnp.zeros((n_tail, W), F32)
        for t in range(N_SPLIT):
            tail = jnp.where(r == t, c_terms[t], tail)
            tail = jnp.where(r == N_SPLIT + t, c_terms[t], tail)
            tail = jnp.where(r == 2 * N_SPLIT + t, v3_terms[t], tail)
        qaug_ref[slot, :DH_B, :] = qst
        qaug_ref[slot, DH_B:DH_B + n_blk, :] = selbias.astype(BF16)
        qaug_ref[slot, DH_B + n_blk:, :] = tail.astype(BF16)

    def block_scores(slot, cid, b):
        start = pl.multiple_of(cid * W + b * T, T)
        return jnp.dot(kaug_ref[pl.ds(start, T), :], qaug_ref[slot],
                       preferred_element_type=F32)

    def own_block_scores(slot, sblock, b):
        return jnp.where(key + b * T > tok, NEG_BIG, block_scores(slot, sblock, b))

    @pl.when(sb == 0)
    def _():
        write_qaug(0, qst0_ref[...], 0)
        cmax = None
        for b in range(CHUNK_B):
            sblk = own_block_scores(0, 0, b)
            s_ref[b * T:(b + 1) * T, :] = sblk
            mb = jnp.max(sblk, axis=0, keepdims=True)
            cmax = mb if cmax is None else jnp.maximum(cmax, mb)
        cmax_ref[...] = cmax

    m_ref[...] = jnp.full(m_ref.shape, NEG_BIG, F32)
    acc_ref[...] = jnp.zeros_like(acc_ref)

    def step(cid, refill, after_block=None):
        m_prev = m_ref[...]
        m_new = jnp.maximum(m_prev, cmax_ref[...])
        alpha = jnp.exp2(m_prev - m_new)
        pv = jnp.zeros((V_ROWS, W), F32)
        cmax = None
        for b in range(CHUNK_B):
            rows = slice(b * T, (b + 1) * T)
            p = jnp.exp2(s_ref[rows, :] - m_new)
            pv = pv + jnp.dot(vt_ref[cid * CHUNK_B + b], p.astype(BF16), preferred_element_type=F32)
            nb = refill(b)
            s_ref[rows, :] = nb
            mb = jnp.max(nb, axis=0, keepdims=True)
            cmax = mb if cmax is None else jnp.maximum(cmax, mb)
            if after_block is not None:
                after_block()
        acc_ref[...] = alpha * acc_ref[...] + pv
        m_ref[...] = m_new
        cmax_ref[...] = cmax

    def body(n, carry):
        step(jnp.where(n == 0, sb, n - 1), lambda b: block_scores(cur, n, b))
        return carry

    lax.fori_loop(0, sb, body, 0)

    mlstm_heads = _mlstm_step((h == 0) & (sb == 0), qa_ref, ka_ref, va_ref, oa_ref, if_ref, gain_ref,
                              ha_ref, c_ref, ma_ref)
    nxt = jnp.minimum(sb + 1, n_sb - 1)
    write_qaug(1 - cur, qstn_ref[...], nxt)

    def run_mlstm_head():
        if mlstm_heads:
            mlstm_heads.pop(0)()

    step(jnp.where(sb == 0, sb, sb - 1), lambda b: own_block_scores(1 - cur, nxt, b), run_mlstm_head)
    while mlstm_heads:
        mlstm_heads.pop(0)()
    acc = acc_ref[...]
    o_ref[...] = (acc[:DH_B, :] / acc[DH_B:DH_B + 1, :]).T.astype(o_ref.dtype)


def _branches(qst, kaug, vt, kmean, proj, ifp, gain_h):
    S = qst.shape[2]
    T = BLOCK_B
    W = CHUNK_B * T
    L = CHUNK_A
    n_blk = S // T
    n_sb = n_blk // CHUNK_B
    assert n_blk % CHUNK_B == 0 and NH_B * n_sb * L == S
    kern = functools.partial(_branches_kernel, n_blk=n_blk)

    def chunk(col):
        return lambda h, s: (h * n_sb + s, col)

    return pl.pallas_call(
        kern,
        grid=(NH_B, n_sb),
        in_specs=[
            pl.BlockSpec((None, DH_B, W), lambda h, s: (h, 0, 0)),
            pl.BlockSpec((None, DH_B, W), lambda h, s: (h, 0, jnp.minimum(s + 1, n_sb - 1))),
            pl.BlockSpec((None, S, DH_B + LANE), lambda h, s: (h, 0, 0)),
            pl.BlockSpec((None, n_blk, V_ROWS, T), lambda h, s: (h, 0, 0, 0)),
            pl.BlockSpec((LANE, DH_B), lambda h, s: (0, h)),
            pl.BlockSpec((L, QK_A), chunk(C_QA // QK_A)),
            pl.BlockSpec((L, QK_A), chunk(C_KA // QK_A)),
            pl.BlockSpec((L, W_A), chunk(C_VA // W_A)),
            pl.BlockSpec((L, W_A), chunk(C_OA // W_A)),
            pl.BlockSpec((L, IF_PAD), chunk(0)),
            pl.BlockSpec((1, W_A), lambda h, s: (0, 0)),
        ],
        out_specs=[
            pl.BlockSpec((W, DH_B), lambda h, s: (s, h)),
            pl.BlockSpec((L, W_A), chunk(0)),
        ],
        out_shape=[
            jax.ShapeDtypeStruct((S, W_B), BF16),
            jax.ShapeDtypeStruct((S, W_A), BF16),
        ],
        scratch_shapes=[
            pltpu.VMEM((2, DH_B + LANE, W), BF16),
            pltpu.VMEM((W, W), F32),
            pltpu.VMEM((1, W), F32),
            pltpu.VMEM((1, W), F32),
            pltpu.VMEM((V_ROWS, W), F32),
            pltpu.VMEM((NH_A, DQK_A, DV_A + LANE), F32),
            pltpu.VMEM((8, LANE), F32),
        ],
        compiler_params=_cparams(("arbitrary", "arbitrary")),
        name="branches",
    )(qst, qst, kaug, vt, kmean, proj, proj, proj, proj, ifp, gain_h)


def _merge_out_kernel(x_ref, ha_ref, hb_ref, ga_ref, gb_ref, wa_ref, wb_ref, wo_ref, o_ref):
    ta = jnp.dot(ha_ref[...], wa_ref[...], preferred_element_type=F32)
    tb = jnp.dot(hb_ref[...], wb_ref[...], preferred_element_type=F32)
    merged = (jax.nn.sigmoid(ga_ref[...].astype(F32)) * ta
              + jax.nn.sigmoid(gb_ref[...].astype(F32)) * tb).astype(BF16)
    o_ref[...] = x_ref[...] + jnp.dot(merged, wo_ref[...], preferred_element_type=F32)


def _merge_out(x, h_a, h_b, proj, w_a, w_b, w_o, layer, tm):
    S, D = x.shape
    const = dict(pipeline_mode=pl.Buffered(1))
    return pl.pallas_call(
        _merge_out_kernel,
        grid=(S // tm,),
        in_specs=[
            pl.BlockSpec((tm, D), lambda m: (m, 0)),
            pl.BlockSpec((tm, W_A), lambda m: (m, 0)),
            pl.BlockSpec((tm, W_B), lambda m: (m, 0)),
            pl.BlockSpec((tm, D), lambda m: (m, C_G // D)),
            pl.BlockSpec((tm, D), lambda m: (m, C_G // D + 1)),
            pl.BlockSpec((None, W_A, D), lambda m: (layer, 0, 0), **const),
            pl.BlockSpec((None, W_B, D), lambda m: (layer, 0, 0), **const),
            pl.BlockSpec((None, D, D), lambda m: (layer, 0, 0), **const),
        ],
        out_specs=pl.BlockSpec((tm, D), lambda m: (m, 0)),
        out_shape=jax.ShapeDtypeStruct((S, D), F32),
        compiler_params=_cparams(("parallel",)),
        name="merge_out",
    )(x, h_a, h_b, proj, proj, w_a, w_b, w_o)


def _mlp_kernel(x_ref, g_ref, wu_ref, wd_ref, o_ref, hn_ref):
    @pl.when(pl.program_id(1) == 0)
    def _():
        x = x_ref[...]
        hn_ref[...] = _rms(x, g_ref[...]).astype(BF16)
        o_ref[...] = x

    u = jnp.dot(hn_ref[...], wu_ref[...], preferred_element_type=F32)
    a = jnp.square(jnp.maximum(u, 0.0)).astype(BF16)
    o_ref[...] += jnp.dot(a, wd_ref[...], preferred_element_type=F32)


def _mlp(x, gain, w_up, w_down, layer, tm, tf):
    S, D = x.shape
    FF = w_up.shape[2]
    return pl.pallas_call(
        _mlp_kernel,
        grid=(S // tm, FF // tf),
        in_specs=[
            pl.BlockSpec((tm, D), lambda m, f: (m, 0)),
            pl.BlockSpec((1, D), lambda m, f: (0, 0)),
            pl.BlockSpec((None, D, tf), lambda m, f: (layer, 0, f)),
            pl.BlockSpec((None, tf, D), lambda m, f: (layer, f, 0)),
        ],
        out_specs=pl.BlockSpec((tm, D), lambda m, f: (m, 0)),
        out_shape=jax.ShapeDtypeStruct((S, D), F32),
        scratch_shapes=[pltpu.VMEM((tm, D), BF16)],
        compiler_params=_cparams(("parallel", "arbitrary"), MLP_VMEM_LIMIT),
        name="mlp",
    )(x, gain, w_up, w_down)


IN_PROJ_TM, IN_PROJ_TN = 1024, 2048
MERGE_TM = 512
MLP_TM, MLP_TF = 512, 2048
MLP_VMEM_LIMIT = 60 * 1024 * 1024


def _tile(n, pref):
    t = min(n, pref)
    while n % t:
        t -= LANE
    assert t > 0
    return t


def kernel(x, norm_mix, w_in, b_if, norm_h_mlstm, norm_q_moba, norm_k_moba,
           w_branch_a, w_branch_b, w_out, norm_mlp, w_up, w_down):
    B, S, D = x.shape
    assert B == 1 and S % BLOCK_B == 0 and D % LANE == 0
    depth = w_in.shape[0]
    c_if = C_QB
    n_main = w_in.shape[2] - N_IF
    assert n_main == C_G + 2 * D

    w_main, w_if = _w_repack(jnp.swapaxes(w_in, 1, 2), c_if, _tile(D, 256))
    w_a, w_b, w_o = w_branch_a.astype(BF16), w_branch_b.astype(BF16), w_out.astype(BF16)
    w_u, w_d = w_up.astype(BF16), w_down.astype(BF16)

    xs = x.reshape(S, D)
    for l in range(depth):
        bias_if = jnp.pad(b_if[l].astype(F32), (0, IF_PAD - N_IF)).reshape(1, IF_PAD)
        proj, ifp = _in_proj(xs, norm_mix[l].reshape(1, D), w_main, w_if, bias_if, l,
                             _tile(S, IN_PROJ_TM), _tile(n_main, IN_PROJ_TN))
        qst, kaug, vt, kmean = _moba_prep(proj, norm_q_moba[l].reshape(1, DH_B),
                                          norm_k_moba[l].reshape(1, DH_B))
        h_b, h_a = _branches(qst, kaug, vt, kmean, proj, ifp, norm_h_mlstm[l].reshape(1, W_A))
        xs = _merge_out(xs, h_a, h_b, proj, w_a, w_b, w_o, l, _tile(S, MERGE_TM))
        xs = _mlp(xs, norm_mlp[l].reshape(1, D), w_u, w_d, l, _tile(S, MLP_TM), _tile(w_up.shape[2], MLP_TF))
    return xs.reshape(B, S, D)
```

```python
import functools

import jax
import jax.numpy as jnp
import numpy as np
from jax import lax
from jax.experimental import pallas as pl
from jax.experimental.pallas import tpu as pltpu

F32 = jnp.float32
BF16 = jnp.bfloat16

NH_A, DQK_A, DV_A, CHUNK_A = 4, 128, 256, 128
GATE_SOFTCAP = 15.0
NH_B, DH_B, BLOCK_B, TOPK_B = 8, 128, 256, 3
EPS = 1e-6

W_A = NH_A * DV_A
W_B = NH_B * DH_B
QK_A = NH_A * DQK_A
N_IF = 2 * NH_A
LANE = 128
SUBLANE = 8
BF16_SUBLANES = 2 * SUBLANE
IF_PAD = LANE
NEG_BIG = -1e30
VMEM_LIMIT = 56 * 1024 * 1024

C_QA, C_KA, C_VA, C_OA = 0, QK_A, 2 * QK_A, 2 * QK_A + W_A
C_QB = C_OA + W_A
C_KB = C_QB + W_B
C_VB = C_KB + W_B
C_G = C_VB + W_B


def _cparams(sem, vmem_limit=VMEM_LIMIT):
    return pltpu.CompilerParams(dimension_semantics=sem, vmem_limit_bytes=vmem_limit)


def _rms(x, gain):
    ms = jnp.mean(x * x, axis=-1, keepdims=True)
    return x * lax.rsqrt(ms + EPS) * gain


def _w_repack_kernel(wt_ref, main_ref, if_ref, *, c_if):
    n_cols = wt_ref.shape[0]
    main_ref[:c_if, :] = wt_ref[:c_if, :].astype(main_ref.dtype)
    main_ref[c_if:, :] = wt_ref[c_if + N_IF:n_cols, :].astype(main_ref.dtype)
    row = lax.broadcasted_iota(jnp.int32, if_ref.shape, 0)
    if_ref[...] = jnp.where(row < N_IF, wt_ref[c_if:c_if + IF_PAD, :], 0.0).astype(if_ref.dtype)


def _w_repack(w_in_t, c_if, td):
    depth, n_cols, D = w_in_t.shape
    n_main = n_cols - N_IF
    return pl.pallas_call(
        functools.partial(_w_repack_kernel, c_if=c_if),
        grid=(depth, D // td),
        in_specs=[pl.BlockSpec((None, n_cols, td), lambda l, d: (l, 0, d))],
        out_specs=[
            pl.BlockSpec((None, n_main, td), lambda l, d: (l, 0, d)),
            pl.BlockSpec((None, IF_PAD, td), lambda l, d: (l, 0, d)),
        ],
        out_shape=[
            jax.ShapeDtypeStruct((depth, n_main, D), BF16),
            jax.ShapeDtypeStruct((depth, IF_PAD, D), BF16),
        ],
        compiler_params=_cparams(("parallel", "parallel")),
        name="w_repack",
    )(w_in_t)


_NT = (((1,), (1,)), ((), ()))


def _in_proj_kernel(x_ref, g_ref, w_ref, wif_ref, bif_ref, o_ref, oif_ref, xn_ref):
    @pl.when(pl.program_id(1) == 0)
    def _():
        xn = _rms(x_ref[...], g_ref[...]).astype(BF16)
        xn_ref[...] = xn
        oif_ref[...] = lax.dot_general(xn, wif_ref[...], _NT, preferred_element_type=F32) + bif_ref[...]

    o_ref[...] = lax.dot_general(xn_ref[...], w_ref[...], _NT,
                                 preferred_element_type=F32).astype(o_ref.dtype)


def _in_proj(x, gain, w_main_t, w_if_t, b_if, layer, tm, tn):
    S, D = x.shape
    N = w_main_t.shape[1]
    return pl.pallas_call(
        _in_proj_kernel,
        grid=(S // tm, N // tn),
        in_specs=[
            pl.BlockSpec((tm, D), lambda m, n: (m, 0)),
            pl.BlockSpec((1, D), lambda m, n: (0, 0)),
            pl.BlockSpec((None, tn, D), lambda m, n: (layer, n, 0)),
            pl.BlockSpec((None, IF_PAD, D), lambda m, n: (layer, 0, 0)),
            pl.BlockSpec((1, IF_PAD), lambda m, n: (0, 0)),
        ],
        out_specs=[
            pl.BlockSpec((tm, tn), lambda m, n: (m, n)),
            pl.BlockSpec((tm, IF_PAD), lambda m, n: (m, 0)),
        ],
        out_shape=[
            jax.ShapeDtypeStruct((S, N), BF16),
            jax.ShapeDtypeStruct((S, IF_PAD), F32),
        ],
        scratch_shapes=[pltpu.VMEM((tm, D), BF16)],
        compiler_params=_cparams(("parallel", "arbitrary"), BIG_TILE_VMEM_LIMIT),
        name="in_proj",
    )(x, gain, w_main_t, w_if_t, b_if)


def _mlstm_step(first, q_ref, k_ref, v_ref, oa_ref, if_ref, gain_ref, o_ref, c_ref, m_ref):
    L = CHUNK_A

    @pl.when(first)
    def _():
        c_ref[...] = jnp.zeros_like(c_ref)
        m_ref[...] = jnp.zeros_like(m_ref)

    g = if_ref[...]
    gc = GATE_SOFTCAP * jnp.tanh(g / GATE_SOFTCAP)
    col = lax.broadcasted_iota(jnp.int32, (L, LANE), 1)
    row = lax.broadcasted_iota(jnp.int32, (L, LANE), 0)
    log_f = jnp.minimum(gc, 0.0) - jnp.log1p(jnp.exp(-jnp.abs(gc)))
    G = jnp.where(col < NH_A, gc, log_f)
    tril = (row >= col).astype(F32)
    Bc = jnp.dot(tril, G, preferred_element_type=F32, precision=lax.Precision.HIGHEST)
    RT = (G - pltpu.roll(Bc, LANE - NH_A, 1)).T
    causal = row >= col
    scale = DQK_A ** -0.5
    ones_col = jnp.where(col == 0, 1.0, 0.0).astype(BF16)

    def head(h):
        qh = q_ref[:, h * DQK_A:(h + 1) * DQK_A]
        kh = k_ref[:, h * DQK_A:(h + 1) * DQK_A]
        vh = v_ref[:, h * DV_A:(h + 1) * DV_A]
        vext = jnp.concatenate([vh, ones_col], axis=1)
        b_col = Bc[:, NH_A + h:NH_A + h + 1]
        b_last = Bc[L - 1:L, NH_A + h:NH_A + h + 1]
        ig_col = G[:, h:h + 1]
        m_prev = m_ref[h:h + 1, 0:1]
        c_prev = c_ref[h]

        log_d = jnp.where(causal, b_col + RT[h:h + 1, :], -jnp.inf)
        m_inter = b_col + m_prev
        m_row = jnp.maximum(m_inter, jnp.max(log_d, axis=1, keepdims=True))
        dmat = jnp.exp(log_d - m_row)
        s = lax.dot_general(qh, kh, (((1,), (1,)), ((), ())), preferred_element_type=F32)
        sd = (s * (dmat * scale)).astype(BF16)
        inter = jnp.exp(m_inter - m_row) * scale
        num_ext = (jnp.dot(sd, vext, preferred_element_type=F32)
                   + inter * jnp.dot(qh, c_prev.astype(BF16), preferred_element_type=F32))
        num = num_ext[:, :DV_A]
        den = num_ext[:, DV_A:DV_A + 1]
        hval = num / jnp.maximum(jnp.abs(den), jnp.exp(-m_row))

        hn = _rms(hval, gain_ref[:, h * DV_A:(h + 1) * DV_A])
        og = jax.nn.sigmoid(oa_ref[:, h * DV_A:(h + 1) * DV_A].astype(F32))
        o_ref[:, h * DV_A:(h + 1) * DV_A] = (hn * og).astype(o_ref.dtype)

        log_w = b_last - b_col + ig_col
        m_new = jnp.maximum(b_last + m_prev, jnp.max(log_w, axis=0, keepdims=True))
        w = jnp.exp(log_w - m_new)
        decay = jnp.exp(b_last + m_prev - m_new)
        wv = (w * vext.astype(F32)).astype(BF16)
        c_ref[h] = decay * c_prev + lax.dot_general(
            kh, wv, (((0,), (0,)), ((), ())), preferred_element_type=F32)
        m_ref[h:h + 1, :] = jnp.broadcast_to(m_new, (1, LANE))

    return [functools.partial(head, h) for h in range(NH_A)]


LOG2E = 1.4426950408889634
N_SPLIT = 3
V_ROWS = DH_B + BF16_SUBLANES


def _alibi_slope(h):
    return 2.0 ** (-8.0 * (h + 1) / NH_B)


def _bf16_split_const(x):
    terms, rem = [], float(np.float32(x))
    for _ in range(N_SPLIT):
        t = float(np.float32(rem).astype(BF16))
        terms.append(t)
        rem -= t
    return terms


def _moba_prep_kernel(q_ref, k_ref, v_ref, gq_ref, gk_ref, qst_ref, kaug_ref, vt_ref, kmean_ref, *, n_blk):
    i = pl.program_id(0)
    T = BLOCK_B
    scale = DH_B ** -0.5 * LOG2E
    lane = lax.broadcasted_iota(jnp.int32, (T, LANE), 1)
    pos = lax.broadcasted_iota(jnp.int32, (T, LANE), 0).astype(F32)
    blk_f = i.astype(F32)
    row_is_blk = lax.broadcasted_iota(jnp.int32, (LANE, DH_B), 0) == i
    ones_rows = jnp.where(lax.broadcasted_iota(jnp.int32, (V_ROWS - DH_B, T), 0) == 0, 1.0, 0.0)

    @pl.when(i == 0)
    def _():
        kmean_ref[...] = jnp.zeros_like(kmean_ref)

    for h in range(NH_B):
        sl = slice(h * DH_B, (h + 1) * DH_B)
        qn = _rms(q_ref[:, sl].astype(F32), gq_ref[...])
        qst_ref[h] = (qn * scale).T.astype(qst_ref.dtype)
        kn = _rms(k_ref[:, sl].astype(F32), gk_ref[...])
        kmean_ref[:, sl] = jnp.where(row_is_blk, jnp.mean(kn, axis=0, keepdims=True), kmean_ref[:, sl])
        slope = _alibi_slope(h)
        extra = jnp.where(lane < n_blk + 3 * N_SPLIT, 1.0, 0.0)
        extra = jnp.where(lane < n_blk + 2 * N_SPLIT, slope * pos, extra)
        extra = jnp.where(lane < n_blk + N_SPLIT, slope * T * blk_f, extra)
        extra = jnp.where(lane < n_blk, jnp.where(lane == i, 1.0, 0.0), extra)
        kaug_ref[h, :, :DH_B] = kn.astype(kaug_ref.dtype)
        kaug_ref[h, :, DH_B:] = extra.astype(kaug_ref.dtype)
        vt = jnp.concatenate([v_ref[:, sl].astype(F32).T, ones_rows], axis=0)
        vt_ref[h] = vt.astype(vt_ref.dtype)


def _moba_prep(proj, gq, gk):
    S = proj.shape[0]
    T = BLOCK_B
    n_blk = S // T
    assert n_blk + 3 * N_SPLIT <= LANE
    kern = functools.partial(_moba_prep_kernel, n_blk=n_blk)
    return pl.pallas_call(
        kern,
        grid=(n_blk,),
        in_specs=[
            pl.BlockSpec((T, W_B), lambda i: (i, C_QB // W_B)),
            pl.BlockSpec((T, W_B), lambda i: (i, C_KB // W_B)),
            pl.BlockSpec((T, W_B), lambda i: (i, C_VB // W_B)),
            pl.BlockSpec((1, DH_B), lambda i: (0, 0)),
            pl.BlockSpec((1, DH_B), lambda i: (0, 0)),
        ],
        out_specs=[
            pl.BlockSpec((NH_B, DH_B, T), lambda i: (0, 0, i)),
            pl.BlockSpec((NH_B, T, DH_B + LANE), lambda i: (0, i, 0)),
            pl.BlockSpec((NH_B, None, V_ROWS, T), lambda i: (0, i, 0, 0)),
            pl.BlockSpec((LANE, W_B), lambda i: (0, 0)),
        ],
        out_shape=[
            jax.ShapeDtypeStruct((NH_B, DH_B, S), BF16),
            jax.ShapeDtypeStruct((NH_B, S, DH_B + LANE), BF16),
            jax.ShapeDtypeStruct((NH_B, n_blk, V_ROWS, T), BF16),
            jax.ShapeDtypeStruct((LANE, W_B), F32),
        ],
        compiler_params=_cparams(("arbitrary",)),
        name="moba_prep",
    )(proj, proj, proj, gq, gk)


CHUNK_B = 4


def _branches_kernel(qst0_ref, qstn_ref, kaug_ref, vt_ref, kmean_ref,
                     qa_ref, ka_ref, va_ref, oa_ref, if_ref, gain_ref,
                     o_ref, ha_ref,
                     qaug_ref, s_ref, cmax_ref, m_ref, acc_ref, c_ref, ma_ref, *, n_blk):
    h = pl.program_id(0)
    sb = pl.program_id(1)
    n_sb = n_blk // CHUNK_B
    T = BLOCK_B
    W = CHUNK_B * T
    assert T & (T - 1) == 0
    cur = sb % 2
    key = lax.broadcasted_iota(jnp.int32, (T, W), 0)
    tok = lax.broadcasted_iota(jnp.int32, (T, W), 1)

    def write_qaug(slot, qst, sblock):
        gate = jnp.dot(kmean_ref[:n_blk, :].astype(BF16), qst, preferred_element_type=F32)
        blk = lax.broadcasted_iota(jnp.int32, (n_blk, W), 0)
        tok_blk = jnp.right_shift(lax.broadcasted_iota(jnp.int32, (1, W), 1), T.bit_length() - 1)
        own = sblock * CHUNK_B + tok_blk
        valid = blk < own
        g = jnp.where(valid, gate, -jnp.inf)
        sel = jnp.zeros((n_blk, W), F32)
        for _ in range(TOPK_B):
            mx = jnp.max(g, axis=0, keepdims=True)
            idx = jnp.min(jnp.where(g == mx, blk, n_blk), axis=0, keepdims=True)
            pick = blk == idx
            sel = jnp.where(pick, 1.0, sel)
            g = jnp.where(pick, -jnp.inf, g)
        keep = jnp.where(valid, sel, 0.0) + jnp.where(blk == own, 1.0, 0.0)
        selbias = jnp.where(keep > 0.0, 0.0, NEG_BIG)

        slope = jnp.exp2(jnp.full((1, W), -8.0 / NH_B, F32) * (h + 1).astype(F32))
        v3 = -slope * (T * own).astype(F32) * LOG2E
        v3_hi = v3.astype(BF16).astype(F32)
        v3_r = v3 - v3_hi
        v3_lo = v3_r.astype(BF16).astype(F32)
        v3_terms = (v3_hi, v3_lo, v3_r - v3_lo)
        c_terms = _bf16_split_const(LOG2E)
        n_tail = LANE - n_blk
        r = lax.broadcasted_iota(jnp.int32, (n_tail, W), 0)
        tail = jnp.zeros((n_tail, W), F32)
        for t in range(N_SPLIT):
            tail = jnp.where(r == t, c_terms[t], tail)
            tail = jnp.where(r == N_SPLIT + t, c_terms[t], tail)
            tail = jnp.where(r == 2 * N_SPLIT + t, v3_terms[t], tail)
        qaug_ref[slot, :DH_B, :] = qst
        qaug_ref[slot, DH_B:DH_B + n_blk, :] = selbias.astype(BF16)
        qaug_ref[slot, DH_B + n_blk:, :] = tail.astype(BF16)

    def block_scores(slot, cid, b):
        start = pl.multiple_of(cid * W + b * T, T)
        return jnp.dot(kaug_ref[pl.ds(start, T), :], qaug_ref[slot],
                       preferred_element_type=F32)

    def own_block_scores(slot, sblock, b):
        return jnp.where(key + b * T > tok, NEG_BIG, block_scores(slot, sblock, b))

    @pl.when(sb == 0)
    def _():
        write_qaug(0, qst0_ref[...], 0)
        cmax = None
        for b in range(CHUNK_B):
            sblk = own_block_scores(0, 0, b)
            s_ref[b * T:(b + 1) * T, :] = sblk
            mb = jnp.max(sblk, axis=0, keepdims=True)
            cmax = mb if cmax is None else jnp.maximum(cmax, mb)
        cmax_ref[...] = cmax

    m_ref[...] = jnp.full(m_ref.shape, NEG_BIG, F32)
    acc_ref[...] = jnp.zeros_like(acc_ref)

    def step(cid, refill, after_block=None):
        m_prev = m_ref[...]
        m_new = jnp.maximum(m_prev, cmax_ref[...])
        alpha = jnp.exp2(m_prev - m_new)
        pv = jnp.zeros((V_ROWS, W), F32)
        cmax = None
        for b in range(CHUNK_B):
            rows = slice(b * T, (b + 1) * T)
            p = jnp.exp2(s_ref[rows, :] - m_new)
            pv = pv + jnp.dot(vt_ref[cid * CHUNK_B + b], p.astype(BF16), preferred_element_type=F32)
            nb = refill(b)
            s_ref[rows, :] = nb
            mb = jnp.max(nb, axis=0, keepdims=True)
            cmax = mb if cmax is None else jnp.maximum(cmax, mb)
            if after_block is not None:
                after_block()
        acc_ref[...] = alpha * acc_ref[...] + pv
        m_ref[...] = m_new
        cmax_ref[...] = cmax

    def body(n, carry):
        step(jnp.where(n == 0, sb, n - 1), lambda b: block_scores(cur, n, b))
        return carry

    lax.fori_loop(0, sb, body, 0)

    mlstm_heads = _mlstm_step((h == 0) & (sb == 0), qa_ref, ka_ref, va_ref, oa_ref, if_ref, gain_ref,
                              ha_ref, c_ref, ma_ref)
    nxt = jnp.minimum(sb + 1, n_sb - 1)
    write_qaug(1 - cur, qstn_ref[...], nxt)

    def run_mlstm_head():
        if mlstm_heads:
            mlstm_heads.pop(0)()

    step(jnp.where(sb == 0, sb, sb - 1), lambda b: own_block_scores(1 - cur, nxt, b), run_mlstm_head)
    while mlstm_heads:
        mlstm_heads.pop(0)()
    acc = acc_ref[...]
    o_ref[...] = (acc[:DH_B, :] / acc[DH_B:DH_B + 1, :]).T.astype(o_ref.dtype)


def _branches(qst, kaug, vt, kmean, proj, ifp, gain_h):
    S = qst.shape[2]
    T = BLOCK_B
    W = CHUNK_B * T
    L = CHUNK_A
    n_blk = S // T
    n_sb = n_blk // CHUNK_B
    assert n_blk % CHUNK_B == 0 and NH_B * n_sb * L == S
    kern = functools.partial(_branches_kernel, n_blk=n_blk)

    def chunk(col):
        return lambda h, s: (h * n_sb + s, col)

    return pl.pallas_call(
        kern,
        grid=(NH_B, n_sb),
        in_specs=[
            pl.BlockSpec((None, DH_B, W), lambda h, s: (h, 0, 0)),
            pl.BlockSpec((None, DH_B, W), lambda h, s: (h, 0, jnp.minimum(s + 1, n_sb - 1))),
            pl.BlockSpec((None, S, DH_B + LANE), lambda h, s: (h, 0, 0)),
            pl.BlockSpec((None, n_blk, V_ROWS, T), lambda h, s: (h, 0, 0, 0)),
            pl.BlockSpec((LANE, DH_B), lambda h, s: (0, h)),
            pl.BlockSpec((L, QK_A), chunk(C_QA // QK_A)),
            pl.BlockSpec((L, QK_A), chunk(C_KA // QK_A)),
            pl.BlockSpec((L, W_A), chunk(C_VA // W_A)),
            pl.BlockSpec((L, W_A), chunk(C_OA // W_A)),
            pl.BlockSpec((L, IF_PAD), chunk(0)),
            pl.BlockSpec((1, W_A), lambda h, s: (0, 0)),
        ],
        out_specs=[
            pl.BlockSpec((W, DH_B), lambda h, s: (s, h)),
            pl.BlockSpec((L, W_A), chunk(0)),
        ],
        out_shape=[
            jax.ShapeDtypeStruct((S, W_B), BF16),
            jax.ShapeDtypeStruct((S, W_A), BF16),
        ],
        scratch_shapes=[
            pltpu.VMEM((2, DH_B + LANE, W), BF16),
            pltpu.VMEM((W, W), F32),
            pltpu.VMEM((1, W), F32),
            pltpu.VMEM((1, W), F32),
            pltpu.VMEM((V_ROWS, W), F32),
            pltpu.VMEM((NH_A, DQK_A, DV_A + LANE), F32),
            pltpu.VMEM((SUBLANE, LANE), F32),
        ],
        compiler_params=_cparams(("arbitrary", "arbitrary")),
        name="branches",
    )(qst, qst, kaug, vt, kmean, proj, proj, proj, proj, ifp, gain_h)


def _merge_out_kernel(x_ref, ha_ref, hb_ref, ga_ref, gb_ref, wa_ref, wb_ref, wo_ref, o_ref):
    ta = jnp.dot(ha_ref[...], wa_ref[...], preferred_element_type=F32)
    tb = jnp.dot(hb_ref[...], wb_ref[...], preferred_element_type=F32)
    merged = (jax.nn.sigmoid(ga_ref[...].astype(F32)) * ta
              + jax.nn.sigmoid(gb_ref[...].astype(F32)) * tb).astype(BF16)
    o_ref[...] = x_ref[...] + jnp.dot(merged, wo_ref[...], preferred_element_type=F32)


def _merge_out(x, h_a, h_b, proj, w_a, w_b, w_o, layer, tm):
    S, D = x.shape
    const = dict(pipeline_mode=pl.Buffered(1))
    return pl.pallas_call(
        _merge_out_kernel,
        grid=(S // tm,),
        in_specs=[
            pl.BlockSpec((tm, D), lambda m: (m, 0)),
            pl.BlockSpec((tm, W_A), lambda m: (m, 0)),
            pl.BlockSpec((tm, W_B), lambda m: (m, 0)),
            pl.BlockSpec((tm, D), lambda m: (m, C_G // D)),
            pl.BlockSpec((tm, D), lambda m: (m, C_G // D + 1)),
            pl.BlockSpec((None, W_A, D), lambda m: (layer, 0, 0), **const),
            pl.BlockSpec((None, W_B, D), lambda m: (layer, 0, 0), **const),
            pl.BlockSpec((None, D, D), lambda m: (layer, 0, 0), **const),
        ],
        out_specs=pl.BlockSpec((tm, D), lambda m: (m, 0)),
        out_shape=jax.ShapeDtypeStruct((S, D), F32),
        compiler_params=_cparams(("parallel",)),
        name="merge_out",
    )(x, h_a, h_b, proj, proj, w_a, w_b, w_o)


def _mlp_kernel(x_ref, g_ref, wu_ref, wd_ref, o_ref, hn_ref):
    @pl.when(pl.program_id(1) == 0)
    def _():
        x = x_ref[...]
        hn_ref[...] = _rms(x, g_ref[...]).astype(BF16)
        o_ref[...] = x

    u = jnp.dot(hn_ref[...], wu_ref[...], preferred_element_type=F32)
    a = jnp.square(jnp.maximum(u, 0.0)).astype(BF16)
    o_ref[...] += jnp.dot(a, wd_ref[...], preferred_element_type=F32)


def _mlp(x, gain, w_up, w_down, layer, tm, tf):
    S, D = x.shape
    FF = w_up.shape[2]
    return pl.pallas_call(
        _mlp_kernel,
        grid=(S // tm, FF // tf),
        in_specs=[
            pl.BlockSpec((tm, D), lambda m, f: (m, 0)),
            pl.BlockSpec((1, D), lambda m, f: (0, 0)),
            pl.BlockSpec((None, D, tf), lambda m, f: (layer, 0, f)),
            pl.BlockSpec((None, tf, D), lambda m, f: (layer, f, 0)),
        ],
        out_specs=pl.BlockSpec((tm, D), lambda m, f: (m, 0)),
        out_shape=jax.ShapeDtypeStruct((S, D), F32),
        scratch_shapes=[pltpu.VMEM((tm, D), BF16)],
        compiler_params=_cparams(("parallel", "arbitrary"), BIG_TILE_VMEM_LIMIT),
        name="mlp",
    )(x, gain, w_up, w_down)


IN_PROJ_TM, IN_PROJ_TN = 1024, 2560
MERGE_TM = 512
MLP_TM, MLP_TF = 512, 2048
BIG_TILE_VMEM_LIMIT = 60 * 1024 * 1024


def _tile(n, pref):
    t = min(n, pref)
    while n % t:
        t -= LANE
    assert t > 0
    return t


def kernel(x, norm_mix, w_in, b_if, norm_h_mlstm, norm_q_moba, norm_k_moba,
           w_branch_a, w_branch_b, w_out, norm_mlp, w_up, w_down):
    B, S, D = x.shape
    assert B == 1 and S % BLOCK_B == 0 and D % LANE == 0
    depth = w_in.shape[0]
    c_if = C_QB
    n_main = w_in.shape[2] - N_IF
    assert n_main == C_G + 2 * D

    w_main, w_if = _w_repack(jnp.swapaxes(w_in, 1, 2), c_if, _tile(D, 256))
    w_a, w_b, w_o = w_branch_a.astype(BF16), w_branch_b.astype(BF16), w_out.astype(BF16)
    w_u, w_d = w_up.astype(BF16), w_down.astype(BF16)

    xs = x.reshape(S, D)
    for l in range(depth):
        bias_if = jnp.pad(b_if[l].astype(F32), (0, IF_PAD - N_IF)).reshape(1, IF_PAD)
        proj, ifp = _in_proj(xs, norm_mix[l].reshape(1, D), w_main, w_if, bias_if, l,
                             _tile(S, IN_PROJ_TM), _tile(n_main, IN_PROJ_TN))
        qst, kaug, vt, kmean = _moba_prep(proj, norm_q_moba[l].reshape(1, DH_B),
                                          norm_k_moba[l].reshape(1, DH_B))
        h_b, h_a = _branches(qst, kaug, vt, kmean, proj, ifp, norm_h_mlstm[l].reshape(1, W_A))
        xs = _merge_out(xs, h_a, h_b, proj, w_a, w_b, w_o, l, _tile(S, MERGE_TM))
        xs = _mlp(xs, norm_mlp[l].reshape(1, D), w_u, w_d, l, _tile(S, MLP_TM), _tile(w_up.shape[2], MLP_TF))
    return xs.reshape(B, S, D)
```

```python
import functools

import jax
import jax.numpy as jnp
import numpy as np
from jax import lax
from jax.experimental import pallas as pl
from jax.experimental.pallas import tpu as pltpu

F32 = jnp.float32
BF16 = jnp.bfloat16

NH_A, DQK_A, DV_A, CHUNK_A = 4, 128, 256, 128
GATE_SOFTCAP = 15.0
NH_B, DH_B, BLOCK_B, TOPK_B = 8, 128, 256, 3
EPS = 1e-6

W_A = NH_A * DV_A
W_B = NH_B * DH_B
QK_A = NH_A * DQK_A
N_IF = 2 * NH_A
LANE = 128
SUBLANE = 8
BF16_SUBLANES = 2 * SUBLANE
IF_PAD = LANE
NEG_BIG = -1e30
VMEM_LIMIT = 56 * 1024 * 1024

C_QA, C_KA, C_VA, C_OA = 0, QK_A, 2 * QK_A, 2 * QK_A + W_A
C_QB = C_OA + W_A
C_KB = C_QB + W_B
C_VB = C_KB + W_B
C_G = C_VB + W_B


def _cparams(sem, vmem_limit=VMEM_LIMIT):
    return pltpu.CompilerParams(dimension_semantics=sem, vmem_limit_bytes=vmem_limit)


def _rms(x, gain):
    ms = jnp.mean(x * x, axis=-1, keepdims=True)
    return x * lax.rsqrt(ms + EPS) * gain


def _w_repack_kernel(wt_ref, main_ref, if_ref, *, c_if):
    n_cols = wt_ref.shape[0]
    main_ref[:c_if, :] = wt_ref[:c_if, :].astype(main_ref.dtype)
    main_ref[c_if:, :] = wt_ref[c_if + N_IF:n_cols, :].astype(main_ref.dtype)
    row = lax.broadcasted_iota(jnp.int32, if_ref.shape, 0)
    if_ref[...] = jnp.where(row < N_IF, wt_ref[c_if:c_if + IF_PAD, :], 0.0).astype(if_ref.dtype)


def _w_repack(w_in_t, c_if, td):
    depth, n_cols, D = w_in_t.shape
    n_main = n_cols - N_IF
    return pl.pallas_call(
        functools.partial(_w_repack_kernel, c_if=c_if),
        grid=(depth, D // td),
        in_specs=[pl.BlockSpec((None, n_cols, td), lambda l, d: (l, 0, d))],
        out_specs=[
            pl.BlockSpec((None, n_main, td), lambda l, d: (l, 0, d)),
            pl.BlockSpec((None, IF_PAD, td), lambda l, d: (l, 0, d)),
        ],
        out_shape=[
            jax.ShapeDtypeStruct((depth, n_main, D), BF16),
            jax.ShapeDtypeStruct((depth, IF_PAD, D), BF16),
        ],
        compiler_params=_cparams(("parallel", "parallel")),
        name="w_repack",
    )(w_in_t)


_NT = (((1,), (1,)), ((), ()))


def _in_proj_kernel(x_ref, g_ref, w_ref, wif_ref, bif_ref, o_ref, oif_ref, xn_ref):
    @pl.when(pl.program_id(1) == 0)
    def _():
        xn = _rms(x_ref[...], g_ref[...]).astype(BF16)
        xn_ref[...] = xn
        oif_ref[...] = lax.dot_general(xn, wif_ref[...], _NT, preferred_element_type=F32) + bif_ref[...]

    o_ref[...] = lax.dot_general(xn_ref[...], w_ref[...], _NT,
                                 preferred_element_type=F32).astype(o_ref.dtype)


def _in_proj(x, gain, w_main_t, w_if_t, b_if, layer, tm, tn):
    S, D = x.shape
    N = w_main_t.shape[1]
    return pl.pallas_call(
        _in_proj_kernel,
        grid=(S // tm, N // tn),
        in_specs=[
            pl.BlockSpec((tm, D), lambda m, n: (m, 0)),
            pl.BlockSpec((1, D), lambda m, n: (0, 0)),
            pl.BlockSpec((None, tn, D), lambda m, n: (layer, n, 0)),
            pl.BlockSpec((None, IF_PAD, D), lambda m, n: (layer, 0, 0)),
            pl.BlockSpec((1, IF_PAD), lambda m, n: (0, 0)),
        ],
        out_specs=[
            pl.BlockSpec((tm, tn), lambda m, n: (m, n)),
            pl.BlockSpec((tm, IF_PAD), lambda m, n: (m, 0)),
        ],
        out_shape=[
            jax.ShapeDtypeStruct((S, N), BF16),
            jax.ShapeDtypeStruct((S, IF_PAD), F32),
        ],
        scratch_shapes=[pltpu.VMEM((tm, D), BF16)],
        compiler_params=_cparams(("parallel", "arbitrary"), BIG_TILE_VMEM_LIMIT),
        name="in_proj",
    )(x, gain, w_main_t, w_if_t, b_if)


def _mlstm_step(first, q_ref, k_ref, v_ref, oa_ref, if_ref, gain_ref, o_ref, c_ref, m_ref):
    L = CHUNK_A

    @pl.when(first)
    def _():
        c_ref[...] = jnp.zeros_like(c_ref)
        m_ref[...] = jnp.zeros_like(m_ref)

    g = if_ref[...]
    gc = GATE_SOFTCAP * jnp.tanh(g / GATE_SOFTCAP)
    col = lax.broadcasted_iota(jnp.int32, (L, LANE), 1)
    row = lax.broadcasted_iota(jnp.int32, (L, LANE), 0)
    log_f = jnp.minimum(gc, 0.0) - jnp.log1p(jnp.exp(-jnp.abs(gc)))
    G = jnp.where(col < NH_A, gc, log_f)
    tril = (row >= col).astype(F32)
    Bc = jnp.dot(tril, G, preferred_element_type=F32, precision=lax.Precision.HIGHEST)
    RT = (G - pltpu.roll(Bc, LANE - NH_A, 1)).T
    causal = row >= col
    scale = DQK_A ** -0.5
    ones_col = jnp.where(col == 0, 1.0, 0.0).astype(BF16)

    def head(h):
        qh = q_ref[:, h * DQK_A:(h + 1) * DQK_A]
        kh = k_ref[:, h * DQK_A:(h + 1) * DQK_A]
        vh = v_ref[:, h * DV_A:(h + 1) * DV_A]
        vext = jnp.concatenate([vh, ones_col], axis=1)
        b_col = Bc[:, NH_A + h:NH_A + h + 1]
        b_last = Bc[L - 1:L, NH_A + h:NH_A + h + 1]
        ig_col = G[:, h:h + 1]
        m_prev = m_ref[h:h + 1, 0:1]
        c_prev = c_ref[h]

        log_d = jnp.where(causal, b_col + RT[h:h + 1, :], -jnp.inf)
        m_inter = b_col + m_prev
        m_row = jnp.maximum(m_inter, jnp.max(log_d, axis=1, keepdims=True))
        dmat = jnp.exp(log_d - m_row)
        s = lax.dot_general(qh, kh, (((1,), (1,)), ((), ())), preferred_element_type=F32)
        sd = (s * (dmat * scale)).astype(BF16)
        inter = jnp.exp(m_inter - m_row) * scale
        num_ext = (jnp.dot(sd, vext, preferred_element_type=F32)
                   + inter * jnp.dot(qh, c_prev.astype(BF16), preferred_element_type=F32))
        num = num_ext[:, :DV_A]
        den = num_ext[:, DV_A:DV_A + 1]
        hval = num / jnp.maximum(jnp.abs(den), jnp.exp(-m_row))

        hn = _rms(hval, gain_ref[:, h * DV_A:(h + 1) * DV_A])
        og = jax.nn.sigmoid(oa_ref[:, h * DV_A:(h + 1) * DV_A].astype(F32))
        o_ref[:, h * DV_A:(h + 1) * DV_A] = (hn * og).astype(o_ref.dtype)

        log_w = b_last - b_col + ig_col
        m_new = jnp.maximum(b_last + m_prev, jnp.max(log_w, axis=0, keepdims=True))
        w = jnp.exp(log_w - m_new)
        decay = jnp.exp(b_last + m_prev - m_new)
        wv = (w * vext.astype(F32)).astype(BF16)
        c_ref[h] = decay * c_prev + lax.dot_general(
            kh, wv, (((0,), (0,)), ((), ())), preferred_element_type=F32)
        m_ref[h:h + 1, :] = jnp.broadcast_to(m_new, (1, LANE))

    return [functools.partial(head, h) for h in range(NH_A)]


LOG2E = 1.4426950408889634
N_SPLIT = 3
V_ROWS = DH_B + BF16_SUBLANES


def _alibi_slope(h):
    return 2.0 ** (-8.0 * (h + 1) / NH_B)


def _bf16_split_const(x):
    terms, rem = [], float(np.float32(x))
    for _ in range(N_SPLIT):
        t = float(np.float32(rem).astype(BF16))
        terms.append(t)
        rem -= t
    return terms


def _moba_prep_kernel(q_ref, k_ref, v_ref, gq_ref, gk_ref, qst_ref, kaug_ref, vt_ref, kmean_ref, norm_ref, *, n_blk):
    i = pl.program_id(0)
    T = BLOCK_B
    scale = DH_B ** -0.5 * LOG2E
    lane = lax.broadcasted_iota(jnp.int32, (T, LANE), 1)
    pos = lax.broadcasted_iota(jnp.int32, (T, LANE), 0).astype(F32)
    blk_f = i.astype(F32)
    row_is_blk = lax.broadcasted_iota(jnp.int32, (LANE, DH_B), 0) == i
    ones_rows = jnp.where(lax.broadcasted_iota(jnp.int32, (V_ROWS - DH_B, T), 0) == 0, 1.0, 0.0)

    @pl.when(i == 0)
    def _():
        kmean_ref[...] = jnp.zeros_like(kmean_ref)
        norm_ref[...] = jnp.zeros_like(norm_ref)

    stat_lane = lax.broadcasted_iota(jnp.int32, norm_ref.shape, 1)
    sq_norms = jnp.zeros(norm_ref.shape, F32)

    def max_sq_norm(a):
        return jnp.max(jnp.sum(a * a, axis=1, keepdims=True), axis=0, keepdims=True)

    for h in range(NH_B):
        sl = slice(h * DH_B, (h + 1) * DH_B)
        qn = _rms(q_ref[:, sl].astype(F32), gq_ref[...])
        qst_ref[h] = (qn * scale).T.astype(qst_ref.dtype)
        kn = _rms(k_ref[:, sl].astype(F32), gk_ref[...])
        kmean_ref[:, sl] = jnp.where(row_is_blk, jnp.mean(kn, axis=0, keepdims=True), kmean_ref[:, sl])
        slope = _alibi_slope(h)
        extra = jnp.where(lane < n_blk + 3 * N_SPLIT, 1.0, 0.0)
        extra = jnp.where(lane < n_blk + 2 * N_SPLIT, slope * pos, extra)
        extra = jnp.where(lane < n_blk + N_SPLIT, slope * T * blk_f, extra)
        extra = jnp.where(lane < n_blk, jnp.where(lane == i, 1.0, 0.0), extra)
        kaug_ref[h, :, :DH_B] = kn.astype(kaug_ref.dtype)
        kaug_ref[h, :, DH_B:] = extra.astype(kaug_ref.dtype)
        vt = jnp.concatenate([v_ref[:, sl].astype(F32).T, ones_rows], axis=0)
        vt_ref[h] = vt.astype(vt_ref.dtype)
        sq_norms = jnp.where(stat_lane == h, max_sq_norm(qn), sq_norms)
        sq_norms = jnp.where(stat_lane == NH_B + h, max_sq_norm(kn), sq_norms)
    norm_ref[...] = jnp.maximum(norm_ref[...], sq_norms)


def _moba_prep(proj, gq, gk):
    S = proj.shape[0]
    T = BLOCK_B
    n_blk = S // T
    assert n_blk + 3 * N_SPLIT <= LANE and 2 * NH_B <= LANE
    kern = functools.partial(_moba_prep_kernel, n_blk=n_blk)
    return pl.pallas_call(
        kern,
        grid=(n_blk,),
        in_specs=[
            pl.BlockSpec((T, W_B), lambda i: (i, C_QB // W_B)),
            pl.BlockSpec((T, W_B), lambda i: (i, C_KB // W_B)),
            pl.BlockSpec((T, W_B), lambda i: (i, C_VB // W_B)),
            pl.BlockSpec((1, DH_B), lambda i: (0, 0)),
            pl.BlockSpec((1, DH_B), lambda i: (0, 0)),
        ],
        out_specs=[
            pl.BlockSpec((NH_B, DH_B, T), lambda i: (0, 0, i)),
            pl.BlockSpec((NH_B, T, DH_B + LANE), lambda i: (0, i, 0)),
            pl.BlockSpec((NH_B, None, V_ROWS, T), lambda i: (0, i, 0, 0)),
            pl.BlockSpec((LANE, W_B), lambda i: (0, 0)),
            pl.BlockSpec((SUBLANE, LANE), lambda i: (0, 0)),
        ],
        out_shape=[
            jax.ShapeDtypeStruct((NH_B, DH_B, S), BF16),
            jax.ShapeDtypeStruct((NH_B, S, DH_B + LANE), BF16),
            jax.ShapeDtypeStruct((NH_B, n_blk, V_ROWS, T), BF16),
            jax.ShapeDtypeStruct((LANE, W_B), F32),
            jax.ShapeDtypeStruct((SUBLANE, LANE), F32),
        ],
        compiler_params=_cparams(("arbitrary",)),
        name="moba_prep",
    )(proj, proj, proj, gq, gk)


CHUNK_B = 4


UNDERFLOW_LOG2 = 160.0
BOUND_SLACK = 1.02


def _first_live_chunk(norm_ref, h, sb, chunk_len):
    lane = lax.broadcasted_iota(jnp.int32, norm_ref.shape, 1)
    sq = norm_ref[...]
    q_sq = jnp.max(jnp.where(lane == h, sq, 0.0), keepdims=True)
    k_sq = jnp.max(jnp.where(lane == NH_B + h, sq, 0.0), keepdims=True)
    bound = BOUND_SLACK * jnp.sqrt(q_sq * k_sq) * DH_B ** -0.5 + 1.0
    reach = 2.0 * bound + UNDERFLOW_LOG2 / LOG2E
    slope = jnp.exp2(jnp.full((1, 1), -8.0 / NH_B, F32) * (h + 1).astype(F32))
    back = jnp.floor((reach / slope - 1.0) / chunk_len)
    back = jnp.minimum(back, sb.astype(F32))
    first = jnp.maximum(sb.astype(F32) - 1.0 - back, 0.0)
    return jnp.max(first).astype(jnp.int32)


def _branches_kernel(qst0_ref, qstn_ref, kaug_ref, vt_ref, kmean_ref, norm_ref,
                     qa_ref, ka_ref, va_ref, oa_ref, if_ref, gain_ref,
                     o_ref, ha_ref,
                     qaug_ref, s_ref, cmax_ref, m_ref, acc_ref, c_ref, ma_ref, *, n_blk):
    h = pl.program_id(0)
    sb = pl.program_id(1)
    n_sb = n_blk // CHUNK_B
    T = BLOCK_B
    W = CHUNK_B * T
    assert T & (T - 1) == 0
    cur = sb % 2
    key = lax.broadcasted_iota(jnp.int32, (T, W), 0)
    tok = lax.broadcasted_iota(jnp.int32, (T, W), 1)

    def write_qaug(slot, qst, sblock):
        gate = jnp.dot(kmean_ref[:n_blk, :].astype(BF16), qst, preferred_element_type=F32)
        blk = lax.broadcasted_iota(jnp.int32, (n_blk, W), 0)
        tok_blk = jnp.right_shift(lax.broadcasted_iota(jnp.int32, (1, W), 1), T.bit_length() - 1)
        own = sblock * CHUNK_B + tok_blk
        valid = blk < own
        g = jnp.where(valid, gate, -jnp.inf)
        sel = jnp.zeros((n_blk, W), F32)
        for _ in range(TOPK_B):
            mx = jnp.max(g, axis=0, keepdims=True)
            idx = jnp.min(jnp.where(g == mx, blk, n_blk), axis=0, keepdims=True)
            pick = blk == idx
            sel = jnp.where(pick, 1.0, sel)
            g = jnp.where(pick, -jnp.inf, g)
        keep = jnp.where(valid, sel, 0.0) + jnp.where(blk == own, 1.0, 0.0)
        selbias = jnp.where(keep > 0.0, 0.0, NEG_BIG)

        slope = jnp.exp2(jnp.full((1, W), -8.0 / NH_B, F32) * (h + 1).astype(F32))
        v3 = -slope * (T * own).astype(F32) * LOG2E
        v3_hi = v3.astype(BF16).astype(F32)
        v3_r = v3 - v3_hi
        v3_lo = v3_r.astype(BF16).astype(F32)
        v3_terms = (v3_hi, v3_lo, v3_r - v3_lo)
        c_terms = _bf16_split_const(LOG2E)
        n_tail = LANE - n_blk
        r = lax.broadcasted_iota(jnp.int32, (n_tail, W), 0)
        tail = jnp.zeros((n_tail, W), F32)
        for t in range(N_SPLIT):
            tail = jnp.where(r == t, c_terms[t], tail)
            tail = jnp.where(r == N_SPLIT + t, c_terms[t], tail)
            tail = jnp.where(r == 2 * N_SPLIT + t, v3_terms[t], tail)
        qaug_ref[slot, :DH_B, :] = qst
        qaug_ref[slot, DH_B:DH_B + n_blk, :] = selbias.astype(BF16)
        qaug_ref[slot, DH_B + n_blk:, :] = tail.astype(BF16)

    def block_scores(slot, cid, b):
        start = pl.multiple_of(cid * W + b * T, T)
        return jnp.dot(kaug_ref[pl.ds(start, T), :], qaug_ref[slot],
                       preferred_element_type=F32)

    def own_block_scores(slot, sblock, b):
        return jnp.where(key + b * T > tok, NEG_BIG, block_scores(slot, sblock, b))

    @pl.when(sb == 0)
    def _():
        write_qaug(0, qst0_ref[...], 0)
        cmax = None
        for b in range(CHUNK_B):
            sblk = own_block_scores(0, 0, b)
            s_ref[b * T:(b + 1) * T, :] = sblk
            mb = jnp.max(sblk, axis=0, keepdims=True)
            cmax = mb if cmax is None else jnp.maximum(cmax, mb)
        cmax_ref[...] = cmax

    m_ref[...] = jnp.full(m_ref.shape, NEG_BIG, F32)
    acc_ref[...] = jnp.zeros_like(acc_ref)

    def step(cid, refill, after_block=None):
        m_prev = m_ref[...]
        m_new = jnp.maximum(m_prev, cmax_ref[...])
        alpha = jnp.exp2(m_prev - m_new)
        pv = jnp.zeros((V_ROWS, W), F32)
        cmax = None
        for b in range(CHUNK_B):
            rows = slice(b * T, (b + 1) * T)
            p = jnp.exp2(s_ref[rows, :] - m_new)
            pv = pv + jnp.dot(vt_ref[cid * CHUNK_B + b], p.astype(BF16), preferred_element_type=F32)
            nb = refill(b)
            s_ref[rows, :] = nb
            mb = jnp.max(nb, axis=0, keepdims=True)
            cmax = mb if cmax is None else jnp.maximum(cmax, mb)
            if after_block is not None:
                after_block()
        acc_ref[...] = alpha * acc_ref[...] + pv
        m_ref[...] = m_new
        cmax_ref[...] = cmax

    first = _first_live_chunk(norm_ref, h, sb, W)

    def body(n, carry):
        step(jnp.where(n == first, sb, n - 1), lambda b: block_scores(cur, n, b))
        return carry

    lax.fori_loop(first, sb, body, 0)

    mlstm_heads = _mlstm_step((h == 0) & (sb == 0), qa_ref, ka_ref, va_ref, oa_ref, if_ref, gain_ref,
                              ha_ref, c_ref, ma_ref)
    nxt = jnp.minimum(sb + 1, n_sb - 1)
    write_qaug(1 - cur, qstn_ref[...], nxt)

    def run_mlstm_head():
        if mlstm_heads:
            mlstm_heads.pop(0)()

    step(jnp.where(sb == 0, sb, sb - 1), lambda b: own_block_scores(1 - cur, nxt, b), run_mlstm_head)
    while mlstm_heads:
        mlstm_heads.pop(0)()
    acc = acc_ref[...]
    o_ref[...] = (acc[:DH_B, :] / acc[DH_B:DH_B + 1, :]).T.astype(o_ref.dtype)


def _branches(qst, kaug, vt, kmean, sq_norms, proj, ifp, gain_h):
    S = qst.shape[2]
    T = BLOCK_B
    W = CHUNK_B * T
    L = CHUNK_A
    n_blk = S // T
    n_sb = n_blk // CHUNK_B
    assert n_blk % CHUNK_B == 0 and NH_B * n_sb * L == S
    kern = functools.partial(_branches_kernel, n_blk=n_blk)

    def chunk(col):
        return lambda h, s: (h * n_sb + s, col)

    return pl.pallas_call(
        kern,
        grid=(NH_B, n_sb),
        in_specs=[
            pl.BlockSpec((None, DH_B, W), lambda h, s: (h, 0, 0)),
            pl.BlockSpec((None, DH_B, W), lambda h, s: (h, 0, jnp.minimum(s + 1, n_sb - 1))),
            pl.BlockSpec((None, S, DH_B + LANE), lambda h, s: (h, 0, 0)),
            pl.BlockSpec((None, n_blk, V_ROWS, T), lambda h, s: (h, 0, 0, 0)),
            pl.BlockSpec((LANE, DH_B), lambda h, s: (0, h)),
            pl.BlockSpec((SUBLANE, LANE), lambda h, s: (0, 0)),
            pl.BlockSpec((L, QK_A), chunk(C_QA // QK_A)),
            pl.BlockSpec((L, QK_A), chunk(C_KA // QK_A)),
            pl.BlockSpec((L, W_A), chunk(C_VA // W_A)),
            pl.BlockSpec((L, W_A), chunk(C_OA // W_A)),
            pl.BlockSpec((L, IF_PAD), chunk(0)),
            pl.BlockSpec((1, W_A), lambda h, s: (0, 0)),
        ],
        out_specs=[
            pl.BlockSpec((W, DH_B), lambda h, s: (s, h)),
            pl.BlockSpec((L, W_A), chunk(0)),
        ],
        out_shape=[
            jax.ShapeDtypeStruct((S, W_B), BF16),
            jax.ShapeDtypeStruct((S, W_A), BF16),
        ],
        scratch_shapes=[
            pltpu.VMEM((2, DH_B + LANE, W), BF16),
            pltpu.VMEM((W, W), F32),
            pltpu.VMEM((1, W), F32),
            pltpu.VMEM((1, W), F32),
            pltpu.VMEM((V_ROWS, W), F32),
            pltpu.VMEM((NH_A, DQK_A, DV_A + LANE), F32),
            pltpu.VMEM((SUBLANE, LANE), F32),
        ],
        compiler_params=_cparams(("arbitrary", "arbitrary")),
        name="branches",
    )(qst, qst, kaug, vt, kmean, sq_norms, proj, proj, proj, proj, ifp, gain_h)


def _merge_out_kernel(x_ref, ha_ref, hb_ref, ga_ref, gb_ref, wa_ref, wb_ref, wo_ref, o_ref):
    ta = jnp.dot(ha_ref[...], wa_ref[...], preferred_element_type=F32)
    tb = jnp.dot(hb_ref[...], wb_ref[...], preferred_element_type=F32)
    merged = (jax.nn.sigmoid(ga_ref[...].astype(F32)) * ta
              + jax.nn.sigmoid(gb_ref[...].astype(F32)) * tb).astype(BF16)
    o_ref[...] = x_ref[...] + jnp.dot(merged, wo_ref[...], preferred_element_type=F32)


def _merge_out(x, h_a, h_b, proj, w_a, w_b, w_o, layer, tm):
    S, D = x.shape
    const = dict(pipeline_mode=pl.Buffered(1))
    return pl.pallas_call(
        _merge_out_kernel,
        grid=(S // tm,),
        in_specs=[
            pl.BlockSpec((tm, D), lambda m: (m, 0)),
            pl.BlockSpec((tm, W_A), lambda m: (m, 0)),
            pl.BlockSpec((tm, W_B), lambda m: (m, 0)),
            pl.BlockSpec((tm, D), lambda m: (m, C_G // D)),
            pl.BlockSpec((tm, D), lambda m: (m, C_G // D + 1)),
            pl.BlockSpec((None, W_A, D), lambda m: (layer, 0, 0), **const),
            pl.BlockSpec((None, W_B, D), lambda m: (layer, 0, 0), **const),
            pl.BlockSpec((None, D, D), lambda m: (layer, 0, 0), **const),
        ],
        out_specs=pl.BlockSpec((tm, D), lambda m: (m, 0)),
        out_shape=jax.ShapeDtypeStruct((S, D), F32),
        compiler_params=_cparams(("parallel",)),
        name="merge_out",
    )(x, h_a, h_b, proj, proj, w_a, w_b, w_o)


def _mlp_kernel(x_ref, g_ref, wu_ref, wd_ref, o_ref, hn_ref):
    @pl.when(pl.program_id(1) == 0)
    def _():
        x = x_ref[...]
        hn_ref[...] = _rms(x, g_ref[...]).astype(BF16)
        o_ref[...] = x

    u = jnp.dot(hn_ref[...], wu_ref[...], preferred_element_type=F32)
    a = jnp.square(jnp.maximum(u, 0.0)).astype(BF16)
    o_ref[...] += jnp.dot(a, wd_ref[...], preferred_element_type=F32)


def _mlp(x, gain, w_up, w_down, layer, tm, tf):
    S, D = x.shape
    FF = w_up.shape[2]
    return pl.pallas_call(
        _mlp_kernel,
        grid=(S // tm, FF // tf),
        in_specs=[
            pl.BlockSpec((tm, D), lambda m, f: (m, 0)),
            pl.BlockSpec((1, D), lambda m, f: (0, 0)),
            pl.BlockSpec((None, D, tf), lambda m, f: (layer, 0, f)),
            pl.BlockSpec((None, tf, D), lambda m, f: (layer, f, 0)),
        ],
        out_specs=pl.BlockSpec((tm, D), lambda m, f: (m, 0)),
        out_shape=jax.ShapeDtypeStruct((S, D), F32),
        scratch_shapes=[pltpu.VMEM((tm, D), BF16)],
        compiler_params=_cparams(("parallel", "arbitrary"), BIG_TILE_VMEM_LIMIT),
        name="mlp",
    )(x, gain, w_up, w_down)


IN_PROJ_TM, IN_PROJ_TN = 1024, 2560
MERGE_TM = 512
MLP_TM, MLP_TF = 512, 2048
BIG_TILE_VMEM_LIMIT = 60 * 1024 * 1024


def _tile(n, pref):
    t = min(n, pref)
    while n % t:
        t -= LANE
    assert t > 0
    return t


def kernel(x, norm_mix, w_in, b_if, norm_h_mlstm, norm_q_moba, norm_k_moba,
           w_branch_a, w_branch_b, w_out, norm_mlp, w_up, w_down):
    B, S, D = x.shape
    assert B == 1 and S % BLOCK_B == 0 and D % LANE == 0
    depth = w_in.shape[0]
    c_if = C_QB
    n_main = w_in.shape[2] - N_IF
    assert n_main == C_G + 2 * D

    w_main, w_if = _w_repack(jnp.swapaxes(w_in, 1, 2), c_if, _tile(D, 256))
    w_a, w_b, w_o = w_branch_a.astype(BF16), w_branch_b.astype(BF16), w_out.astype(BF16)
    w_u, w_d = w_up.astype(BF16), w_down.astype(BF16)

    xs = x.reshape(S, D)
    for l in range(depth):
        bias_if = jnp.pad(b_if[l].astype(F32), (0, IF_PAD - N_IF)).reshape(1, IF_PAD)
        proj, ifp = _in_proj(xs, norm_mix[l].reshape(1, D), w_main, w_if, bias_if, l,
                             _tile(S, IN_PROJ_TM), _tile(n_main, IN_PROJ_TN))
        qst, kaug, vt, kmean, sq_norms = _moba_prep(proj, norm_q_moba[l].reshape(1, DH_B),
                                          norm_k_moba[l].reshape(1, DH_B))
        h_b, h_a = _branches(qst, kaug, vt, kmean, sq_norms, proj, ifp, norm_h_mlstm[l].reshape(1, W_A))
        xs = _merge_out(xs, h_a, h_b, proj, w_a, w_b, w_o, l, _tile(S, MERGE_TM))
        xs = _mlp(xs, norm_mlp[l].reshape(1, D), w_u, w_d, l, _tile(S, MLP_TM), _tile(w_up.shape[2], MLP_TF))
    return xs.reshape(B, S, D)
```

```python
import functools

import jax
import jax.numpy as jnp
import numpy as np
from jax import lax
from jax.experimental import pallas as pl
from jax.experimental.pallas import tpu as pltpu

F32 = jnp.float32
BF16 = jnp.bfloat16

NH_A, DQK_A, DV_A, CHUNK_A = 4, 128, 256, 128
GATE_SOFTCAP = 15.0
NH_B, DH_B, BLOCK_B, TOPK_B = 8, 128, 256, 3
EPS = 1e-6

W_A = NH_A * DV_A
W_B = NH_B * DH_B
QK_A = NH_A * DQK_A
N_IF = 2 * NH_A
LANE = 128
SUBLANE = 8
BF16_SUBLANES = 2 * SUBLANE
IF_PAD = LANE
NEG_BIG = -1e30
VMEM_LIMIT = 56 * 1024 * 1024

C_QA, C_KA, C_VA, C_OA = 0, QK_A, 2 * QK_A, 2 * QK_A + W_A
C_QB = C_OA + W_A
C_KB = C_QB + W_B
C_VB = C_KB + W_B
C_G = C_VB + W_B


def _cparams(sem, vmem_limit=VMEM_LIMIT):
    return pltpu.CompilerParams(dimension_semantics=sem, vmem_limit_bytes=vmem_limit)


def _rms(x, gain):
    ms = jnp.mean(x * x, axis=-1, keepdims=True)
    return x * lax.rsqrt(ms + EPS) * gain


def _w_repack_kernel(wt_ref, main_ref, if_ref, *, c_if):
    n_cols = wt_ref.shape[0]
    main_ref[:c_if, :] = wt_ref[:c_if, :].astype(main_ref.dtype)
    main_ref[c_if:, :] = wt_ref[c_if + N_IF:n_cols, :].astype(main_ref.dtype)
    row = lax.broadcasted_iota(jnp.int32, if_ref.shape, 0)
    if_ref[...] = jnp.where(row < N_IF, wt_ref[c_if:c_if + IF_PAD, :], 0.0).astype(if_ref.dtype)


def _w_repack(w_in_t, c_if, td):
    depth, n_cols, D = w_in_t.shape
    n_main = n_cols - N_IF
    return pl.pallas_call(
        functools.partial(_w_repack_kernel, c_if=c_if),
        grid=(depth, D // td),
        in_specs=[pl.BlockSpec((None, n_cols, td), lambda l, d: (l, 0, d))],
        out_specs=[
            pl.BlockSpec((None, n_main, td), lambda l, d: (l, 0, d)),
            pl.BlockSpec((None, IF_PAD, td), lambda l, d: (l, 0, d)),
        ],
        out_shape=[
            jax.ShapeDtypeStruct((depth, n_main, D), BF16),
            jax.ShapeDtypeStruct((depth, IF_PAD, D), BF16),
        ],
        compiler_params=_cparams(("parallel", "parallel")),
        name="w_repack",
    )(w_in_t)


_NT = (((1,), (1,)), ((), ()))


def _in_proj_kernel(x_ref, g_ref, w_ref, wif_ref, bif_ref, o_ref, oif_ref, xn_ref):
    @pl.when(pl.program_id(1) == 0)
    def _():
        xn = _rms(x_ref[...], g_ref[...]).astype(BF16)
        xn_ref[...] = xn
        oif_ref[...] = lax.dot_general(xn, wif_ref[...], _NT, preferred_element_type=F32) + bif_ref[...]

    o_ref[...] = lax.dot_general(xn_ref[...], w_ref[...], _NT,
                                 preferred_element_type=F32).astype(o_ref.dtype)


def _in_proj(x, gain, w_main_t, w_if_t, b_if, layer, tm, tn):
    S, D = x.shape
    N = w_main_t.shape[1]
    return pl.pallas_call(
        _in_proj_kernel,
        grid=(S // tm, N // tn),
        in_specs=[
            pl.BlockSpec((tm, D), lambda m, n: (m, 0)),
            pl.BlockSpec((1, D), lambda m, n: (0, 0)),
            pl.BlockSpec((None, tn, D), lambda m, n: (layer, n, 0)),
            pl.BlockSpec((None, IF_PAD, D), lambda m, n: (layer, 0, 0)),
            pl.BlockSpec((1, IF_PAD), lambda m, n: (0, 0)),
        ],
        out_specs=[
            pl.BlockSpec((tm, tn), lambda m, n: (m, n)),
            pl.BlockSpec((tm, IF_PAD), lambda m, n: (m, 0)),
        ],
        out_shape=[
            jax.ShapeDtypeStruct((S, N), BF16),
            jax.ShapeDtypeStruct((S, IF_PAD), F32),
        ],
        scratch_shapes=[pltpu.VMEM((tm, D), BF16)],
        compiler_params=_cparams(("parallel", "arbitrary"), BIG_TILE_VMEM_LIMIT),
        name="in_proj",
    )(x, gain, w_main_t, w_if_t, b_if)


def _mlstm_step(first, q_ref, k_ref, v_ref, oa_ref, if_ref, gain_ref, o_ref, c_ref, m_ref):
    L = CHUNK_A

    @pl.when(first)
    def _():
        c_ref[...] = jnp.zeros_like(c_ref)
        m_ref[...] = jnp.zeros_like(m_ref)

    g = if_ref[...]
    gc = GATE_SOFTCAP * jnp.tanh(g / GATE_SOFTCAP)
    col = lax.broadcasted_iota(jnp.int32, (L, LANE), 1)
    row = lax.broadcasted_iota(jnp.int32, (L, LANE), 0)
    log_f = jnp.minimum(gc, 0.0) - jnp.log1p(jnp.exp(-jnp.abs(gc)))
    G = jnp.where(col < NH_A, gc, log_f)
    tril = (row >= col).astype(F32)
    Bc = jnp.dot(tril, G, preferred_element_type=F32, precision=lax.Precision.HIGHEST)
    RT = (G - pltpu.roll(Bc, LANE - NH_A, 1)).T
    causal = row >= col
    scale = DQK_A ** -0.5
    ones_col = jnp.where(col == 0, 1.0, 0.0).astype(BF16)

    def head(h):
        qh = q_ref[:, h * DQK_A:(h + 1) * DQK_A]
        kh = k_ref[:, h * DQK_A:(h + 1) * DQK_A]
        vh = v_ref[:, h * DV_A:(h + 1) * DV_A]
        vext = jnp.concatenate([vh, ones_col], axis=1)
        b_col = Bc[:, NH_A + h:NH_A + h + 1]
        b_last = Bc[L - 1:L, NH_A + h:NH_A + h + 1]
        ig_col = G[:, h:h + 1]
        m_prev = m_ref[h:h + 1, 0:1]
        c_prev = c_ref[h]

        log_d = jnp.where(causal, b_col + RT[h:h + 1, :], -jnp.inf)
        m_inter = b_col + m_prev
        m_row = jnp.maximum(m_inter, jnp.max(log_d, axis=1, keepdims=True))
        dmat = jnp.exp(log_d - m_row)
        s = lax.dot_general(qh, kh, (((1,), (1,)), ((), ())), preferred_element_type=F32)
        sd = (s * (dmat * scale)).astype(BF16)
        inter = jnp.exp(m_inter - m_row) * scale
        num_ext = (jnp.dot(sd, vext, preferred_element_type=F32)
                   + inter * jnp.dot(qh, c_prev.astype(BF16), preferred_element_type=F32))
        num = num_ext[:, :DV_A]
        den = num_ext[:, DV_A:DV_A + 1]
        hval = num / jnp.maximum(jnp.abs(den), jnp.exp(-m_row))

        hn = _rms(hval, gain_ref[:, h * DV_A:(h + 1) * DV_A])
        og = jax.nn.sigmoid(oa_ref[:, h * DV_A:(h + 1) * DV_A].astype(F32))
        o_ref[:, h * DV_A:(h + 1) * DV_A] = (hn * og).astype(o_ref.dtype)

        log_w = b_last - b_col + ig_col
        m_new = jnp.maximum(b_last + m_prev, jnp.max(log_w, axis=0, keepdims=True))
        w = jnp.exp(log_w - m_new)
        decay = jnp.exp(b_last + m_prev - m_new)
        wv = (w * vext.astype(F32)).astype(BF16)
        c_ref[h] = decay * c_prev + lax.dot_general(
            kh, wv, (((0,), (0,)), ((), ())), preferred_element_type=F32)
        m_ref[h:h + 1, :] = jnp.broadcast_to(m_new, (1, LANE))

    return [functools.partial(head, h) for h in range(NH_A)]


LOG2E = 1.4426950408889634
N_SPLIT = 3
V_ROWS = DH_B + BF16_SUBLANES


def _alibi_slope(h):
    return 2.0 ** (-8.0 * (h + 1) / NH_B)


def _bf16_split_const(x):
    terms, rem = [], float(np.float32(x))
    for _ in range(N_SPLIT):
        t = float(np.float32(rem).astype(BF16))
        terms.append(t)
        rem -= t
    return terms


def _moba_prep_kernel(q_ref, k_ref, v_ref, gq_ref, gk_ref, qst_ref, kaug_ref, vt_ref, kmean_ref, *, n_blk):
    i = pl.program_id(0)
    T = BLOCK_B
    scale = DH_B ** -0.5 * LOG2E
    lane = lax.broadcasted_iota(jnp.int32, (T, LANE), 1)
    pos = lax.broadcasted_iota(jnp.int32, (T, LANE), 0).astype(F32)
    blk_f = i.astype(F32)
    row_is_blk = lax.broadcasted_iota(jnp.int32, (LANE, DH_B), 0) == i
    ones_rows = jnp.where(lax.broadcasted_iota(jnp.int32, (V_ROWS - DH_B, T), 0) == 0, 1.0, 0.0)

    @pl.when(i == 0)
    def _():
        kmean_ref[...] = jnp.zeros_like(kmean_ref)

    for h in range(NH_B):
        sl = slice(h * DH_B, (h + 1) * DH_B)
        qn = _rms(q_ref[:, sl].astype(F32), gq_ref[...])
        qst_ref[h] = (qn * scale).T.astype(qst_ref.dtype)
        kn = _rms(k_ref[:, sl].astype(F32), gk_ref[...])
        kmean_ref[:, sl] = jnp.where(row_is_blk, jnp.mean(kn, axis=0, keepdims=True), kmean_ref[:, sl])
        slope = _alibi_slope(h)
        extra = jnp.where(lane < n_blk + 3 * N_SPLIT, 1.0, 0.0)
        extra = jnp.where(lane < n_blk + 2 * N_SPLIT, slope * pos, extra)
        extra = jnp.where(lane < n_blk + N_SPLIT, slope * T * blk_f, extra)
        extra = jnp.where(lane < n_blk, jnp.where(lane == i, 1.0, 0.0), extra)
        kaug_ref[h, :, :DH_B] = kn.astype(kaug_ref.dtype)
        kaug_ref[h, :, DH_B:] = extra.astype(kaug_ref.dtype)
        vt = jnp.concatenate([v_ref[:, sl].astype(F32).T, ones_rows], axis=0)
        vt_ref[h] = vt.astype(vt_ref.dtype)


def _moba_prep(proj, gq, gk):
    S = proj.shape[0]
    T = BLOCK_B
    n_blk = S // T
    assert n_blk + 3 * N_SPLIT <= LANE
    kern = functools.partial(_moba_prep_kernel, n_blk=n_blk)
    return pl.pallas_call(
        kern,
        grid=(n_blk,),
        in_specs=[
            pl.BlockSpec((T, W_B), lambda i: (i, C_QB // W_B)),
            pl.BlockSpec((T, W_B), lambda i: (i, C_KB // W_B)),
            pl.BlockSpec((T, W_B), lambda i: (i, C_VB // W_B)),
            pl.BlockSpec((1, DH_B), lambda i: (0, 0)),
            pl.BlockSpec((1, DH_B), lambda i: (0, 0)),
        ],
        out_specs=[
            pl.BlockSpec((NH_B, DH_B, T), lambda i: (0, 0, i)),
            pl.BlockSpec((NH_B, T, DH_B + LANE), lambda i: (0, i, 0)),
            pl.BlockSpec((NH_B, None, V_ROWS, T), lambda i: (0, i, 0, 0)),
            pl.BlockSpec((LANE, W_B), lambda i: (0, 0)),
        ],
        out_shape=[
            jax.ShapeDtypeStruct((NH_B, DH_B, S), BF16),
            jax.ShapeDtypeStruct((NH_B, S, DH_B + LANE), BF16),
            jax.ShapeDtypeStruct((NH_B, n_blk, V_ROWS, T), BF16),
            jax.ShapeDtypeStruct((LANE, W_B), F32),
        ],
        compiler_params=_cparams(("arbitrary",)),
        name="moba_prep",
    )(proj, proj, proj, gq, gk)


CHUNK_B = 4


UNDERFLOW_LOG2 = 160.0
BOUND_SLACK = 1.02


def _first_live_chunk(gq_ref, gk_ref, h, sb, chunk_len):
    gq_max = jnp.max(jnp.abs(gq_ref[...]), keepdims=True)
    gk_max = jnp.max(jnp.abs(gk_ref[...]), keepdims=True)
    bound = BOUND_SLACK * (DH_B ** 0.5) * gq_max * gk_max + 1.0
    reach = 2.0 * bound + UNDERFLOW_LOG2 / LOG2E
    slope = jnp.exp2(jnp.full((1, 1), -8.0 / NH_B, F32) * (h + 1).astype(F32))
    back = jnp.floor((reach / slope - 1.0) / chunk_len)
    back = jnp.minimum(back, sb.astype(F32))
    first = jnp.maximum(sb.astype(F32) - 1.0 - back, 0.0)
    return jnp.max(first).astype(jnp.int32)


def _branches_kernel(qst0_ref, qstn_ref, kaug_ref, vt_ref, kmean_ref, gq_ref, gk_ref,
                     qa_ref, ka_ref, va_ref, oa_ref, if_ref, gain_ref,
                     o_ref, ha_ref,
                     qaug_ref, s_ref, cmax_ref, m_ref, acc_ref, c_ref, ma_ref, *, n_blk):
    h = pl.program_id(0)
    sb = pl.program_id(1)
    n_sb = n_blk // CHUNK_B
    T = BLOCK_B
    W = CHUNK_B * T
    assert T & (T - 1) == 0
    cur = sb % 2
    key = lax.broadcasted_iota(jnp.int32, (T, W), 0)
    tok = lax.broadcasted_iota(jnp.int32, (T, W), 1)

    def write_qaug(slot, qst, sblock):
        gate = jnp.dot(kmean_ref[:n_blk, :].astype(BF16), qst, preferred_element_type=F32)
        blk = lax.broadcasted_iota(jnp.int32, (n_blk, W), 0)
        tok_blk = jnp.right_shift(lax.broadcasted_iota(jnp.int32, (1, W), 1), T.bit_length() - 1)
        own = sblock * CHUNK_B + tok_blk
        valid = blk < own
        g = jnp.where(valid, gate, -jnp.inf)
        sel = jnp.zeros((n_blk, W), F32)
        for _ in range(TOPK_B):
            mx = jnp.max(g, axis=0, keepdims=True)
            idx = jnp.min(jnp.where(g == mx, blk, n_blk), axis=0, keepdims=True)
            pick = blk == idx
            sel = jnp.where(pick, 1.0, sel)
            g = jnp.where(pick, -jnp.inf, g)
        keep = jnp.where(valid, sel, 0.0) + jnp.where(blk == own, 1.0, 0.0)
        selbias = jnp.where(keep > 0.0, 0.0, NEG_BIG)

        slope = jnp.exp2(jnp.full((1, W), -8.0 / NH_B, F32) * (h + 1).astype(F32))
        v3 = -slope * (T * own).astype(F32) * LOG2E
        v3_hi = v3.astype(BF16).astype(F32)
        v3_r = v3 - v3_hi
        v3_lo = v3_r.astype(BF16).astype(F32)
        v3_terms = (v3_hi, v3_lo, v3_r - v3_lo)
        c_terms = _bf16_split_const(LOG2E)
        n_tail = LANE - n_blk
        r = lax.broadcasted_iota(jnp.int32, (n_tail, W), 0)
        tail = jnp.zeros((n_tail, W), F32)
        for t in range(N_SPLIT):
            tail = jnp.where(r == t, c_terms[t], tail)
            tail = jnp.where(r == N_SPLIT + t, c_terms[t], tail)
            tail = jnp.where(r == 2 * N_SPLIT + t, v3_terms[t], tail)
        qaug_ref[slot, :DH_B, :] = qst
        qaug_ref[slot, DH_B:DH_B + n_blk, :] = selbias.astype(BF16)
        qaug_ref[slot, DH_B + n_blk:, :] = tail.astype(BF16)

    def block_scores(slot, cid, b):
        start = pl.multiple_of(cid * W + b * T, T)
        return jnp.dot(kaug_ref[pl.ds(start, T), :], qaug_ref[slot],
                       preferred_element_type=F32)

    def own_block_scores(slot, sblock, b):
        return jnp.where(key + b * T > tok, NEG_BIG, block_scores(slot, sblock, b))

    @pl.when(sb == 0)
    def _():
        write_qaug(0, qst0_ref[...], 0)
        cmax = None
        for b in range(CHUNK_B):
            sblk = own_block_scores(0, 0, b)
            s_ref[b * T:(b + 1) * T, :] = sblk
            mb = jnp.max(sblk, axis=0, keepdims=True)
            cmax = mb if cmax is None else jnp.maximum(cmax, mb)
        cmax_ref[...] = cmax

    m_ref[...] = jnp.full(m_ref.shape, NEG_BIG, F32)
    acc_ref[...] = jnp.zeros_like(acc_ref)

    def step(cid, refill, after_block=None):
        m_prev = m_ref[...]
        m_new = jnp.maximum(m_prev, cmax_ref[...])
        alpha = jnp.exp2(m_prev - m_new)
        pv = jnp.zeros((V_ROWS, W), F32)
        cmax = None
        for b in range(CHUNK_B):
            rows = slice(b * T, (b + 1) * T)
            p = jnp.exp2(s_ref[rows, :] - m_new)
            pv = pv + jnp.dot(vt_ref[cid * CHUNK_B + b], p.astype(BF16), preferred_element_type=F32)
            nb = refill(b)
            s_ref[rows, :] = nb
            mb = jnp.max(nb, axis=0, keepdims=True)
            cmax = mb if cmax is None else jnp.maximum(cmax, mb)
            if after_block is not None:
                after_block()
        acc_ref[...] = alpha * acc_ref[...] + pv
        m_ref[...] = m_new
        cmax_ref[...] = cmax

    first = _first_live_chunk(gq_ref, gk_ref, h, sb, W)

    def body(n, carry):
        step(jnp.where(n == first, sb, n - 1), lambda b: block_scores(cur, n, b))
        return carry

    lax.fori_loop(first, sb, body, 0)

    mlstm_heads = _mlstm_step((h == 0) & (sb == 0), qa_ref, ka_ref, va_ref, oa_ref, if_ref, gain_ref,
                              ha_ref, c_ref, ma_ref)
    nxt = jnp.minimum(sb + 1, n_sb - 1)
    write_qaug(1 - cur, qstn_ref[...], nxt)

    def run_mlstm_head():
        if mlstm_heads:
            mlstm_heads.pop(0)()

    step(jnp.where(sb == 0, sb, sb - 1), lambda b: own_block_scores(1 - cur, nxt, b), run_mlstm_head)
    while mlstm_heads:
        mlstm_heads.pop(0)()
    acc = acc_ref[...]
    o_ref[...] = (acc[:DH_B, :] / acc[DH_B:DH_B + 1, :]).T.astype(o_ref.dtype)


def _branches(qst, kaug, vt, kmean, gq, gk, proj, ifp, gain_h):
    S = qst.shape[2]
    T = BLOCK_B
    W = CHUNK_B * T
    L = CHUNK_A
    n_blk = S // T
    n_sb = n_blk // CHUNK_B
    assert n_blk % CHUNK_B == 0 and NH_B * n_sb * L == S
    kern = functools.partial(_branches_kernel, n_blk=n_blk)

    def chunk(col):
        return lambda h, s: (h * n_sb + s, col)

    return pl.pallas_call(
        kern,
        grid=(NH_B, n_sb),
        in_specs=[
            pl.BlockSpec((None, DH_B, W), lambda h, s: (h, 0, 0)),
            pl.BlockSpec((None, DH_B, W), lambda h, s: (h, 0, jnp.minimum(s + 1, n_sb - 1))),
            pl.BlockSpec((None, S, DH_B + LANE), lambda h, s: (h, 0, 0)),
            pl.BlockSpec((None, n_blk, V_ROWS, T), lambda h, s: (h, 0, 0, 0)),
            pl.BlockSpec((LANE, DH_B), lambda h, s: (0, h)),
            pl.BlockSpec((1, DH_B), lambda h, s: (0, 0)),
            pl.BlockSpec((1, DH_B), lambda h, s: (0, 0)),
            pl.BlockSpec((L, QK_A), chunk(C_QA // QK_A)),
            pl.BlockSpec((L, QK_A), chunk(C_KA // QK_A)),
            pl.BlockSpec((L, W_A), chunk(C_VA // W_A)),
            pl.BlockSpec((L, W_A), chunk(C_OA // W_A)),
            pl.BlockSpec((L, IF_PAD), chunk(0)),
            pl.BlockSpec((1, W_A), lambda h, s: (0, 0)),
        ],
        out_specs=[
            pl.BlockSpec((W, DH_B), lambda h, s: (s, h)),
            pl.BlockSpec((L, W_A), chunk(0)),
        ],
        out_shape=[
            jax.ShapeDtypeStruct((S, W_B), BF16),
            jax.ShapeDtypeStruct((S, W_A), BF16),
        ],
        scratch_shapes=[
            pltpu.VMEM((2, DH_B + LANE, W), BF16),
            pltpu.VMEM((W, W), F32),
            pltpu.VMEM((1, W), F32),
            pltpu.VMEM((1, W), F32),
            pltpu.VMEM((V_ROWS, W), F32),
            pltpu.VMEM((NH_A, DQK_A, DV_A + LANE), F32),
            pltpu.VMEM((SUBLANE, LANE), F32),
        ],
        compiler_params=_cparams(("arbitrary", "arbitrary")),
        name="branches",
    )(qst, qst, kaug, vt, kmean, gq, gk, proj, proj, proj, proj, ifp, gain_h)


def _merge_out_kernel(x_ref, ha_ref, hb_ref, ga_ref, gb_ref, wa_ref, wb_ref, wo_ref, o_ref):
    ta = jnp.dot(ha_ref[...], wa_ref[...], preferred_element_type=F32)
    tb = jnp.dot(hb_ref[...], wb_ref[...], preferred_element_type=F32)
    merged = (jax.nn.sigmoid(ga_ref[...].astype(F32)) * ta
              + jax.nn.sigmoid(gb_ref[...].astype(F32)) * tb).astype(BF16)
    o_ref[...] = x_ref[...] + jnp.dot(merged, wo_ref[...], preferred_element_type=F32)


def _merge_out(x, h_a, h_b, proj, w_a, w_b, w_o, layer, tm):
    S, D = x.shape
    const = dict(pipeline_mode=pl.Buffered(1))
    return pl.pallas_call(
        _merge_out_kernel,
        grid=(S // tm,),
        in_specs=[
            pl.BlockSpec((tm, D), lambda m: (m, 0)),
            pl.BlockSpec((tm, W_A), lambda m: (m, 0)),
            pl.BlockSpec((tm, W_B), lambda m: (m, 0)),
            pl.BlockSpec((tm, D), lambda m: (m, C_G // D)),
            pl.BlockSpec((tm, D), lambda m: (m, C_G // D + 1)),
            pl.BlockSpec((None, W_A, D), lambda m: (layer, 0, 0), **const),
            pl.BlockSpec((None, W_B, D), lambda m: (layer, 0, 0), **const),
            pl.BlockSpec((None, D, D), lambda m: (layer, 0, 0), **const),
        ],
        out_specs=pl.BlockSpec((tm, D), lambda m: (m, 0)),
        out_shape=jax.ShapeDtypeStruct((S, D), F32),
        compiler_params=_cparams(("parallel",)),
        name="merge_out",
    )(x, h_a, h_b, proj, proj, w_a, w_b, w_o)


def _mlp_kernel(x_ref, g_ref, wu_ref, wd_ref, o_ref, hn_ref):
    @pl.when(pl.program_id(1) == 0)
    def _():
        x = x_ref[...]
        hn_ref[...] = _rms(x, g_ref[...]).astype(BF16)
        o_ref[...] = x

    u = jnp.dot(hn_ref[...], wu_ref[...], preferred_element_type=F32)
    a = jnp.square(jnp.maximum(u, 0.0)).astype(BF16)
    o_ref[...] += jnp.dot(a, wd_ref[...], preferred_element_type=F32)


def _mlp(x, gain, w_up, w_down, layer, tm, tf):
    S, D = x.shape
    FF = w_up.shape[2]
    return pl.pallas_call(
        _mlp_kernel,
        grid=(S // tm, FF // tf),
        in_specs=[
            pl.BlockSpec((tm, D), lambda m, f: (m, 0)),
            pl.BlockSpec((1, D), lambda m, f: (0, 0)),
            pl.BlockSpec((None, D, tf), lambda m, f: (layer, 0, f)),
            pl.BlockSpec((None, tf, D), lambda m, f: (layer, f, 0)),
        ],
        out_specs=pl.BlockSpec((tm, D), lambda m, f: (m, 0)),
        out_shape=jax.ShapeDtypeStruct((S, D), F32),
        scratch_shapes=[pltpu.VMEM((tm, D), BF16)],
        compiler_params=_cparams(("parallel", "arbitrary"), BIG_TILE_VMEM_LIMIT),
        name="mlp",
    )(x, gain, w_up, w_down)


IN_PROJ_TM, IN_PROJ_TN = 1024, 2560
MERGE_TM = 512
MLP_TM, MLP_TF = 512, 2048
BIG_TILE_VMEM_LIMIT = 60 * 1024 * 1024


def _tile(n, pref):
    t = min(n, pref)
    while n % t:
        t -= LANE
    assert t > 0
    return t


def kernel(x, norm_mix, w_in, b_if, norm_h_mlstm, norm_q_moba, norm_k_moba,
           w_branch_a, w_branch_b, w_out, norm_mlp, w_up, w_down):
    B, S, D = x.shape
    assert B == 1 and S % BLOCK_B == 0 and D % LANE == 0
    depth = w_in.shape[0]
    c_if = C_QB
    n_main = w_in.shape[2] - N_IF
    assert n_main == C_G + 2 * D

    w_main, w_if = _w_repack(jnp.swapaxes(w_in, 1, 2), c_if, _tile(D, 256))
    w_a, w_b, w_o = w_branch_a.astype(BF16), w_branch_b.astype(BF16), w_out.astype(BF16)
    w_u, w_d = w_up.astype(BF16), w_down.astype(BF16)

    xs = x.reshape(S, D)
    for l in range(depth):
        bias_if = jnp.pad(b_if[l].astype(F32), (0, IF_PAD - N_IF)).reshape(1, IF_PAD)
        proj, ifp = _in_proj(xs, norm_mix[l].reshape(1, D), w_main, w_if, bias_if, l,
                             _tile(S, IN_PROJ_TM), _tile(n_main, IN_PROJ_TN))
        gq, gk = norm_q_moba[l].reshape(1, DH_B), norm_k_moba[l].reshape(1, DH_B)
        qst, kaug, vt, kmean = _moba_prep(proj, gq, gk)
        h_b, h_a = _branches(qst, kaug, vt, kmean, gq, gk, proj, ifp, norm_h_mlstm[l].reshape(1, W_A))
        xs = _merge_out(xs, h_a, h_b, proj, w_a, w_b, w_o, l, _tile(S, MERGE_TM))
        xs = _mlp(xs, norm_mlp[l].reshape(1, D), w_u, w_d, l, _tile(S, MLP_TM), _tile(w_up.shape[2], MLP_TF))
    return xs.reshape(B, S, D)
```

```python
import functools

import jax
import jax.numpy as jnp
import numpy as np
from jax import lax
from jax.experimental import pallas as pl
from jax.experimental.pallas import tpu as pltpu

F32 = jnp.float32
BF16 = jnp.bfloat16

NH_A, DQK_A, DV_A, CHUNK_A = 4, 128, 256, 128
GATE_SOFTCAP = 15.0
NH_B, DH_B, BLOCK_B, TOPK_B = 8, 128, 256, 3
EPS = 1e-6

W_A = NH_A * DV_A
W_B = NH_B * DH_B
QK_A = NH_A * DQK_A
N_IF = 2 * NH_A
LANE = 128
SUBLANE = 8
BF16_SUBLANES = 2 * SUBLANE
IF_PAD = LANE
NEG_BIG = -1e30
VMEM_LIMIT = 56 * 1024 * 1024

C_QA, C_KA, C_VA, C_OA = 0, QK_A, 2 * QK_A, 2 * QK_A + W_A
C_QB = C_OA + W_A
C_KB = C_QB + W_B
C_VB = C_KB + W_B
C_G = C_VB + W_B


def _cparams(sem, vmem_limit=VMEM_LIMIT):
    return pltpu.CompilerParams(dimension_semantics=sem, vmem_limit_bytes=vmem_limit)


def _rms(x, gain):
    ms = jnp.mean(x * x, axis=-1, keepdims=True)
    return x * lax.rsqrt(ms + EPS) * gain


def _w_repack_kernel(wt_ref, main_ref, if_ref, *, c_if):
    n_cols = wt_ref.shape[0]
    main_ref[:c_if, :] = wt_ref[:c_if, :].astype(main_ref.dtype)
    main_ref[c_if:, :] = wt_ref[c_if + N_IF:n_cols, :].astype(main_ref.dtype)
    row = lax.broadcasted_iota(jnp.int32, if_ref.shape, 0)
    if_ref[...] = jnp.where(row < N_IF, wt_ref[c_if:c_if + IF_PAD, :], 0.0).astype(if_ref.dtype)


def _w_repack(w_in_t, c_if, td):
    depth, n_cols, D = w_in_t.shape
    n_main = n_cols - N_IF
    return pl.pallas_call(
        functools.partial(_w_repack_kernel, c_if=c_if),
        grid=(depth, D // td),
        in_specs=[pl.BlockSpec((None, n_cols, td), lambda l, d: (l, 0, d))],
        out_specs=[
            pl.BlockSpec((None, n_main, td), lambda l, d: (l, 0, d)),
            pl.BlockSpec((None, IF_PAD, td), lambda l, d: (l, 0, d)),
        ],
        out_shape=[
            jax.ShapeDtypeStruct((depth, n_main, D), BF16),
            jax.ShapeDtypeStruct((depth, IF_PAD, D), BF16),
        ],
        compiler_params=_cparams(("parallel", "parallel")),
        name="w_repack",
    )(w_in_t)


_NT = (((1,), (1,)), ((), ()))


def _in_proj_kernel(x_ref, g_ref, w_ref, wif_ref, bif_ref, o_ref, oif_ref, xn_ref):
    @pl.when(pl.program_id(1) == 0)
    def _():
        xn = _rms(x_ref[...], g_ref[...]).astype(BF16)
        xn_ref[...] = xn
        oif_ref[...] = lax.dot_general(xn, wif_ref[...], _NT, preferred_element_type=F32) + bif_ref[...]

    o_ref[...] = lax.dot_general(xn_ref[...], w_ref[...], _NT,
                                 preferred_element_type=F32).astype(o_ref.dtype)


def _in_proj(x, gain, w_main_t, w_if_t, b_if, layer, tm, tn):
    S, D = x.shape
    N = w_main_t.shape[1]
    return pl.pallas_call(
        _in_proj_kernel,
        grid=(S // tm, N // tn),
        in_specs=[
            pl.BlockSpec((tm, D), lambda m, n: (m, 0)),
            pl.BlockSpec((1, D), lambda m, n: (0, 0)),
            pl.BlockSpec((None, tn, D), lambda m, n: (layer, n, 0)),
            pl.BlockSpec((None, IF_PAD, D), lambda m, n: (layer, 0, 0)),
            pl.BlockSpec((1, IF_PAD), lambda m, n: (0, 0)),
        ],
        out_specs=[
            pl.BlockSpec((tm, tn), lambda m, n: (m, n)),
            pl.BlockSpec((tm, IF_PAD), lambda m, n: (m, 0)),
        ],
        out_shape=[
            jax.ShapeDtypeStruct((S, N), BF16),
            jax.ShapeDtypeStruct((S, IF_PAD), F32),
        ],
        scratch_shapes=[pltpu.VMEM((tm, D), BF16)],
        compiler_params=_cparams(("parallel", "arbitrary"), BIG_TILE_VMEM_LIMIT),
        name="in_proj",
    )(x, gain, w_main_t, w_if_t, b_if)


def _mlstm_step(first, q_ref, k_ref, v_ref, oa_ref, if_ref, gain_ref, o_ref, c_ref, m_ref):
    L = CHUNK_A

    @pl.when(first)
    def _():
        c_ref[...] = jnp.zeros_like(c_ref)
        m_ref[...] = jnp.zeros_like(m_ref)

    g = if_ref[...]
    gc = GATE_SOFTCAP * jnp.tanh(g / GATE_SOFTCAP)
    col = lax.broadcasted_iota(jnp.int32, (L, LANE), 1)
    row = lax.broadcasted_iota(jnp.int32, (L, LANE), 0)
    log_f = jnp.minimum(gc, 0.0) - jnp.log1p(jnp.exp(-jnp.abs(gc)))
    G = jnp.where(col < NH_A, gc, log_f)
    tril = (row >= col).astype(F32)
    Bc = jnp.dot(tril, G, preferred_element_type=F32, precision=lax.Precision.HIGHEST)
    RT = (G - pltpu.roll(Bc, LANE - NH_A, 1)).T
    causal = row >= col
    scale = DQK_A ** -0.5
    ones_col = jnp.where(col == 0, 1.0, 0.0).astype(BF16)

    def head(h):
        qh = q_ref[:, h * DQK_A:(h + 1) * DQK_A]
        kh = k_ref[:, h * DQK_A:(h + 1) * DQK_A]
        vh = v_ref[:, h * DV_A:(h + 1) * DV_A]
        vext = jnp.concatenate([vh, ones_col], axis=1)
        b_col = Bc[:, NH_A + h:NH_A + h + 1]
        b_last = Bc[L - 1:L, NH_A + h:NH_A + h + 1]
        ig_col = G[:, h:h + 1]
        m_prev = m_ref[h:h + 1, 0:1]
        c_prev = c_ref[h]

        log_d = jnp.where(causal, b_col + RT[h:h + 1, :], -jnp.inf)
        m_inter = b_col + m_prev
        m_row = jnp.maximum(m_inter, jnp.max(log_d, axis=1, keepdims=True))
        dmat = jnp.exp(log_d - m_row)
        s = lax.dot_general(qh, kh, (((1,), (1,)), ((), ())), preferred_element_type=F32)
        sd = (s * (dmat * scale)).astype(BF16)
        inter = jnp.exp(m_inter - m_row) * scale
        num_ext = (jnp.dot(sd, vext, preferred_element_type=F32)
                   + inter * jnp.dot(qh, c_prev.astype(BF16), preferred_element_type=F32))
        num = num_ext[:, :DV_A]
        den = num_ext[:, DV_A:DV_A + 1]
        hval = num / jnp.maximum(jnp.abs(den), jnp.exp(-m_row))

        hn = _rms(hval, gain_ref[:, h * DV_A:(h + 1) * DV_A])
        og = jax.nn.sigmoid(oa_ref[:, h * DV_A:(h + 1) * DV_A].astype(F32))
        o_ref[:, h * DV_A:(h + 1) * DV_A] = (hn * og).astype(o_ref.dtype)

        log_w = b_last - b_col + ig_col
        m_new = jnp.maximum(b_last + m_prev, jnp.max(log_w, axis=0, keepdims=True))
        w = jnp.exp(log_w - m_new)
        decay = jnp.exp(b_last + m_prev - m_new)
        wv = (w * vext.astype(F32)).astype(BF16)
        c_ref[h] = decay * c_prev + lax.dot_general(
            kh, wv, (((0,), (0,)), ((), ())), preferred_element_type=F32)
        m_ref[h:h + 1, :] = jnp.broadcast_to(m_new, (1, LANE))

    return [functools.partial(head, h) for h in range(NH_A)]


LOG2E = 1.4426950408889634
N_SPLIT = 3
V_ROWS = DH_B + BF16_SUBLANES


def _alibi_slope(h):
    return 2.0 ** (-8.0 * (h + 1) / NH_B)


def _bf16_split_const(x):
    terms, rem = [], float(np.float32(x))
    for _ in range(N_SPLIT):
        t = float(np.float32(rem).astype(BF16))
        terms.append(t)
        rem -= t
    return terms


def _moba_prep_kernel(q_ref, k_ref, v_ref, gq_ref, gk_ref, qst_ref, kaug_ref, vt_ref, kmean_ref, *, n_blk):
    i = pl.program_id(0)
    T = BLOCK_B
    scale = DH_B ** -0.5 * LOG2E
    lane = lax.broadcasted_iota(jnp.int32, (T, LANE), 1)
    pos = lax.broadcasted_iota(jnp.int32, (T, LANE), 0).astype(F32)
    blk_f = i.astype(F32)
    row_is_blk = lax.broadcasted_iota(jnp.int32, (LANE, DH_B), 0) == i
    ones_rows = jnp.where(lax.broadcasted_iota(jnp.int32, (V_ROWS - DH_B, T), 0) == 0, 1.0, 0.0)

    @pl.when(i == 0)
    def _():
        kmean_ref[...] = jnp.zeros_like(kmean_ref)

    for h in range(NH_B):
        sl = slice(h * DH_B, (h + 1) * DH_B)
        qn = _rms(q_ref[:, sl].astype(F32), gq_ref[...])
        qst_ref[h] = (qn * scale).T.astype(qst_ref.dtype)
        kn = _rms(k_ref[:, sl].astype(F32), gk_ref[...])
        kmean_ref[:, sl] = jnp.where(row_is_blk, jnp.mean(kn, axis=0, keepdims=True), kmean_ref[:, sl])
        slope = _alibi_slope(h)
        extra = jnp.where(lane < n_blk + 3 * N_SPLIT, 1.0, 0.0)
        extra = jnp.where(lane < n_blk + 2 * N_SPLIT, slope * pos, extra)
        extra = jnp.where(lane < n_blk + N_SPLIT, slope * T * blk_f, extra)
        extra = jnp.where(lane < n_blk, jnp.where(lane == i, 1.0, 0.0), extra)
        kaug_ref[h, :, :DH_B] = kn.astype(kaug_ref.dtype)
        kaug_ref[h, :, DH_B:] = extra.astype(kaug_ref.dtype)
        vt = jnp.concatenate([v_ref[:, sl].astype(F32).T, ones_rows], axis=0)
        vt_ref[h] = vt.astype(vt_ref.dtype)


def _moba_prep(proj, gq, gk):
    S = proj.shape[0]
    T = BLOCK_B
    n_blk = S // T
    assert n_blk + 3 * N_SPLIT <= LANE
    kern = functools.partial(_moba_prep_kernel, n_blk=n_blk)
    return pl.pallas_call(
        kern,
        grid=(n_blk,),
        in_specs=[
            pl.BlockSpec((T, W_B), lambda i: (i, C_QB // W_B)),
            pl.BlockSpec((T, W_B), lambda i: (i, C_KB // W_B)),
            pl.BlockSpec((T, W_B), lambda i: (i, C_VB // W_B)),
            pl.BlockSpec((1, DH_B), lambda i: (0, 0)),
            pl.BlockSpec((1, DH_B), lambda i: (0, 0)),
        ],
        out_specs=[
            pl.BlockSpec((NH_B, DH_B, T), lambda i: (0, 0, i)),
            pl.BlockSpec((NH_B, T, DH_B + LANE), lambda i: (0, i, 0)),
            pl.BlockSpec((NH_B, None, V_ROWS, T), lambda i: (0, i, 0, 0)),
            pl.BlockSpec((LANE, W_B), lambda i: (0, 0)),
        ],
        out_shape=[
            jax.ShapeDtypeStruct((NH_B, DH_B, S), BF16),
            jax.ShapeDtypeStruct((NH_B, S, DH_B + LANE), BF16),
            jax.ShapeDtypeStruct((NH_B, n_blk, V_ROWS, T), BF16),
            jax.ShapeDtypeStruct((LANE, W_B), F32),
        ],
        compiler_params=_cparams(("arbitrary",)),
        name="moba_prep",
    )(proj, proj, proj, gq, gk)


CHUNK_B = 4


UNDERFLOW_LOG2 = 160.0
BOUND_SLACK = 1.02


def _first_live_chunk(gq_ref, gk_ref, own_max_ref, h, sb, chunk_len):
    gq_max = jnp.max(jnp.abs(gq_ref[...]), keepdims=True)
    gk_max = jnp.max(jnp.abs(gk_ref[...]), keepdims=True)
    bound = BOUND_SLACK * (DH_B ** 0.5) * gq_max * gk_max + 1.0
    slope = jnp.exp2(jnp.full((1, 1), -8.0 / NH_B, F32) * (h + 1).astype(F32))
    own_max = own_max_ref[...]
    offs = jnp.bitwise_and(lax.broadcasted_iota(jnp.int32, own_max.shape, 1), BLOCK_B - 1).astype(F32)
    lowest_max = jnp.min(own_max - (LOG2E * slope) * offs, keepdims=True)
    reach = bound + (UNDERFLOW_LOG2 - lowest_max) / LOG2E
    back = jnp.floor((reach / slope - 1.0) / chunk_len)
    back = jnp.minimum(back, sb.astype(F32))
    first = jnp.maximum(sb.astype(F32) - 1.0 - back, 0.0)
    return jnp.max(first).astype(jnp.int32)


def _branches_kernel(qst0_ref, qstn_ref, kaug_ref, vt_ref, kmean_ref, gq_ref, gk_ref,
                     qa_ref, ka_ref, va_ref, oa_ref, if_ref, gain_ref,
                     o_ref, ha_ref,
                     qaug_ref, s_ref, cmax_ref, m_ref, acc_ref, c_ref, ma_ref, *, n_blk):
    h = pl.program_id(0)
    sb = pl.program_id(1)
    n_sb = n_blk // CHUNK_B
    T = BLOCK_B
    W = CHUNK_B * T
    assert T & (T - 1) == 0
    cur = sb % 2
    key = lax.broadcasted_iota(jnp.int32, (T, W), 0)
    tok = lax.broadcasted_iota(jnp.int32, (T, W), 1)

    def write_qaug(slot, qst, sblock):
        gate = jnp.dot(kmean_ref[:n_blk, :].astype(BF16), qst, preferred_element_type=F32)
        blk = lax.broadcasted_iota(jnp.int32, (n_blk, W), 0)
        tok_blk = jnp.right_shift(lax.broadcasted_iota(jnp.int32, (1, W), 1), T.bit_length() - 1)
        own = sblock * CHUNK_B + tok_blk
        valid = blk < own
        g = jnp.where(valid, gate, -jnp.inf)
        sel = jnp.zeros((n_blk, W), F32)
        for _ in range(TOPK_B):
            mx = jnp.max(g, axis=0, keepdims=True)
            idx = jnp.min(jnp.where(g == mx, blk, n_blk), axis=0, keepdims=True)
            pick = blk == idx
            sel = jnp.where(pick, 1.0, sel)
            g = jnp.where(pick, -jnp.inf, g)
        keep = jnp.where(valid, sel, 0.0) + jnp.where(blk == own, 1.0, 0.0)
        selbias = jnp.where(keep > 0.0, 0.0, NEG_BIG)

        slope = jnp.exp2(jnp.full((1, W), -8.0 / NH_B, F32) * (h + 1).astype(F32))
        v3 = -slope * (T * own).astype(F32) * LOG2E
        v3_hi = v3.astype(BF16).astype(F32)
        v3_r = v3 - v3_hi
        v3_lo = v3_r.astype(BF16).astype(F32)
        v3_terms = (v3_hi, v3_lo, v3_r - v3_lo)
        c_terms = _bf16_split_const(LOG2E)
        n_tail = LANE - n_blk
        r = lax.broadcasted_iota(jnp.int32, (n_tail, W), 0)
        tail = jnp.zeros((n_tail, W), F32)
        for t in range(N_SPLIT):
            tail = jnp.where(r == t, c_terms[t], tail)
            tail = jnp.where(r == N_SPLIT + t, c_terms[t], tail)
            tail = jnp.where(r == 2 * N_SPLIT + t, v3_terms[t], tail)
        qaug_ref[slot, :DH_B, :] = qst
        qaug_ref[slot, DH_B:DH_B + n_blk, :] = selbias.astype(BF16)
        qaug_ref[slot, DH_B + n_blk:, :] = tail.astype(BF16)

    def block_scores(slot, cid, b):
        start = pl.multiple_of(cid * W + b * T, T)
        return jnp.dot(kaug_ref[pl.ds(start, T), :], qaug_ref[slot],
                       preferred_element_type=F32)

    def own_block_scores(slot, sblock, b):
        return jnp.where(key + b * T > tok, NEG_BIG, block_scores(slot, sblock, b))

    @pl.when(sb == 0)
    def _():
        write_qaug(0, qst0_ref[...], 0)
        cmax = None
        for b in range(CHUNK_B):
            sblk = own_block_scores(0, 0, b)
            s_ref[b * T:(b + 1) * T, :] = sblk
            mb = jnp.max(sblk, axis=0, keepdims=True)
            cmax = mb if cmax is None else jnp.maximum(cmax, mb)
        cmax_ref[...] = cmax

    m_ref[...] = jnp.full(m_ref.shape, NEG_BIG, F32)
    acc_ref[...] = jnp.zeros_like(acc_ref)

    def step(cid, refill, after_block=None):
        m_prev = m_ref[...]
        m_new = jnp.maximum(m_prev, cmax_ref[...])
        alpha = jnp.exp2(m_prev - m_new)
        pv = jnp.zeros((V_ROWS, W), F32)
        cmax = None
        for b in range(CHUNK_B):
            rows = slice(b * T, (b + 1) * T)
            p = jnp.exp2(s_ref[rows, :] - m_new)
            pv = pv + jnp.dot(vt_ref[cid * CHUNK_B + b], p.astype(BF16), preferred_element_type=F32)
            nb = refill(b)
            s_ref[rows, :] = nb
            mb = jnp.max(nb, axis=0, keepdims=True)
            cmax = mb if cmax is None else jnp.maximum(cmax, mb)
            if after_block is not None:
                after_block()
        acc_ref[...] = alpha * acc_ref[...] + pv
        m_ref[...] = m_new
        cmax_ref[...] = cmax

    first = _first_live_chunk(gq_ref, gk_ref, cmax_ref, h, sb, W)

    def body(n, carry):
        step(jnp.where(n == first, sb, n - 1), lambda b: block_scores(cur, n, b))
        return carry

    lax.fori_loop(first, sb, body, 0)

    mlstm_heads = _mlstm_step((h == 0) & (sb == 0), qa_ref, ka_ref, va_ref, oa_ref, if_ref, gain_ref,
                              ha_ref, c_ref, ma_ref)
    nxt = jnp.minimum(sb + 1, n_sb - 1)
    write_qaug(1 - cur, qstn_ref[...], nxt)

    def run_mlstm_head():
        if mlstm_heads:
            mlstm_heads.pop(0)()

    step(jnp.where(sb == 0, sb, sb - 1), lambda b: own_block_scores(1 - cur, nxt, b), run_mlstm_head)
    while mlstm_heads:
        mlstm_heads.pop(0)()
    acc = acc_ref[...]
    o_ref[...] = (acc[:DH_B, :] / acc[DH_B:DH_B + 1, :]).T.astype(o_ref.dtype)


def _branches(qst, kaug, vt, kmean, gq, gk, proj, ifp, gain_h):
    S = qst.shape[2]
    T = BLOCK_B
    W = CHUNK_B * T
    L = CHUNK_A
    n_blk = S // T
    n_sb = n_blk // CHUNK_B
    assert n_blk % CHUNK_B == 0 and NH_B * n_sb * L == S
    kern = functools.partial(_branches_kernel, n_blk=n_blk)

    def chunk(col):
        return lambda h, s: (h * n_sb + s, col)

    return pl.pallas_call(
        kern,
        grid=(NH_B, n_sb),
        in_specs=[
            pl.BlockSpec((None, DH_B, W), lambda h, s: (h, 0, 0)),
            pl.BlockSpec((None, DH_B, W), lambda h, s: (h, 0, jnp.minimum(s + 1, n_sb - 1))),
            pl.BlockSpec((None, S, DH_B + LANE), lambda h, s: (h, 0, 0)),
            pl.BlockSpec((None, n_blk, V_ROWS, T), lambda h, s: (h, 0, 0, 0)),
            pl.BlockSpec((LANE, DH_B), lambda h, s: (0, h)),
            pl.BlockSpec((1, DH_B), lambda h, s: (0, 0)),
            pl.BlockSpec((1, DH_B), lambda h, s: (0, 0)),
            pl.BlockSpec((L, QK_A), chunk(C_QA // QK_A)),
            pl.BlockSpec((L, QK_A), chunk(C_KA // QK_A)),
            pl.BlockSpec((L, W_A), chunk(C_VA // W_A)),
            pl.BlockSpec((L, W_A), chunk(C_OA // W_A)),
            pl.BlockSpec((L, IF_PAD), chunk(0)),
            pl.BlockSpec((1, W_A), lambda h, s: (0, 0)),
        ],
        out_specs=[
            pl.BlockSpec((W, DH_B), lambda h, s: (s, h)),
            pl.BlockSpec((L, W_A), chunk(0)),
        ],
        out_shape=[
            jax.ShapeDtypeStruct((S, W_B), BF16),
            jax.ShapeDtypeStruct((S, W_A), BF16),
        ],
        scratch_shapes=[
            pltpu.VMEM((2, DH_B + LANE, W), BF16),
            pltpu.VMEM((W, W), F32),
            pltpu.VMEM((1, W), F32),
            pltpu.VMEM((1, W), F32),
            pltpu.VMEM((V_ROWS, W), F32),
            pltpu.VMEM((NH_A, DQK_A, DV_A + LANE), F32),
            pltpu.VMEM((SUBLANE, LANE), F32),
        ],
        compiler_params=_cparams(("arbitrary", "arbitrary")),
        name="branches",
    )(qst, qst, kaug, vt, kmean, gq, gk, proj, proj, proj, proj, ifp, gain_h)


def _merge_out_kernel(x_ref, ha_ref, hb_ref, ga_ref, gb_ref, wa_ref, wb_ref, wo_ref, o_ref):
    ta = jnp.dot(ha_ref[...], wa_ref[...], preferred_element_type=F32)
    tb = jnp.dot(hb_ref[...], wb_ref[...], preferred_element_type=F32)
    merged = (jax.nn.sigmoid(ga_ref[...].astype(F32)) * ta
              + jax.nn.sigmoid(gb_ref[...].astype(F32)) * tb).astype(BF16)
    o_ref[...] = x_ref[...] + jnp.dot(merged, wo_ref[...], preferred_element_type=F32)


def _merge_out(x, h_a, h_b, proj, w_a, w_b, w_o, layer, tm):
    S, D = x.shape
    const = dict(pipeline_mode=pl.Buffered(1))
    return pl.pallas_call(
        _merge_out_kernel,
        grid=(S // tm,),
        in_specs=[
            pl.BlockSpec((tm, D), lambda m: (m, 0)),
            pl.BlockSpec((tm, W_A), lambda m: (m, 0)),
            pl.BlockSpec((tm, W_B), lambda m: (m, 0)),
            pl.BlockSpec((tm, D), lambda m: (m, C_G // D)),
            pl.BlockSpec((tm, D), lambda m: (m, C_G // D + 1)),
            pl.BlockSpec((None, W_A, D), lambda m: (layer, 0, 0), **const),
            pl.BlockSpec((None, W_B, D), lambda m: (layer, 0, 0), **const),
            pl.BlockSpec((None, D, D), lambda m: (layer, 0, 0), **const),
        ],
        out_specs=pl.BlockSpec((tm, D), lambda m: (m, 0)),
        out_shape=jax.ShapeDtypeStruct((S, D), F32),
        compiler_params=_cparams(("parallel",)),
        name="merge_out",
    )(x, h_a, h_b, proj, proj, w_a, w_b, w_o)


def _mlp_kernel(x_ref, g_ref, wu_ref, wd_ref, o_ref, hn_ref):
    @pl.when(pl.program_id(1) == 0)
    def _():
        x = x_ref[...]
        hn_ref[...] = _rms(x, g_ref[...]).astype(BF16)
        o_ref[...] = x

    u = jnp.dot(hn_ref[...], wu_ref[...], preferred_element_type=F32)
    a = jnp.square(jnp.maximum(u, 0.0)).astype(BF16)
    o_ref[...] += jnp.dot(a, wd_ref[...], preferred_element_type=F32)


def _mlp(x, gain, w_up, w_down, layer, tm, tf):
    S, D = x.shape
    FF = w_up.shape[2]
    return pl.pallas_call(
        _mlp_kernel,
        grid=(S // tm, FF // tf),
        in_specs=[
            pl.BlockSpec((tm, D), lambda m, f: (m, 0)),
            pl.BlockSpec((1, D), lambda m, f: (0, 0)),
            pl.BlockSpec((None, D, tf), lambda m, f: (layer, 0, f)),
            pl.BlockSpec((None, tf, D), lambda m, f: (layer, f, 0)),
        ],
        out_specs=pl.BlockSpec((tm, D), lambda m, f: (m, 0)),
        out_shape=jax.ShapeDtypeStruct((S, D), F32),
        scratch_shapes=[pltpu.VMEM((tm, D), BF16)],
        compiler_params=_cparams(("parallel", "arbitrary"), BIG_TILE_VMEM_LIMIT),
        name="mlp",
    )(x, gain, w_up, w_down)


IN_PROJ_TM, IN_PROJ_TN = 1024, 2560
MERGE_TM = 512
MLP_TM, MLP_TF = 512, 2048
BIG_TILE_VMEM_LIMIT = 60 * 1024 * 1024


def _tile(n, pref):
    t = min(n, pref)
    while n % t:
        t -= LANE
    assert t > 0
    return t


def kernel(x, norm_mix, w_in, b_if, norm_h_mlstm, norm_q_moba, norm_k_moba,
           w_branch_a, w_branch_b, w_out, norm_mlp, w_up, w_down):
    B, S, D = x.shape
    assert B == 1 and S % BLOCK_B == 0 and D % LANE == 0
    depth = w_in.shape[0]
    c_if = C_QB
    n_main = w_in.shape[2] - N_IF
    assert n_main == C_G + 2 * D

    w_main, w_if = _w_repack(jnp.swapaxes(w_in, 1, 2), c_if, _tile(D, 256))
    w_a, w_b, w_o = w_branch_a.astype(BF16), w_branch_b.astype(BF16), w_out.astype(BF16)
    w_u, w_d = w_up.astype(BF16), w_down.astype(BF16)

    xs = x.reshape(S, D)
    for l in range(depth):
        bias_if = jnp.pad(b_if[l].astype(F32), (0, IF_PAD - N_IF)).reshape(1, IF_PAD)
        proj, ifp = _in_proj(xs, norm_mix[l].reshape(1, D), w_main, w_if, bias_if, l,
                             _tile(S, IN_PROJ_TM), _tile(n_main, IN_PROJ_TN))
        gq, gk = norm_q_moba[l].reshape(1, DH_B), norm_k_moba[l].reshape(1, DH_B)
        qst, kaug, vt, kmean = _moba_prep(proj, gq, gk)
        h_b, h_a = _branches(qst, kaug, vt, kmean, gq, gk, proj, ifp, norm_h_mlstm[l].reshape(1, W_A))
        xs = _merge_out(xs, h_a, h_b, proj, w_a, w_b, w_o, l, _tile(S, MERGE_TM))
        xs = _mlp(xs, norm_mlp[l].reshape(1, D), w_u, w_d, l, _tile(S, MLP_TM), _tile(w_up.shape[2], MLP_TF))
    return xs.reshape(B, S, D)
```

```python
import functools

import jax
import jax.numpy as jnp
import numpy as np
from jax import lax
from jax.experimental import pallas as pl
from jax.experimental.pallas import tpu as pltpu

F32 = jnp.float32
BF16 = jnp.bfloat16

NH_A, DQK_A, DV_A, CHUNK_A = 4, 128, 256, 128
GATE_SOFTCAP = 15.0
NH_B, DH_B, BLOCK_B, TOPK_B = 8, 128, 256, 3
EPS = 1e-6

W_A = NH_A * DV_A
W_B = NH_B * DH_B
QK_A = NH_A * DQK_A
N_IF = 2 * NH_A
LANE = 128
SUBLANE = 8
BF16_SUBLANES = 2 * SUBLANE
IF_PAD = LANE
NEG_BIG = -1e30
VMEM_LIMIT = 56 * 1024 * 1024

C_QA, C_KA, C_VA, C_OA = 0, QK_A, 2 * QK_A, 2 * QK_A + W_A
C_QB = C_OA + W_A
C_KB = C_QB + W_B
C_VB = C_KB + W_B
C_G = C_VB + W_B


def _cparams(sem, vmem_limit=VMEM_LIMIT):
    return pltpu.CompilerParams(dimension_semantics=sem, vmem_limit_bytes=vmem_limit)


def _rms(x, gain):
    ms = jnp.mean(x * x, axis=-1, keepdims=True)
    return x * lax.rsqrt(ms + EPS) * gain


def _w_repack_kernel(wt_ref, main_ref, if_ref, *, c_if):
    n_cols = wt_ref.shape[0]
    main_ref[:c_if, :] = wt_ref[:c_if, :].astype(main_ref.dtype)
    main_ref[c_if:, :] = wt_ref[c_if + N_IF:n_cols, :].astype(main_ref.dtype)
    row = lax.broadcasted_iota(jnp.int32, if_ref.shape, 0)
    if_ref[...] = jnp.where(row < N_IF, wt_ref[c_if:c_if + IF_PAD, :], 0.0).astype(if_ref.dtype)


def _w_repack(w_in_t, c_if, td):
    depth, n_cols, D = w_in_t.shape
    n_main = n_cols - N_IF
    return pl.pallas_call(
        functools.partial(_w_repack_kernel, c_if=c_if),
        grid=(depth, D // td),
        in_specs=[pl.BlockSpec((None, n_cols, td), lambda l, d: (l, 0, d))],
        out_specs=[
            pl.BlockSpec((None, n_main, td), lambda l, d: (l, 0, d)),
            pl.BlockSpec((None, IF_PAD, td), lambda l, d: (l, 0, d)),
        ],
        out_shape=[
            jax.ShapeDtypeStruct((depth, n_main, D), BF16),
            jax.ShapeDtypeStruct((depth, IF_PAD, D), BF16),
        ],
        compiler_params=_cparams(("parallel", "parallel")),
        name="w_repack",
    )(w_in_t)


_NT = (((1,), (1,)), ((), ()))


def _in_proj_kernel(x_ref, g_ref, w_ref, wif_ref, bif_ref, o_ref, oif_ref, xn_ref):
    @pl.when(pl.program_id(1) == 0)
    def _():
        xn = _rms(x_ref[...], g_ref[...]).astype(BF16)
        xn_ref[...] = xn
        oif_ref[...] = lax.dot_general(xn, wif_ref[...], _NT, preferred_element_type=F32) + bif_ref[...]

    o_ref[...] = lax.dot_general(xn_ref[...], w_ref[...], _NT,
                                 preferred_element_type=F32).astype(o_ref.dtype)


def _in_proj(x, gain, w_main_t, w_if_t, b_if, layer, tm, tn):
    S, D = x.shape
    N = w_main_t.shape[1]
    return pl.pallas_call(
        _in_proj_kernel,
        grid=(S // tm, N // tn),
        in_specs=[
            pl.BlockSpec((tm, D), lambda m, n: (m, 0)),
            pl.BlockSpec((1, D), lambda m, n: (0, 0)),
            pl.BlockSpec((None, tn, D), lambda m, n: (layer, n, 0)),
            pl.BlockSpec((None, IF_PAD, D), lambda m, n: (layer, 0, 0)),
            pl.BlockSpec((1, IF_PAD), lambda m, n: (0, 0)),
        ],
        out_specs=[
            pl.BlockSpec((tm, tn), lambda m, n: (m, n)),
            pl.BlockSpec((tm, IF_PAD), lambda m, n: (m, 0)),
        ],
        out_shape=[
            jax.ShapeDtypeStruct((S, N), BF16),
            jax.ShapeDtypeStruct((S, IF_PAD), F32),
        ],
        scratch_shapes=[pltpu.VMEM((tm, D), BF16)],
        compiler_params=_cparams(("parallel", "arbitrary"), BIG_TILE_VMEM_LIMIT),
        name="in_proj",
    )(x, gain, w_main_t, w_if_t, b_if)


def _mlstm_step(first, q_ref, k_ref, v_ref, oa_ref, if_ref, gain_ref, o_ref, c_ref, m_ref):
    L = CHUNK_A

    @pl.when(first)
    def _():
        c_ref[...] = jnp.zeros_like(c_ref)
        m_ref[...] = jnp.zeros_like(m_ref)

    g = if_ref[...]
    gc = GATE_SOFTCAP * jnp.tanh(g / GATE_SOFTCAP)
    col = lax.broadcasted_iota(jnp.int32, (L, LANE), 1)
    row = lax.broadcasted_iota(jnp.int32, (L, LANE), 0)
    log_f = jnp.minimum(gc, 0.0) - jnp.log1p(jnp.exp(-jnp.abs(gc)))
    G = jnp.where(col < NH_A, gc, log_f)
    tril = (row >= col).astype(F32)
    Bc = jnp.dot(tril, G, preferred_element_type=F32, precision=lax.Precision.HIGHEST)
    RT = (G - pltpu.roll(Bc, LANE - NH_A, 1)).T
    causal = row >= col
    scale = DQK_A ** -0.5
    ones_col = jnp.where(col == 0, 1.0, 0.0).astype(BF16)

    def head(h):
        qh = q_ref[:, h * DQK_A:(h + 1) * DQK_A]
        kh = k_ref[:, h * DQK_A:(h + 1) * DQK_A]
        vh = v_ref[:, h * DV_A:(h + 1) * DV_A]
        vext = jnp.concatenate([vh, ones_col], axis=1)
        b_col = Bc[:, NH_A + h:NH_A + h + 1]
        b_last = Bc[L - 1:L, NH_A + h:NH_A + h + 1]
        ig_col = G[:, h:h + 1]
        m_prev = m_ref[h:h + 1, 0:1]
        c_prev = c_ref[h]

        log_d = jnp.where(causal, b_col + RT[h:h + 1, :], -jnp.inf)
        m_inter = b_col + m_prev
        m_row = jnp.maximum(m_inter, jnp.max(log_d, axis=1, keepdims=True))
        dmat = jnp.exp(log_d - m_row)
        s = lax.dot_general(qh, kh, (((1,), (1,)), ((), ())), preferred_element_type=F32)
        sd = (s * (dmat * scale)).astype(BF16)
        inter = jnp.exp(m_inter - m_row) * scale
        num_ext = (jnp.dot(sd, vext, preferred_element_type=F32)
                   + inter * jnp.dot(qh, c_prev.astype(BF16), preferred_element_type=F32))
        num = num_ext[:, :DV_A]
        den = num_ext[:, DV_A:DV_A + 1]
        hval = num / jnp.maximum(jnp.abs(den), jnp.exp(-m_row))

        hn = _rms(hval, gain_ref[:, h * DV_A:(h + 1) * DV_A])
        og = jax.nn.sigmoid(oa_ref[:, h * DV_A:(h + 1) * DV_A].astype(F32))
        o_ref[:, h * DV_A:(h + 1) * DV_A] = (hn * og).astype(o_ref.dtype)

        log_w = b_last - b_col + ig_col
        m_new = jnp.maximum(b_last + m_prev, jnp.max(log_w, axis=0, keepdims=True))
        w = jnp.exp(log_w - m_new)
        decay = jnp.exp(b_last + m_prev - m_new)
        wv = (w * vext.astype(F32)).astype(BF16)
        c_ref[h] = decay * c_prev + lax.dot_general(
            kh, wv, (((0,), (0,)), ((), ())), preferred_element_type=F32)
        m_ref[h:h + 1, :] = jnp.broadcast_to(m_new, (1, LANE))

    return [functools.partial(head, h) for h in range(NH_A)]


LOG2E = 1.4426950408889634
N_SPLIT = 3
V_ROWS = DH_B + BF16_SUBLANES


def _alibi_slope(h):
    return 2.0 ** (-8.0 * (h + 1) / NH_B)


def _bf16_split_const(x):
    terms, rem = [], float(np.float32(x))
    for _ in range(N_SPLIT):
        t = float(np.float32(rem).astype(BF16))
        terms.append(t)
        rem -= t
    return terms


def _moba_prep_kernel(q_ref, k_ref, v_ref, gq_ref, gk_ref, qst_ref, kaug_ref, vt_ref, kmean_ref, *, n_blk):
    i = pl.program_id(0)
    T = BLOCK_B
    scale = DH_B ** -0.5 * LOG2E
    lane = lax.broadcasted_iota(jnp.int32, (T, LANE), 1)
    pos = lax.broadcasted_iota(jnp.int32, (T, LANE), 0).astype(F32)
    blk_f = i.astype(F32)
    row_is_blk = lax.broadcasted_iota(jnp.int32, (LANE, DH_B), 0) == i
    ones_rows = jnp.where(lax.broadcasted_iota(jnp.int32, (V_ROWS - DH_B, T), 0) == 0, 1.0, 0.0)

    @pl.when(i == 0)
    def _():
        kmean_ref[...] = jnp.zeros_like(kmean_ref)

    for h in range(NH_B):
        sl = slice(h * DH_B, (h + 1) * DH_B)
        qn = _rms(q_ref[:, sl].astype(F32), gq_ref[...])
        qst_ref[h] = (qn * scale).T.astype(qst_ref.dtype)
        kn = _rms(k_ref[:, sl].astype(F32), gk_ref[...])
        kmean_ref[:, sl] = jnp.where(row_is_blk, jnp.mean(kn, axis=0, keepdims=True), kmean_ref[:, sl])
        slope = _alibi_slope(h)
        extra = jnp.where(lane < n_blk + 3 * N_SPLIT, 1.0, 0.0)
        extra = jnp.where(lane < n_blk + 2 * N_SPLIT, slope * pos, extra)
        extra = jnp.where(lane < n_blk + N_SPLIT, slope * T * blk_f, extra)
        extra = jnp.where(lane < n_blk, jnp.where(lane == i, 1.0, 0.0), extra)
        kaug_ref[h, :, :DH_B] = kn.astype(kaug_ref.dtype)
        kaug_ref[h, :, DH_B:] = extra.astype(kaug_ref.dtype)
        vt = jnp.concatenate([v_ref[:, sl].astype(F32).T, ones_rows], axis=0)
        vt_ref[h] = vt.astype(vt_ref.dtype)


def _moba_prep(proj, gq, gk):
    S = proj.shape[0]
    T = BLOCK_B
    n_blk = S // T
    assert n_blk + 3 * N_SPLIT <= LANE
    kern = functools.partial(_moba_prep_kernel, n_blk=n_blk)
    return pl.pallas_call(
        kern,
        grid=(n_blk,),
        in_specs=[
            pl.BlockSpec((T, W_B), lambda i: (i, C_QB // W_B)),
            pl.BlockSpec((T, W_B), lambda i: (i, C_KB // W_B)),
            pl.BlockSpec((T, W_B), lambda i: (i, C_VB // W_B)),
            pl.BlockSpec((1, DH_B), lambda i: (0, 0)),
            pl.BlockSpec((1, DH_B), lambda i: (0, 0)),
        ],
        out_specs=[
            pl.BlockSpec((NH_B, DH_B, T), lambda i: (0, 0, i)),
            pl.BlockSpec((NH_B, T, DH_B + LANE), lambda i: (0, i, 0)),
            pl.BlockSpec((NH_B, None, V_ROWS, T), lambda i: (0, i, 0, 0)),
            pl.BlockSpec((LANE, W_B), lambda i: (0, 0)),
        ],
        out_shape=[
            jax.ShapeDtypeStruct((NH_B, DH_B, S), BF16),
            jax.ShapeDtypeStruct((NH_B, S, DH_B + LANE), BF16),
            jax.ShapeDtypeStruct((NH_B, n_blk, V_ROWS, T), BF16),
            jax.ShapeDtypeStruct((LANE, W_B), F32),
        ],
        compiler_params=_cparams(("arbitrary",)),
        name="moba_prep",
    )(proj, proj, proj, gq, gk)


CHUNK_B = 4
TOKEN_SPLIT = 4


UNDERFLOW_LOG2 = 160.0
BOUND_SLACK = 1.02


def _first_live_chunk(gq_ref, gk_ref, own_max_ref, h, sb, chunk_len):
    gq_max = jnp.max(jnp.abs(gq_ref[...]), keepdims=True)
    gk_max = jnp.max(jnp.abs(gk_ref[...]), keepdims=True)
    bound = BOUND_SLACK * (DH_B ** 0.5) * gq_max * gk_max + 1.0
    slope = jnp.exp2(jnp.full((1, 1), -8.0 / NH_B, F32) * (h + 1).astype(F32))
    own_max = own_max_ref[...]
    offs = jnp.bitwise_and(lax.broadcasted_iota(jnp.int32, own_max.shape, 1), BLOCK_B - 1).astype(F32)
    lowest_max = jnp.min(own_max - (LOG2E * slope) * offs, keepdims=True)
    reach = bound + (UNDERFLOW_LOG2 - lowest_max) / LOG2E
    back = jnp.floor((reach / slope - 1.0) / chunk_len)
    back = jnp.minimum(back, sb.astype(F32))
    first = jnp.maximum(sb.astype(F32) - 1.0 - back, 0.0)
    return jnp.max(first).astype(jnp.int32)


def _branches_kernel(qst0_ref, qstn_ref, kaug_ref, vt_ref, kmean_ref, gq_ref, gk_ref,
                     qa_ref, ka_ref, va_ref, oa_ref, if_ref, gain_ref,
                     o_ref, ha_ref,
                     qaug_ref, s_ref, cmax_ref, m_ref, acc_ref, c_ref, ma_ref, *, n_blk):
    h = pl.program_id(0)
    sb = pl.program_id(1)
    n_sb = n_blk // CHUNK_B
    T = BLOCK_B
    W = CHUNK_B * T
    assert T & (T - 1) == 0
    cur = sb % 2

    def write_qaug(slot, qst, sblock):
        gate = jnp.dot(kmean_ref[:n_blk, :].astype(BF16), qst, preferred_element_type=F32)
        blk = lax.broadcasted_iota(jnp.int32, (n_blk, W), 0)
        tok_blk = jnp.right_shift(lax.broadcasted_iota(jnp.int32, (1, W), 1), T.bit_length() - 1)
        own = sblock * CHUNK_B + tok_blk
        valid = blk < own
        g = jnp.where(valid, gate, -jnp.inf)
        sel = jnp.zeros((n_blk, W), F32)
        for _ in range(TOPK_B):
            mx = jnp.max(g, axis=0, keepdims=True)
            idx = jnp.min(jnp.where(g == mx, blk, n_blk), axis=0, keepdims=True)
            pick = blk == idx
            sel = jnp.where(pick, 1.0, sel)
            g = jnp.where(pick, -jnp.inf, g)
        keep = jnp.where(valid, sel, 0.0) + jnp.where(blk == own, 1.0, 0.0)
        selbias = jnp.where(keep > 0.0, 0.0, NEG_BIG)

        slope = jnp.exp2(jnp.full((1, W), -8.0 / NH_B, F32) * (h + 1).astype(F32))
        v3 = -slope * (T * own).astype(F32) * LOG2E
        v3_hi = v3.astype(BF16).astype(F32)
        v3_r = v3 - v3_hi
        v3_lo = v3_r.astype(BF16).astype(F32)
        v3_terms = (v3_hi, v3_lo, v3_r - v3_lo)
        c_terms = _bf16_split_const(LOG2E)
        n_tail = LANE - n_blk
        r = lax.broadcasted_iota(jnp.int32, (n_tail, W), 0)
        tail = jnp.zeros((n_tail, W), F32)
        for t in range(N_SPLIT):
            tail = jnp.where(r == t, c_terms[t], tail)
            tail = jnp.where(r == N_SPLIT + t, c_terms[t], tail)
            tail = jnp.where(r == 2 * N_SPLIT + t, v3_terms[t], tail)
        qaug_ref[slot, :DH_B, :] = qst
        qaug_ref[slot, DH_B:DH_B + n_blk, :] = selbias.astype(BF16)
        qaug_ref[slot, DH_B + n_blk:, :] = tail.astype(BF16)

    col_ranges = [slice(i * (W // TOKEN_SPLIT), (i + 1) * (W // TOKEN_SPLIT)) for i in range(TOKEN_SPLIT)]

    def block_scores(slot, cid, b, cols):
        start = pl.multiple_of(cid * W + b * T, T)
        return jnp.dot(kaug_ref[pl.ds(start, T), :], qaug_ref[slot][:, cols],
                       preferred_element_type=F32)

    def own_block_scores(slot, sblock, b, cols):
        n_cols = cols.stop - cols.start
        key = lax.broadcasted_iota(jnp.int32, (T, n_cols), 0) + b * T
        tok = lax.broadcasted_iota(jnp.int32, (T, n_cols), 1) + cols.start
        return jnp.where(key > tok, NEG_BIG, block_scores(slot, sblock, b, cols))

    @pl.when(sb == 0)
    def _():
        write_qaug(0, qst0_ref[...], 0)
        for cols in col_ranges:
            cmax = None
            for b in range(CHUNK_B):
                sblk = own_block_scores(0, 0, b, cols)
                s_ref[b * T:(b + 1) * T, cols] = sblk
                mb = jnp.max(sblk, axis=0, keepdims=True)
                cmax = mb if cmax is None else jnp.maximum(cmax, mb)
            cmax_ref[:, cols] = cmax

    m_ref[...] = jnp.full(m_ref.shape, NEG_BIG, F32)
    acc_ref[...] = jnp.zeros_like(acc_ref)

    def step(cid, refill, after_block=None):
        m_prev = m_ref[...]
        m_new = jnp.maximum(m_prev, cmax_ref[...])
        alpha = jnp.exp2(m_prev - m_new)
        pv = [jnp.zeros((V_ROWS, c.stop - c.start), F32) for c in col_ranges]
        cmax = [None] * TOKEN_SPLIT
        for b in range(CHUNK_B):
            rows = slice(b * T, (b + 1) * T)
            for i, cols in enumerate(col_ranges):
                p = jnp.exp2(s_ref[rows, cols] - m_new[:, cols])
                pv[i] = pv[i] + jnp.dot(vt_ref[cid * CHUNK_B + b], p.astype(BF16), preferred_element_type=F32)
                nb = refill(b, cols)
                s_ref[rows, cols] = nb
                mb = jnp.max(nb, axis=0, keepdims=True)
                cmax[i] = mb if cmax[i] is None else jnp.maximum(cmax[i], mb)
            if after_block is not None:
                after_block()
        for i, cols in enumerate(col_ranges):
            acc_ref[:, cols] = alpha[:, cols] * acc_ref[:, cols] + pv[i]
            cmax_ref[:, cols] = cmax[i]
        m_ref[...] = m_new

    first = _first_live_chunk(gq_ref, gk_ref, cmax_ref, h, sb, W)

    def body(n, carry):
        step(jnp.where(n == first, sb, n - 1), lambda b, cols: block_scores(cur, n, b, cols))
        return carry

    lax.fori_loop(first, sb, body, 0)

    mlstm_heads = _mlstm_step((h == 0) & (sb == 0), qa_ref, ka_ref, va_ref, oa_ref, if_ref, gain_ref,
                              ha_ref, c_ref, ma_ref)
    nxt = jnp.minimum(sb + 1, n_sb - 1)
    write_qaug(1 - cur, qstn_ref[...], nxt)

    def run_mlstm_head():
        if mlstm_heads:
            mlstm_heads.pop(0)()

    step(jnp.where(sb == 0, sb, sb - 1), lambda b, cols: own_block_scores(1 - cur, nxt, b, cols), run_mlstm_head)
    while mlstm_heads:
        mlstm_heads.pop(0)()
    acc = acc_ref[...]
    o_ref[...] = (acc[:DH_B, :] / acc[DH_B:DH_B + 1, :]).T.astype(o_ref.dtype)


def _branches(qst, kaug, vt, kmean, gq, gk, proj, ifp, gain_h):
    S = qst.shape[2]
    T = BLOCK_B
    W = CHUNK_B * T
    L = CHUNK_A
    n_blk = S // T
    n_sb = n_blk // CHUNK_B
    assert n_blk % CHUNK_B == 0 and NH_B * n_sb * L == S
    kern = functools.partial(_branches_kernel, n_blk=n_blk)

    def chunk(col):
        return lambda h, s: (h * n_sb + s, col)

    return pl.pallas_call(
        kern,
        grid=(NH_B, n_sb),
        in_specs=[
            pl.BlockSpec((None, DH_B, W), lambda h, s: (h, 0, 0)),
            pl.BlockSpec((None, DH_B, W), lambda h, s: (h, 0, jnp.minimum(s + 1, n_sb - 1))),
            pl.BlockSpec((None, S, DH_B + LANE), lambda h, s: (h, 0, 0)),
            pl.BlockSpec((None, n_blk, V_ROWS, T), lambda h, s: (h, 0, 0, 0)),
            pl.BlockSpec((LANE, DH_B), lambda h, s: (0, h)),
            pl.BlockSpec((1, DH_B), lambda h, s: (0, 0)),
            pl.BlockSpec((1, DH_B), lambda h, s: (0, 0)),
            pl.BlockSpec((L, QK_A), chunk(C_QA // QK_A)),
            pl.BlockSpec((L, QK_A), chunk(C_KA // QK_A)),
            pl.BlockSpec((L, W_A), chunk(C_VA // W_A)),
            pl.BlockSpec((L, W_A), chunk(C_OA // W_A)),
            pl.BlockSpec((L, IF_PAD), chunk(0)),
            pl.BlockSpec((1, W_A), lambda h, s: (0, 0)),
        ],
        out_specs=[
            pl.BlockSpec((W, DH_B), lambda h, s: (s, h)),
            pl.BlockSpec((L, W_A), chunk(0)),
        ],
        out_shape=[
            jax.ShapeDtypeStruct((S, W_B), BF16),
            jax.ShapeDtypeStruct((S, W_A), BF16),
        ],
        scratch_shapes=[
            pltpu.VMEM((2, DH_B + LANE, W), BF16),
            pltpu.VMEM((W, W), F32),
            pltpu.VMEM((1, W), F32),
            pltpu.VMEM((1, W), F32),
            pltpu.VMEM((V_ROWS, W), F32),
            pltpu.VMEM((NH_A, DQK_A, DV_A + LANE), F32),
            pltpu.VMEM((SUBLANE, LANE), F32),
        ],
        compiler_params=_cparams(("arbitrary", "arbitrary")),
        name="branches",
    )(qst, qst, kaug, vt, kmean, gq, gk, proj, proj, proj, proj, ifp, gain_h)


def _merge_out_kernel(x_ref, ha_ref, hb_ref, ga_ref, gb_ref, wa_ref, wb_ref, wo_ref, o_ref):
    ta = jnp.dot(ha_ref[...], wa_ref[...], preferred_element_type=F32)
    tb = jnp.dot(hb_ref[...], wb_ref[...], preferred_element_type=F32)
    merged = (jax.nn.sigmoid(ga_ref[...].astype(F32)) * ta
              + jax.nn.sigmoid(gb_ref[...].astype(F32)) * tb).astype(BF16)
    o_ref[...] = x_ref[...] + jnp.dot(merged, wo_ref[...], preferred_element_type=F32)


def _merge_out(x, h_a, h_b, proj, w_a, w_b, w_o, layer, tm):
    S, D = x.shape
    const = dict(pipeline_mode=pl.Buffered(1))
    return pl.pallas_call(
        _merge_out_kernel,
        grid=(S // tm,),
        in_specs=[
            pl.BlockSpec((tm, D), lambda m: (m, 0)),
            pl.BlockSpec((tm, W_A), lambda m: (m, 0)),
            pl.BlockSpec((tm, W_B), lambda m: (m, 0)),
            pl.BlockSpec((tm, D), lambda m: (m, C_G // D)),
            pl.BlockSpec((tm, D), lambda m: (m, C_G // D + 1)),
            pl.BlockSpec((None, W_A, D), lambda m: (layer, 0, 0), **const),
            pl.BlockSpec((None, W_B, D), lambda m: (layer, 0, 0), **const),
            pl.BlockSpec((None, D, D), lambda m: (layer, 0, 0), **const),
        ],
        out_specs=pl.BlockSpec((tm, D), lambda m: (m, 0)),
        out_shape=jax.ShapeDtypeStruct((S, D), F32),
        compiler_params=_cparams(("parallel",)),
        name="merge_out",
    )(x, h_a, h_b, proj, proj, w_a, w_b, w_o)


def _mlp_kernel(x_ref, g_ref, wu_ref, wd_ref, o_ref, hn_ref):
    @pl.when(pl.program_id(1) == 0)
    def _():
        x = x_ref[...]
        hn_ref[...] = _rms(x, g_ref[...]).astype(BF16)
        o_ref[...] = x

    u = jnp.dot(hn_ref[...], wu_ref[...], preferred_element_type=F32)
    a = jnp.square(jnp.maximum(u, 0.0)).astype(BF16)
    o_ref[...] += jnp.dot(a, wd_ref[...], preferred_element_type=F32)


def _mlp(x, gain, w_up, w_down, layer, tm, tf):
    S, D = x.shape
    FF = w_up.shape[2]
    return pl.pallas_call(
        _mlp_kernel,
        grid=(S // tm, FF // tf),
        in_specs=[
            pl.BlockSpec((tm, D), lambda m, f: (m, 0)),
            pl.BlockSpec((1, D), lambda m, f: (0, 0)),
            pl.BlockSpec((None, D, tf), lambda m, f: (layer, 0, f)),
            pl.BlockSpec((None, tf, D), lambda m, f: (layer, f, 0)),
        ],
        out_specs=pl.BlockSpec((tm, D), lambda m, f: (m, 0)),
        out_shape=jax.ShapeDtypeStruct((S, D), F32),
        scratch_shapes=[pltpu.VMEM((tm, D), BF16)],
        compiler_params=_cparams(("parallel", "arbitrary"), BIG_TILE_VMEM_LIMIT),
        name="mlp",
    )(x, gain, w_up, w_down)


IN_PROJ_TM, IN_PROJ_TN = 1024, 2560
MERGE_TM = 512
MLP_TM, MLP_TF = 512, 2048
BIG_TILE_VMEM_LIMIT = 60 * 1024 * 1024


def _tile(n, pref):
    t = min(n, pref)
    while n % t:
        t -= LANE
    assert t > 0
    return t


def kernel(x, norm_mix, w_in, b_if, norm_h_mlstm, norm_q_moba, norm_k_moba,
           w_branch_a, w_branch_b, w_out, norm_mlp, w_up, w_down):
    B, S, D = x.shape
    assert B == 1 and S % BLOCK_B == 0 and D % LANE == 0
    depth = w_in.shape[0]
    c_if = C_QB
    n_main = w_in.shape[2] - N_IF
    assert n_main == C_G + 2 * D

    w_main, w_if = _w_repack(jnp.swapaxes(w_in, 1, 2), c_if, _tile(D, 256))
    w_a, w_b, w_o = w_branch_a.astype(BF16), w_branch_b.astype(BF16), w_out.astype(BF16)
    w_u, w_d = w_up.astype(BF16), w_down.astype(BF16)

    xs = x.reshape(S, D)
    for l in range(depth):
        bias_if = jnp.pad(b_if[l].astype(F32), (0, IF_PAD - N_IF)).reshape(1, IF_PAD)
        proj, ifp = _in_proj(xs, norm_mix[l].reshape(1, D), w_main, w_if, bias_if, l,
                             _tile(S, IN_PROJ_TM), _tile(n_main, IN_PROJ_TN))
        gq, gk = norm_q_moba[l].reshape(1, DH_B), norm_k_moba[l].reshape(1, DH_B)
        qst, kaug, vt, kmean = _moba_prep(proj, gq, gk)
        h_b, h_a = _branches(qst, kaug, vt, kmean, gq, gk, proj, ifp, norm_h_mlstm[l].reshape(1, W_A))
        xs = _merge_out(xs, h_a, h_b, proj, w_a, w_b, w_o, l, _tile(S, MERGE_TM))
        xs = _mlp(xs, norm_mlp[l].reshape(1, D), w_u, w_d, l, _tile(S, MLP_TM), _tile(w_up.shape[2], MLP_TF))
    return xs.reshape(B, S, D)
```

```python
import functools

import jax
import jax.numpy as jnp
import numpy as np
from jax import lax
from jax.experimental import pallas as pl
from jax.experimental.pallas import tpu as pltpu

F32 = jnp.float32
BF16 = jnp.bfloat16

NH_A, DQK_A, DV_A, CHUNK_A = 4, 128, 256, 128
GATE_SOFTCAP = 15.0
NH_B, DH_B, BLOCK_B, TOPK_B = 8, 128, 256, 3
EPS = 1e-6

W_A = NH_A * DV_A
W_B = NH_B * DH_B
QK_A = NH_A * DQK_A
N_IF = 2 * NH_A
LANE = 128
SUBLANE = 8
BF16_SUBLANES = 2 * SUBLANE
IF_PAD = LANE
NEG_BIG = -1e30
VMEM_LIMIT = 56 * 1024 * 1024

C_QA, C_KA, C_VA, C_OA = 0, QK_A, 2 * QK_A, 2 * QK_A + W_A
C_QB = C_OA + W_A
C_KB = C_QB + W_B
C_VB = C_KB + W_B
C_G = C_VB + W_B


def _cparams(sem, vmem_limit=VMEM_LIMIT):
    return pltpu.CompilerParams(dimension_semantics=sem, vmem_limit_bytes=vmem_limit)


def _rms(x, gain):
    ms = jnp.mean(x * x, axis=-1, keepdims=True)
    return x * lax.rsqrt(ms + EPS) * gain


def _w_repack_kernel(wt_ref, main_ref, if_ref, *, c_if):
    n_cols = wt_ref.shape[0]
    main_ref[:c_if, :] = wt_ref[:c_if, :].astype(main_ref.dtype)
    main_ref[c_if:, :] = wt_ref[c_if + N_IF:n_cols, :].astype(main_ref.dtype)
    row = lax.broadcasted_iota(jnp.int32, if_ref.shape, 0)
    if_ref[...] = jnp.where(row < N_IF, wt_ref[c_if:c_if + IF_PAD, :], 0.0).astype(if_ref.dtype)


def _w_repack(w_in_t, c_if, td):
    depth, n_cols, D = w_in_t.shape
    n_main = n_cols - N_IF
    return pl.pallas_call(
        functools.partial(_w_repack_kernel, c_if=c_if),
        grid=(depth, D // td),
        in_specs=[pl.BlockSpec((None, n_cols, td), lambda l, d: (l, 0, d))],
        out_specs=[
            pl.BlockSpec((None, n_main, td), lambda l, d: (l, 0, d)),
            pl.BlockSpec((None, IF_PAD, td), lambda l, d: (l, 0, d)),
        ],
        out_shape=[
            jax.ShapeDtypeStruct((depth, n_main, D), BF16),
            jax.ShapeDtypeStruct((depth, IF_PAD, D), BF16),
        ],
        compiler_params=_cparams(("parallel", "parallel")),
        name="w_repack",
    )(w_in_t)


_NT = (((1,), (1,)), ((), ()))


def _in_proj_kernel(x_ref, g_ref, w_ref, wif_ref, bif_ref, o_ref, oif_ref, xn_ref):
    @pl.when(pl.program_id(1) == 0)
    def _():
        xn = _rms(x_ref[...], g_ref[...]).astype(BF16)
        xn_ref[...] = xn
        oif_ref[...] = lax.dot_general(xn, wif_ref[...], _NT, preferred_element_type=F32) + bif_ref[...]

    o_ref[...] = lax.dot_general(xn_ref[...], w_ref[...], _NT,
                                 preferred_element_type=F32).astype(o_ref.dtype)


def _in_proj(x, gain, w_main_t, w_if_t, b_if, layer, tm, tn):
    S, D = x.shape
    N = w_main_t.shape[1]
    return pl.pallas_call(
        _in_proj_kernel,
        grid=(S // tm, N // tn),
        in_specs=[
            pl.BlockSpec((tm, D), lambda m, n: (m, 0)),
            pl.BlockSpec((1, D), lambda m, n: (0, 0)),
            pl.BlockSpec((None, tn, D), lambda m, n: (layer, n, 0)),
            pl.BlockSpec((None, IF_PAD, D), lambda m, n: (layer, 0, 0)),
            pl.BlockSpec((1, IF_PAD), lambda m, n: (0, 0)),
        ],
        out_specs=[
            pl.BlockSpec((tm, tn), lambda m, n: (m, n)),
            pl.BlockSpec((tm, IF_PAD), lambda m, n: (m, 0)),
        ],
        out_shape=[
            jax.ShapeDtypeStruct((S, N), BF16),
            jax.ShapeDtypeStruct((S, IF_PAD), F32),
        ],
        scratch_shapes=[pltpu.VMEM((tm, D), BF16)],
        compiler_params=_cparams(("parallel", "arbitrary"), BIG_TILE_VMEM_LIMIT),
        name="in_proj",
    )(x, gain, w_main_t, w_if_t, b_if)


def _mlstm_step(first, q_ref, k_ref, v_ref, oa_ref, if_ref, gain_ref, o_ref, c_ref, m_ref):
    L = CHUNK_A

    @pl.when(first)
    def _():
        c_ref[...] = jnp.zeros_like(c_ref)
        m_ref[...] = jnp.zeros_like(m_ref)

    g = if_ref[...]
    gc = GATE_SOFTCAP * jnp.tanh(g / GATE_SOFTCAP)
    col = lax.broadcasted_iota(jnp.int32, (L, LANE), 1)
    row = lax.broadcasted_iota(jnp.int32, (L, LANE), 0)
    log_f = jnp.minimum(gc, 0.0) - jnp.log1p(jnp.exp(-jnp.abs(gc)))
    G = jnp.where(col < NH_A, gc, log_f)
    tril = (row >= col).astype(F32)
    Bc = jnp.dot(tril, G, preferred_element_type=F32, precision=lax.Precision.HIGHEST)
    RT = (G - pltpu.roll(Bc, LANE - NH_A, 1)).T
    causal = row >= col
    scale = DQK_A ** -0.5
    ones_col = jnp.where(col == 0, 1.0, 0.0).astype(BF16)

    def head(h):
        qh = q_ref[:, h * DQK_A:(h + 1) * DQK_A]
        kh = k_ref[:, h * DQK_A:(h + 1) * DQK_A]
        vh = v_ref[:, h * DV_A:(h + 1) * DV_A]
        vext = jnp.concatenate([vh, ones_col], axis=1)
        b_col = Bc[:, NH_A + h:NH_A + h + 1]
        b_last = Bc[L - 1:L, NH_A + h:NH_A + h + 1]
        ig_col = G[:, h:h + 1]
        m_prev = m_ref[h:h + 1, 0:1]
        c_prev = c_ref[h]

        log_d = jnp.where(causal, b_col + RT[h:h + 1, :], -jnp.inf)
        m_inter = b_col + m_prev
        m_row = jnp.maximum(m_inter, jnp.max(log_d, axis=1, keepdims=True))
        dmat = jnp.exp(log_d - m_row)
        s = lax.dot_general(qh, kh, (((1,), (1,)), ((), ())), preferred_element_type=F32)
        sd = (s * (dmat * scale)).astype(BF16)
        inter = jnp.exp(m_inter - m_row) * scale
        num_ext = (jnp.dot(sd, vext, preferred_element_type=F32)
                   + inter * jnp.dot(qh, c_prev.astype(BF16), preferred_element_type=F32))
        num = num_ext[:, :DV_A]
        den = num_ext[:, DV_A:DV_A + 1]
        hval = num / jnp.maximum(jnp.abs(den), jnp.exp(-m_row))

        hn = _rms(hval, gain_ref[:, h * DV_A:(h + 1) * DV_A])
        og = jax.nn.sigmoid(oa_ref[:, h * DV_A:(h + 1) * DV_A].astype(F32))
        o_ref[:, h * DV_A:(h + 1) * DV_A] = (hn * og).astype(o_ref.dtype)

        log_w = b_last - b_col + ig_col
        m_new = jnp.maximum(b_last + m_prev, jnp.max(log_w, axis=0, keepdims=True))
        w = jnp.exp(log_w - m_new)
        decay = jnp.exp(b_last + m_prev - m_new)
        wv = (w * vext.astype(F32)).astype(BF16)
        c_ref[h] = decay * c_prev + lax.dot_general(
            kh, wv, (((0,), (0,)), ((), ())), preferred_element_type=F32)
        m_ref[h:h + 1, :] = jnp.broadcast_to(m_new, (1, LANE))

    return [functools.partial(head, h) for h in range(NH_A)]


LOG2E = 1.4426950408889634
N_SPLIT = 3
V_ROWS = DH_B + BF16_SUBLANES


def _alibi_slope(h):
    return 2.0 ** (-8.0 * (h + 1) / NH_B)


def _bf16_split_const(x):
    terms, rem = [], float(np.float32(x))
    for _ in range(N_SPLIT):
        t = float(np.float32(rem).astype(BF16))
        terms.append(t)
        rem -= t
    return terms


def _moba_prep_kernel(q_ref, k_ref, v_ref, gq_ref, gk_ref, qst_ref, kaug_ref, vt_ref, kmean_ref, *, n_blk):
    i = pl.program_id(0)
    T = BLOCK_B
    scale = DH_B ** -0.5 * LOG2E
    lane = lax.broadcasted_iota(jnp.int32, (T, LANE), 1)
    pos = lax.broadcasted_iota(jnp.int32, (T, LANE), 0).astype(F32)
    blk_f = i.astype(F32)
    row_is_blk = lax.broadcasted_iota(jnp.int32, (LANE, DH_B), 0) == i
    ones_rows = jnp.where(lax.broadcasted_iota(jnp.int32, (V_ROWS - DH_B, T), 0) == 0, 1.0, 0.0)

    @pl.when(i == 0)
    def _():
        kmean_ref[...] = jnp.zeros_like(kmean_ref)

    for h in range(NH_B):
        sl = slice(h * DH_B, (h + 1) * DH_B)
        qn = _rms(q_ref[:, sl].astype(F32), gq_ref[...])
        qst_ref[h] = (qn * scale).T.astype(qst_ref.dtype)
        kn = _rms(k_ref[:, sl].astype(F32), gk_ref[...])
        kmean_ref[:, sl] = jnp.where(row_is_blk, jnp.mean(kn, axis=0, keepdims=True), kmean_ref[:, sl])
        slope = _alibi_slope(h)
        extra = jnp.where(lane < n_blk + 3 * N_SPLIT, 1.0, 0.0)
        extra = jnp.where(lane < n_blk + 2 * N_SPLIT, slope * pos, extra)
        extra = jnp.where(lane < n_blk + N_SPLIT, slope * T * blk_f, extra)
        extra = jnp.where(lane < n_blk, jnp.where(lane == i, 1.0, 0.0), extra)
        kaug_ref[h, :, :DH_B] = kn.astype(kaug_ref.dtype)
        kaug_ref[h, :, DH_B:] = extra.astype(kaug_ref.dtype)
        vt = jnp.concatenate([v_ref[:, sl].astype(F32).T, ones_rows], axis=0)
        vt_ref[h] = vt.astype(vt_ref.dtype)


def _moba_prep(proj, gq, gk):
    S = proj.shape[0]
    T = BLOCK_B
    n_blk = S // T
    assert n_blk + 3 * N_SPLIT <= LANE
    kern = functools.partial(_moba_prep_kernel, n_blk=n_blk)
    return pl.pallas_call(
        kern,
        grid=(n_blk,),
        in_specs=[
            pl.BlockSpec((T, W_B), lambda i: (i, C_QB // W_B)),
            pl.BlockSpec((T, W_B), lambda i: (i, C_KB // W_B)),
            pl.BlockSpec((T, W_B), lambda i: (i, C_VB // W_B)),
            pl.BlockSpec((1, DH_B), lambda i: (0, 0)),
            pl.BlockSpec((1, DH_B), lambda i: (0, 0)),
        ],
        out_specs=[
            pl.BlockSpec((NH_B, DH_B, T), lambda i: (0, 0, i)),
            pl.BlockSpec((NH_B, T, DH_B + LANE), lambda i: (0, i, 0)),
            pl.BlockSpec((NH_B, None, V_ROWS, T), lambda i: (0, i, 0, 0)),
            pl.BlockSpec((LANE, W_B), lambda i: (0, 0)),
        ],
        out_shape=[
            jax.ShapeDtypeStruct((NH_B, DH_B, S), BF16),
            jax.ShapeDtypeStruct((NH_B, S, DH_B + LANE), BF16),
            jax.ShapeDtypeStruct((NH_B, n_blk, V_ROWS, T), BF16),
            jax.ShapeDtypeStruct((LANE, W_B), F32),
        ],
        compiler_params=_cparams(("arbitrary",)),
        name="moba_prep",
    )(proj, proj, proj, gq, gk)


CHUNK_B = 4
TOKEN_SPLIT = 4


UNDERFLOW_LOG2 = 160.0
BOUND_SLACK = 1.02


def _first_live_chunk(gq_ref, gk_ref, own_max_ref, h, sb, chunk_len):
    gq_max = jnp.max(jnp.abs(gq_ref[...]), keepdims=True)
    gk_max = jnp.max(jnp.abs(gk_ref[...]), keepdims=True)
    bound = BOUND_SLACK * (DH_B ** 0.5) * gq_max * gk_max + 1.0
    slope = jnp.exp2(jnp.full((1, 1), -8.0 / NH_B, F32) * (h + 1).astype(F32))
    own_max = own_max_ref[...]
    offs = jnp.bitwise_and(lax.broadcasted_iota(jnp.int32, own_max.shape, 1), BLOCK_B - 1).astype(F32)
    lowest_max = jnp.min(own_max - (LOG2E * slope) * offs, keepdims=True)
    reach = bound + (UNDERFLOW_LOG2 - lowest_max) / LOG2E
    back = jnp.floor((reach / slope - 1.0) / chunk_len)
    back = jnp.minimum(back, sb.astype(F32))
    first = jnp.maximum(sb.astype(F32) - 1.0 - back, 0.0)
    return jnp.max(first).astype(jnp.int32)


def _branches_kernel(qst0_ref, qstn_ref, kaug_ref, vt_ref, kmean_ref, gq_ref, gk_ref,
                     qa_ref, ka_ref, va_ref, oa_ref, if_ref, gain_ref,
                     o_ref, ha_ref,
                     qaug_ref, s_ref, cmax_ref, m_ref, acc_ref, c_ref, ma_ref, *, n_blk):
    h = pl.program_id(0)
    sb = pl.program_id(1)
    n_sb = n_blk // CHUNK_B
    T = BLOCK_B
    W = CHUNK_B * T
    assert T & (T - 1) == 0
    cur = sb % 2
    col_ranges = [slice(i * (W // TOKEN_SPLIT), (i + 1) * (W // TOKEN_SPLIT)) for i in range(TOKEN_SPLIT)]

    def write_qaug(slot, qst, sblock):
        kmean = kmean_ref[:n_blk, :].astype(BF16)
        c_terms = _bf16_split_const(LOG2E)
        n_tail = LANE - n_blk
        slope = jnp.exp2(jnp.full((1, 1), -8.0 / NH_B, F32) * (h + 1).astype(F32))
        sel_parts, tail_parts = [], []
        for cols in col_ranges:
            n_cols = cols.stop - cols.start
            gate = jnp.dot(kmean, qst[:, cols], preferred_element_type=F32)
            blk = lax.broadcasted_iota(jnp.int32, (n_blk, n_cols), 0)
            tok_blk = jnp.right_shift(lax.broadcasted_iota(jnp.int32, (1, n_cols), 1) + cols.start,
                                      T.bit_length() - 1)
            own = sblock * CHUNK_B + tok_blk
            valid = blk < own
            g = jnp.where(valid, gate, -jnp.inf)
            sel = jnp.zeros((n_blk, n_cols), F32)
            for _ in range(TOPK_B):
                mx = jnp.max(g, axis=0, keepdims=True)
                idx = jnp.min(jnp.where(g == mx, blk, n_blk), axis=0, keepdims=True)
                pick = blk == idx
                sel = jnp.where(pick, 1.0, sel)
                g = jnp.where(pick, -jnp.inf, g)
            keep = jnp.where(valid, sel, 0.0) + jnp.where(blk == own, 1.0, 0.0)
            sel_parts.append(jnp.where(keep > 0.0, 0.0, NEG_BIG).astype(BF16))

            v3 = -slope * (T * own).astype(F32) * LOG2E
            v3_hi = v3.astype(BF16).astype(F32)
            v3_r = v3 - v3_hi
            v3_lo = v3_r.astype(BF16).astype(F32)
            v3_terms = (v3_hi, v3_lo, v3_r - v3_lo)
            r = lax.broadcasted_iota(jnp.int32, (n_tail, n_cols), 0)
            tail = jnp.zeros((n_tail, n_cols), F32)
            for t in range(N_SPLIT):
                tail = jnp.where(r == t, c_terms[t], tail)
                tail = jnp.where(r == N_SPLIT + t, c_terms[t], tail)
                tail = jnp.where(r == 2 * N_SPLIT + t, v3_terms[t], tail)
            tail_parts.append(tail.astype(BF16))
        qaug_ref[slot, :DH_B, :] = qst
        qaug_ref[slot, DH_B:DH_B + n_blk, :] = jnp.concatenate(sel_parts, axis=1)
        qaug_ref[slot, DH_B + n_blk:, :] = jnp.concatenate(tail_parts, axis=1)

    def block_scores(slot, cid, b, cols):
        start = pl.multiple_of(cid * W + b * T, T)
        return jnp.dot(kaug_ref[pl.ds(start, T), :], qaug_ref[slot][:, cols],
                       preferred_element_type=F32)

    def own_block_scores(slot, sblock, b, cols):
        n_cols = cols.stop - cols.start
        key = lax.broadcasted_iota(jnp.int32, (T, n_cols), 0) + b * T
        tok = lax.broadcasted_iota(jnp.int32, (T, n_cols), 1) + cols.start
        return jnp.where(key > tok, NEG_BIG, block_scores(slot, sblock, b, cols))

    @pl.when(sb == 0)
    def _():
        write_qaug(0, qst0_ref[...], 0)
        for cols in col_ranges:
            cmax = None
            for b in range(CHUNK_B):
                sblk = own_block_scores(0, 0, b, cols)
                s_ref[b * T:(b + 1) * T, cols] = sblk
                mb = jnp.max(sblk, axis=0, keepdims=True)
                cmax = mb if cmax is None else jnp.maximum(cmax, mb)
            cmax_ref[:, cols] = cmax

    m_ref[...] = jnp.full(m_ref.shape, NEG_BIG, F32)
    acc_ref[...] = jnp.zeros_like(acc_ref)

    def step(cid, refill, after_block=None):
        m_prev = m_ref[...]
        m_new = jnp.maximum(m_prev, cmax_ref[...])
        alpha = jnp.exp2(m_prev - m_new)
        pv = [jnp.zeros((V_ROWS, c.stop - c.start), F32) for c in col_ranges]
        cmax = [None] * TOKEN_SPLIT
        for b in range(CHUNK_B):
            rows = slice(b * T, (b + 1) * T)
            for i, cols in enumerate(col_ranges):
                p = jnp.exp2(s_ref[rows, cols] - m_new[:, cols])
                pv[i] = pv[i] + jnp.dot(vt_ref[cid * CHUNK_B + b], p.astype(BF16), preferred_element_type=F32)
                nb = refill(b, cols)
                s_ref[rows, cols] = nb
                mb = jnp.max(nb, axis=0, keepdims=True)
                cmax[i] = mb if cmax[i] is None else jnp.maximum(cmax[i], mb)
            if after_block is not None:
                after_block()
        for i, cols in enumerate(col_ranges):
            acc_ref[:, cols] = alpha[:, cols] * acc_ref[:, cols] + pv[i]
            cmax_ref[:, cols] = cmax[i]
        m_ref[...] = m_new

    first = _first_live_chunk(gq_ref, gk_ref, cmax_ref, h, sb, W)

    def body(n, carry):
        step(jnp.where(n == first, sb, n - 1), lambda b, cols: block_scores(cur, n, b, cols))
        return carry

    lax.fori_loop(first, sb, body, 0)

    mlstm_heads = _mlstm_step((h == 0) & (sb == 0), qa_ref, ka_ref, va_ref, oa_ref, if_ref, gain_ref,
                              ha_ref, c_ref, ma_ref)
    nxt = jnp.minimum(sb + 1, n_sb - 1)
    write_qaug(1 - cur, qstn_ref[...], nxt)

    def run_mlstm_head():
        if mlstm_heads:
            mlstm_heads.pop(0)()

    step(jnp.where(sb == 0, sb, sb - 1), lambda b, cols: own_block_scores(1 - cur, nxt, b, cols), run_mlstm_head)
    while mlstm_heads:
        mlstm_heads.pop(0)()
    acc = acc_ref[...]
    o_ref[...] = (acc[:DH_B, :] / acc[DH_B:DH_B + 1, :]).T.astype(o_ref.dtype)


def _branches(qst, kaug, vt, kmean, gq, gk, proj, ifp, gain_h):
    S = qst.shape[2]
    T = BLOCK_B
    W = CHUNK_B * T
    L = CHUNK_A
    n_blk = S // T
    n_sb = n_blk // CHUNK_B
    assert n_blk % CHUNK_B == 0 and NH_B * n_sb * L == S
    kern = functools.partial(_branches_kernel, n_blk=n_blk)

    def chunk(col):
        return lambda h, s: (h * n_sb + s, col)

    return pl.pallas_call(
        kern,
        grid=(NH_B, n_sb),
        in_specs=[
            pl.BlockSpec((None, DH_B, W), lambda h, s: (h, 0, 0)),
            pl.BlockSpec((None, DH_B, W), lambda h, s: (h, 0, jnp.minimum(s + 1, n_sb - 1))),
            pl.BlockSpec((None, S, DH_B + LANE), lambda h, s: (h, 0, 0)),
            pl.BlockSpec((None, n_blk, V_ROWS, T), lambda h, s: (h, 0, 0, 0)),
            pl.BlockSpec((LANE, DH_B), lambda h, s: (0, h)),
            pl.BlockSpec((1, DH_B), lambda h, s: (0, 0)),
            pl.BlockSpec((1, DH_B), lambda h, s: (0, 0)),
            pl.BlockSpec((L, QK_A), chunk(C_QA // QK_A)),
            pl.BlockSpec((L, QK_A), chunk(C_KA // QK_A)),
            pl.BlockSpec((L, W_A), chunk(C_VA // W_A)),
            pl.BlockSpec((L, W_A), chunk(C_OA // W_A)),
            pl.BlockSpec((L, IF_PAD), chunk(0)),
            pl.BlockSpec((1, W_A), lambda h, s: (0, 0)),
        ],
        out_specs=[
            pl.BlockSpec((W, DH_B), lambda h, s: (s, h)),
            pl.BlockSpec((L, W_A), chunk(0)),
        ],
        out_shape=[
            jax.ShapeDtypeStruct((S, W_B), BF16),
            jax.ShapeDtypeStruct((S, W_A), BF16),
        ],
        scratch_shapes=[
            pltpu.VMEM((2, DH_B + LANE, W), BF16),
            pltpu.VMEM((W, W), F32),
            pltpu.VMEM((1, W), F32),
            pltpu.VMEM((1, W), F32),
            pltpu.VMEM((V_ROWS, W), F32),
            pltpu.VMEM((NH_A, DQK_A, DV_A + LANE), F32),
            pltpu.VMEM((SUBLANE, LANE), F32),
        ],
        compiler_params=_cparams(("arbitrary", "arbitrary")),
        name="branches",
    )(qst, qst, kaug, vt, kmean, gq, gk, proj, proj, proj, proj, ifp, gain_h)


def _merge_out_kernel(x_ref, ha_ref, hb_ref, ga_ref, gb_ref, wa_ref, wb_ref, wo_ref, o_ref):
    ta = jnp.dot(ha_ref[...], wa_ref[...], preferred_element_type=F32)
    tb = jnp.dot(hb_ref[...], wb_ref[...], preferred_element_type=F32)
    merged = (jax.nn.sigmoid(ga_ref[...].astype(F32)) * ta
              + jax.nn.sigmoid(gb_ref[...].astype(F32)) * tb).astype(BF16)
    o_ref[...] = x_ref[...] + jnp.dot(merged, wo_ref[...], preferred_element_type=F32)


def _merge_out(x, h_a, h_b, proj, w_a, w_b, w_o, layer, tm):
    S, D = x.shape
    const = dict(pipeline_mode=pl.Buffered(1))
    return pl.pallas_call(
        _merge_out_kernel,
        grid=(S // tm,),
        in_specs=[
            pl.BlockSpec((tm, D), lambda m: (m, 0)),
            pl.BlockSpec((tm, W_A), lambda m: (m, 0)),
            pl.BlockSpec((tm, W_B), lambda m: (m, 0)),
            pl.BlockSpec((tm, D), lambda m: (m, C_G // D)),
            pl.BlockSpec((tm, D), lambda m: (m, C_G // D + 1)),
            pl.BlockSpec((None, W_A, D), lambda m: (layer, 0, 0), **const),
            pl.BlockSpec((None, W_B, D), lambda m: (layer, 0, 0), **const),
            pl.BlockSpec((None, D, D), lambda m: (layer, 0, 0), **const),
        ],
        out_specs=pl.BlockSpec((tm, D), lambda m: (m, 0)),
        out_shape=jax.ShapeDtypeStruct((S, D), F32),
        compiler_params=_cparams(("parallel",)),
        name="merge_out",
    )(x, h_a, h_b, proj, proj, w_a, w_b, w_o)


def _mlp_kernel(x_ref, g_ref, wu_ref, wd_ref, o_ref, hn_ref):
    @pl.when(pl.program_id(1) == 0)
    def _():
        x = x_ref[...]
        hn_ref[...] = _rms(x, g_ref[...]).astype(BF16)
        o_ref[...] = x

    u = jnp.dot(hn_ref[...], wu_ref[...], preferred_element_type=F32)
    a = jnp.square(jnp.maximum(u, 0.0)).astype(BF16)
    o_ref[...] += jnp.dot(a, wd_ref[...], preferred_element_type=F32)


def _mlp(x, gain, w_up, w_down, layer, tm, tf):
    S, D = x.shape
    FF = w_up.shape[2]
    return pl.pallas_call(
        _mlp_kernel,
        grid=(S // tm, FF // tf),
        in_specs=[
            pl.BlockSpec((tm, D), lambda m, f: (m, 0)),
            pl.BlockSpec((1, D), lambda m, f: (0, 0)),
            pl.BlockSpec((None, D, tf), lambda m, f: (layer, 0, f)),
            pl.BlockSpec((None, tf, D), lambda m, f: (layer, f, 0)),
        ],
        out_specs=pl.BlockSpec((tm, D), lambda m, f: (m, 0)),
        out_shape=jax.ShapeDtypeStruct((S, D), F32),
        scratch_shapes=[pltpu.VMEM((tm, D), BF16)],
        compiler_params=_cparams(("parallel", "arbitrary"), BIG_TILE_VMEM_LIMIT),
        name="mlp",
    )(x, gain, w_up, w_down)


IN_PROJ_TM, IN_PROJ_TN = 1024, 2560
MERGE_TM = 512
MLP_TM, MLP_TF = 512, 2048
BIG_TILE_VMEM_LIMIT = 60 * 1024 * 1024


def _tile(n, pref):
    t = min(n, pref)
    while n % t:
        t -= LANE
    assert t > 0
    return t


def kernel(x, norm_mix, w_in, b_if, norm_h_mlstm, norm_q_moba, norm_k_moba,
           w_branch_a, w_branch_b, w_out, norm_mlp, w_up, w_down):
    B, S, D = x.shape
    assert B == 1 and S % BLOCK_B == 0 and D % LANE == 0
    depth = w_in.shape[0]
    c_if = C_QB
    n_main = w_in.shape[2] - N_IF
    assert n_main == C_G + 2 * D

    w_main, w_if = _w_repack(jnp.swapaxes(w_in, 1, 2), c_if, _tile(D, 256))
    w_a, w_b, w_o = w_branch_a.astype(BF16), w_branch_b.astype(BF16), w_out.astype(BF16)
    w_u, w_d = w_up.astype(BF16), w_down.astype(BF16)

    xs = x.reshape(S, D)
    for l in range(depth):
        bias_if = jnp.pad(b_if[l].astype(F32), (0, IF_PAD - N_IF)).reshape(1, IF_PAD)
        proj, ifp = _in_proj(xs, norm_mix[l].reshape(1, D), w_main, w_if, bias_if, l,
                             _tile(S, IN_PROJ_TM), _tile(n_main, IN_PROJ_TN))
        gq, gk = norm_q_moba[l].reshape(1, DH_B), norm_k_moba[l].reshape(1, DH_B)
        qst, kaug, vt, kmean = _moba_prep(proj, gq, gk)
        h_b, h_a = _branches(qst, kaug, vt, kmean, gq, gk, proj, ifp, norm_h_mlstm[l].reshape(1, W_A))
        xs = _merge_out(xs, h_a, h_b, proj, w_a, w_b, w_o, l, _tile(S, MERGE_TM))
        xs = _mlp(xs, norm_mlp[l].reshape(1, D), w_u, w_d, l, _tile(S, MLP_TM), _tile(w_up.shape[2], MLP_TF))
    return xs.reshape(B, S, D)
```
